```python
import jax, jax.numpy as jnp
from jax import lax
import numpy as np

D_MODEL = 2048
BATCH = 4
SEQ = 4096
DEPTH = 2

A_WIDTH = D_MODEL // 2
A_HEADS = 8
A_HEAD_DIM = A_WIDTH // A_HEADS
GMLP_CHUNK = 128
B_WIDTH = D_MODEL // 2
B_HEAD_DIM = 128
B_HEADS = B_WIDTH // B_HEAD_DIM
HGRN_CHUNK = 64
C_HEAD_DIM = 128
C_HEADS = D_MODEL // C_HEAD_DIM
ATTN_BLOCK = 128
FFN_HIDDEN = ((8 * D_MODEL // 3 + 255) // 256) * 256
CONV_WIDTH = 3
RMS_EPS = 1e-6
N_AB = (DEPTH + 1) // 2
N_C = DEPTH // 2
AB_IN = 2 * A_WIDTH + 4 * B_WIDTH
C_IN = 4 * D_MODEL + C_HEADS

kernel_name = "hybrid_gmlp_hgrn2_fox_convffn"


def rms_norm(x, gain):
    xf = x.astype(jnp.float32)
    y = xf * lax.rsqrt(jnp.mean(xf * xf, axis=-1, keepdims=True) + RMS_EPS)
    return (y * gain.astype(jnp.float32)).astype(x.dtype)


def hgrn2(q, f_logit, i, g, lower_bound, o_gain):
    bsz, seq, _ = q.shape
    dt = q.dtype
    f32 = jnp.float32
    shp = (bsz, seq, B_HEADS, B_HEAD_DIM)
    fl = f_logit.astype(f32).reshape(shp)
    lb = lower_bound.astype(f32).reshape(B_HEADS, B_HEAD_DIM)
    log_f = jnp.logaddexp(jnp.log(lb), jnp.log1p(-lb) + jax.nn.log_sigmoid(fl))
    k = (1.0 - lb) * jax.nn.sigmoid(-fl)
    nc = seq // HGRN_CHUNK

    def chunks(t):
        return t.reshape(bsz, nc, HGRN_CHUNK, B_HEADS, -1).transpose(1, 0, 3, 2, 4)

    qc = chunks(q.astype(f32).reshape(shp))
    kc = chunks(k)
    vc = chunks(i.astype(f32).reshape(shp))
    lfc = chunks(log_f)
    mask = jnp.tril(jnp.ones((HGRN_CHUNK, HGRN_CHUNK), dtype=bool))

    def step(state, xs):
        q_c, k_c, v_c, lf_c = xs
        G = jnp.cumsum(lf_c, axis=2)
        o_inter = jnp.einsum('bhtk,bhkv->bhtv', q_c * jnp.exp(G), state)
        diff = G[:, :, :, None, :] - G[:, :, None, :, :]
        decay = jnp.exp(jnp.where(mask[:, :, None], diff, -jnp.inf))
        scores = jnp.einsum('bhtk,bhtsk,bhsk->bhts', q_c, decay, k_c)
        o_intra = jnp.einsum('bhts,bhsv->bhtv', scores, v_c)
        G_end = G[:, :, -1:, :]
        new_state = (jnp.exp(G_end[:, :, 0, :])[..., None] * state
                     + jnp.einsum('bhsk,bhsv->bhkv', k_c * jnp.exp(G_end - G), v_c))
        return new_state, o_inter + o_intra

    s0 = jnp.zeros((bsz, B_HEADS, B_HEAD_DIM, B_HEAD_DIM), f32)
    _, o = lax.scan(step, s0, (qc, kc, vc, lfc))
    o = o.transpose(1, 0, 3, 2, 4).reshape(shp)
    o = rms_norm(o, o_gain) * jax.nn.silu(g.astype(f32).reshape(shp))
    return o.reshape(bsz, seq, B_WIDTH).astype(dt)


def mixer_ab(h, w_in, sp_w, sp_b, v_gain, lower_bound, o_gain, w_out):
    bsz, seq, _ = h.shape
    cuts = [A_WIDTH, 2 * A_WIDTH, 2 * A_WIDTH + B_WIDTH,
            2 * A_WIDTH + 2 * B_WIDTH, 2 * A_WIDTH + 3 * B_WIDTH]
    u, v, q, f, i, g = jnp.split(h @ w_in, cuts, axis=-1)
    u = jax.nn.gelu(u, approximate=False)
    v = jax.nn.gelu(v, approximate=False)
    v = rms_norm(v.reshape(bsz, seq, A_HEADS, A_HEAD_DIM), v_gain.reshape(A_HEADS, A_HEAD_DIM))
    v = v.reshape(bsz, seq // GMLP_CHUNK, GMLP_CHUNK, A_HEADS, A_HEAD_DIM)
    w_causal = sp_w * jnp.tril(jnp.ones((GMLP_CHUNK, GMLP_CHUNK), sp_w.dtype))
    mixed = jnp.einsum('hts,bcshd->bcthd', w_causal, v) + sp_b.T[:, :, None]
    y_a = u * mixed.reshape(bsz, seq, A_WIDTH)
    y_b = hgrn2(q, f, i, g, lower_bound, o_gain)
    return jnp.concatenate([y_a, y_b], axis=-1) @ w_out


def mixer_c(h, w_in, b_f, q_gain, k_gain, w_out):
    bsz, seq, _ = h.shape
    shp = (bsz, seq, C_HEADS, C_HEAD_DIM)
    q, k, v, g, f = jnp.split(h @ w_in, [D_MODEL, 2 * D_MODEL, 3 * D_MODEL, 4 * D_MODEL], axis=-1)
    q = rms_norm(q.reshape(shp), q_gain)
    k = rms_norm(k.reshape(shp), k_gain)
    v = v.reshape(shp)
    log_f = jax.nn.log_sigmoid(f.astype(jnp.float32) + b_f.astype(jnp.float32))
    c = jnp.cumsum(log_f, axis=1).transpose(0, 2, 1)
    scale = C_HEAD_DIM ** -0.5
    outs = []
    for blk in range(seq // ATTN_BLOCK):
        lo, hi = blk * ATTN_BLOCK, (blk + 1) * ATTN_BLOCK
        s = jnp.einsum('bthd,bshd->bhts', q[:, lo:hi], k[:, :hi]).astype(jnp.float32) * scale
        s = s + c[:, :, lo:hi, None] - c[:, :, None, :hi]
        causal = (lo + jnp.arange(ATTN_BLOCK))[:, None] >= jnp.arange(hi)[None, :]
        p = jax.nn.softmax(jnp.where(causal, s, -jnp.inf), axis=-1).astype(v.dtype)
        outs.append(jnp.einsum('bhts,bshd->bthd', p, v[:, :hi]))
    o = jnp.concatenate(outs, axis=1).reshape(bsz, seq, D_MODEL) * jax.nn.sigmoid(g)
    return o @ w_out


def conv_ffn(h, w_up, conv_w, conv_b, w_down):
    seq = h.shape[1]
    z = h @ w_up
    zp = jnp.pad(z, ((0, 0), (CONV_WIDTH - 1, 0), (0, 0)))
    z = sum(conv_w[j] * zp[:, j:j + seq] for j in range(CONV_WIDTH)) + conv_b
    a, b = jnp.split(z, 2, axis=-1)
    return (jax.nn.silu(a) * b) @ w_down


def setup_inputs(seed: int = 0) -> dict:
    key = jax.random.key(seed)
    ks = jax.random.split(key, 20)
    f32 = jnp.float32

    def dense(k, shape, fan_in):
        return jax.random.normal(k, shape, f32) * (fan_in ** -0.5)

    def gain(k, shape):
        return 1.0 + 0.02 * jax.random.normal(k, shape, f32)

    def small(k, shape, s=0.02):
        return s * jax.random.normal(k, shape, f32)

    return {
        "x": jax.random.normal(ks[0], (BATCH, SEQ, D_MODEL), f32),
        "mix_norm": gain(ks[1], (DEPTH, D_MODEL)),
        "ab_w_in": dense(ks[2], (N_AB, D_MODEL, AB_IN), D_MODEL),
        "ab_sp_w": dense(ks[3], (N_AB, A_HEADS, GMLP_CHUNK, GMLP_CHUNK), GMLP_CHUNK),
        "ab_sp_b": small(ks[4], (N_AB, A_HEADS, GMLP_CHUNK)),
        "ab_v_norm": gain(ks[5], (N_AB, A_WIDTH)),
        "hgrn_gamma": small(ks[6], (DEPTH + 1, B_WIDTH)),
        "hgrn_o_norm": gain(ks[7], (N_AB, B_HEAD_DIM)),
        "ab_w_out": dense(ks[8], (N_AB, A_WIDTH + B_WIDTH, D_MODEL), A_WIDTH + B_WIDTH),
        "c_w_in": dense(ks[9], (N_C, D_MODEL, C_IN), D_MODEL),
        "c_b_f": small(ks[10], (N_C, C_HEADS), 0.1),
        "c_q_norm": gain(ks[11], (N_C, C_HEAD_DIM)),
        "c_k_norm": gain(ks[12], (N_C, C_HEAD_DIM)),
        "c_w_out": dense(ks[13], (N_C, D_MODEL, D_MODEL), D_MODEL),
        "ffn_norm": gain(ks[14], (DEPTH, D_MODEL)),
        "ffn_w_up": dense(ks[15], (DEPTH, D_MODEL, 2 * FFN_HIDDEN), D_MODEL),
        "ffn_conv_w": dense(ks[16], (DEPTH, CONV_WIDTH, 2 * FFN_HIDDEN), CONV_WIDTH),
        "ffn_conv_b": small(ks[17], (DEPTH, 2 * FFN_HIDDEN)),
        "ffn_w_down": dense(ks[18], (DEPTH, FFN_HIDDEN, D_MODEL), FFN_HIDDEN),
    }


def reference(x, mix_norm, ab_w_in, ab_sp_w, ab_sp_b, ab_v_norm, hgrn_gamma, hgrn_o_norm,
              ab_w_out, c_w_in, c_b_f, c_q_norm, c_k_norm, c_w_out, ffn_norm, ffn_w_up,
              ffn_conv_w, ffn_conv_b, ffn_w_down):
    lb_table = jnp.cumsum(jax.nn.softmax(hgrn_gamma.astype(jnp.float32), axis=0), axis=0)
    for l in range(DEPTH):
        j = l // 2
        h = rms_norm(x, mix_norm[l])
        if l % 2 == 0:
            h = mixer_ab(h, ab_w_in[j], ab_sp_w[j], ab_sp_b[j], ab_v_norm[j],
                         lb_table[l], hgrn_o_norm[j], ab_w_out[j])
        else:
            h = mixer_c(h, c_w_in[j], c_b_f[j], c_q_norm[j], c_k_norm[j], c_w_out[j])
        x = x + h
        h = rms_norm(x, ffn_norm[l])
        x = x + conv_ffn(h, ffn_w_up[l], ffn_conv_w[l], ffn_conv_b[l], ffn_w_down[l])
    return x
```

```python
import functools
import math

import jax
import jax.numpy as jnp
from jax import lax
from jax.experimental import pallas as pl
from jax.experimental.pallas import tpu as pltpu

F32 = jnp.float32
BF16 = jnp.bfloat16
RMS_EPS = 1e-6
LANES = 128
HGRN_CHUNK = 64
V7X_VMEM_BYTES = 64 * 1024 * 1024
VMEM_CAP_BYTES = V7X_VMEM_BYTES - 6 * 1024 * 1024


def _nbytes(shape, dtype):
    return math.prod(shape) * jnp.dtype(dtype).itemsize


def _vmem_limit(blocks, scratch=(), temps=()):
    est = 2 * sum(_nbytes(s, d) for s, d in blocks)
    est += sum(_nbytes(s, d) for s, d in scratch) + sum(_nbytes(s, d) for s, d in temps)
    return int(min(VMEM_CAP_BYTES, est * 5 // 4 + (4 << 20)))


def _tile(n, pref, mult=LANES):
    if n <= pref:
        return n
    t = (pref // mult) * mult
    while n % t:
        t -= mult
    return t


def _params(sem, limit):
    return pltpu.CompilerParams(dimension_semantics=sem, vmem_limit_bytes=limit)


def _rms_rows(xf, gain):
    ms = jnp.mean(xf * xf, axis=-1, keepdims=True)
    return xf * lax.rsqrt(ms + RMS_EPS) * gain


def _gelu(x):
    return 0.5 * x * (1.0 + lax.erf(x * (2.0 ** -0.5)))


def _fill_normed(x_ref, gain_ref, h_ref, rows=256):
    rows = min(rows, x_ref.shape[0])

    def body(r, carry):
        sl = pl.ds(pl.multiple_of(r * rows, rows), rows)
        h_ref[sl, :] = _rms_rows(x_ref[sl, :], gain_ref[...]).astype(h_ref.dtype)
        return carry

    lax.fori_loop(0, x_ref.shape[0] // rows, body, 0)


def _tril_mask(n):
    t = lax.broadcasted_iota(jnp.int32, (n, n), 0)
    s = lax.broadcasted_iota(jnp.int32, (n, n), 1)
    return s <= t


def _cumsum_rows(x, tril_bf16):
    hi = x.astype(BF16)
    r1 = x - hi.astype(F32)
    mid = r1.astype(BF16)
    lo = (r1 - mid.astype(F32)).astype(BF16)
    y = jnp.dot(tril_bf16, jnp.concatenate([hi, mid, lo], axis=1), preferred_element_type=F32)
    d = x.shape[1]
    return y[:, :d] + y[:, d:2 * d] + y[:, 2 * d:]


def _dot_nt(a, b):
    return lax.dot_general(a, b, (((1,), (1,)), ((), ())), preferred_element_type=F32)


def _dot_tn(a, b):
    return lax.dot_general(a, b, (((0,), (0,)), ((), ())), preferred_element_type=F32)


def _gmlp_kernel(x_ref, gain_ref, wu_ref, wv_ref, vg_ref, spw_ref, spb_ref, o_ref, h_ref, *, chunk):
    @pl.when(pl.program_id(1) == 0)
    def _():
        _fill_normed(x_ref, gain_ref, h_ref)

    h = h_ref[...]
    u = _gelu(jnp.dot(h, wu_ref[...], preferred_element_type=F32))
    v = _gelu(jnp.dot(h, wv_ref[...], preferred_element_type=F32))
    tm, tn = u.shape
    n_chunks = tm // chunk
    tril = _tril_mask(chunk)
    for hh in range(tn // LANES):
        lanes = slice(hh * LANES, (hh + 1) * LANES)
        vh = _rms_rows(v[:, lanes], vg_ref[:, lanes]).astype(BF16)
        vcat = jnp.concatenate([vh[c * chunk:(c + 1) * chunk, :] for c in range(n_chunks)], axis=1)
        w_causal = jnp.where(tril, spw_ref[hh], 0.0).astype(BF16)
        mixed = jnp.dot(w_causal, vcat, preferred_element_type=F32)
        for c in range(n_chunks):
            rows = slice(c * chunk, (c + 1) * chunk)
            m_c = mixed[:, c * LANES:(c + 1) * LANES] + spb_ref[hh]
            o_ref[rows, lanes] = (u[rows, lanes] * m_c).astype(o_ref.dtype)


def _gmlp_proj(x2d, gain, w_in, v_gain, sp_w, sp_b, seq):
    T, D = x2d.shape
    n_heads, chunk, _ = sp_w.shape
    a_width = n_heads * LANES
    tm = _tile(seq, 1024, chunk)
    tn = _tile(a_width, 256)
    nj = a_width // tn
    spb = jnp.broadcast_to(sp_b[:, :, None], (n_heads, chunk, LANES))
    blocks = [((tm, D), F32), ((1, D), F32), ((D, tn), BF16), ((D, tn), BF16), ((1, tn), F32),
              ((tn // LANES, chunk, chunk), F32), ((tn // LANES, chunk, LANES), F32), ((tm, tn), BF16)]
    return pl.pallas_call(
        functools.partial(_gmlp_kernel, chunk=chunk),
        grid=(T // tm, nj),
        in_specs=[
            pl.BlockSpec((tm, D), lambda i, j: (i, 0)),
            pl.BlockSpec((1, D), lambda i, j: (0, 0)),
            pl.BlockSpec((D, tn), lambda i, j: (0, j)),
            pl.BlockSpec((D, tn), lambda i, j: (0, j + nj)),
            pl.BlockSpec((1, tn), lambda i, j: (0, j)),
            pl.BlockSpec((tn // LANES, chunk, chunk), lambda i, j: (j, 0, 0)),
            pl.BlockSpec((tn // LANES, chunk, LANES), lambda i, j: (j, 0, 0)),
        ],
        out_specs=pl.BlockSpec((tm, tn), lambda i, j: (i, j)),
        out_shape=jax.ShapeDtypeStruct((T, a_width), BF16),
        scratch_shapes=[pltpu.VMEM((tm, D), BF16)],
        compiler_params=_params(("parallel", "arbitrary"),
                                _vmem_limit(blocks, [((tm, D), BF16)], [((tm, tn), F32)] * 6)),
        name="gmlp_proj",
    )(x2d, gain, w_in, w_in, v_gain, sp_w, spb)


def _norm_mm_kernel(x_ref, gain_ref, w_ref, o_ref, h_ref):
    @pl.when(pl.program_id(1) == 0)
    def _():
        _fill_normed(x_ref, gain_ref, h_ref)

    o_ref[...] = jnp.dot(h_ref[...], w_ref[...], preferred_element_type=F32).astype(o_ref.dtype)


def _norm_matmul(x2d, gain, w, col0, n_cols, seq, out_dtype, name):
    T, D = x2d.shape
    tm = _tile(seq, 1024)
    tn = _tile(n_cols, 1024)
    assert col0 % tn == 0
    j0 = col0 // tn
    blocks = [((tm, D), F32), ((1, D), F32), ((D, tn), BF16), ((tm, tn), out_dtype)]
    return pl.pallas_call(
        _norm_mm_kernel,
        grid=(T // tm, n_cols // tn),
        in_specs=[
            pl.BlockSpec((tm, D), lambda i, j: (i, 0)),
            pl.BlockSpec((1, D), lambda i, j: (0, 0)),
            pl.BlockSpec((D, tn), lambda i, j: (0, j + j0)),
        ],
        out_specs=pl.BlockSpec((tm, tn), lambda i, j: (i, j)),
        out_shape=jax.ShapeDtypeStruct((T, n_cols), out_dtype),
        scratch_shapes=[pltpu.VMEM((tm, D), BF16)],
        compiler_params=_params(("parallel", "arbitrary"),
                                _vmem_limit(blocks, [((tm, D), BF16)], [((tm, tn), F32)])),
        name=name,
    )(x2d, gain, w)


def _hgrn_kernel(gamma_ref, q_ref, f_ref, i_ref, g_ref, og_ref, o_ref, st_ref, *, layer, chunk):
    @pl.when(pl.program_id(2) == 0)
    def _():
        st_ref[...] = jnp.zeros_like(st_ref)

    gam = gamma_ref[...]
    ex = jnp.exp(gam - jnp.max(gam, axis=0, keepdims=True))
    lb_all = jnp.sum(ex[:layer + 1], axis=0, keepdims=True) / jnp.sum(ex, axis=0, keepdims=True)

    L, width = q_ref.shape
    mask = _tril_mask(chunk)
    tril = mask.astype(BF16)
    mid = chunk // 2 - 1
    for hh in range(width // LANES):
        lanes = slice(hh * LANES, (hh + 1) * LANES)
        lb = lb_all[:, lanes]
        for cc in range(L // chunk):
            rows = slice(cc * chunk, (cc + 1) * chunk)
            fl = f_ref[rows, lanes]
            e = jnp.exp(-jnp.abs(fl))
            r = 1.0 / (1.0 + e)
            pos = fl >= 0
            sig = jnp.where(pos, r, e * r)
            nsig = jnp.where(pos, e * r, r)
            log_f = jnp.log(lb + (1.0 - lb) * sig)
            kk = (1.0 - lb) * nsig
            G = _cumsum_rows(log_f, tril)
            g_mid = G[mid:mid + 1, :]
            g_end = G[chunk - 1:chunk, :]
            qv = q_ref[rows, lanes]
            vv = i_ref[rows, lanes].astype(BF16)
            q_t = (qv * jnp.exp(G - g_mid)).astype(BF16)
            k_t = (kk * jnp.exp(g_mid - G)).astype(BF16)
            scores = jnp.where(mask, _dot_nt(q_t, k_t), 0.0).astype(BF16)
            st = st_ref[hh]
            o = jnp.dot(scores, vv, preferred_element_type=F32)
            o = o + _dot_nt((qv * jnp.exp(G)).astype(BF16), st.astype(BF16))
            k_end = (kk * jnp.exp(g_end - G)).astype(BF16)
            st_ref[hh] = st * jnp.exp(g_end) + _dot_tn(vv, k_end)
            gv = g_ref[rows, lanes]
            y = _rms_rows(o, og_ref[...]) * (gv * jax.nn.sigmoid(gv))
            o_ref[rows, lanes] = y.astype(o_ref.dtype)


def _hgrn(qfig, gamma, o_gain, layer, batch, seq):
    T = qfig.shape[0]
    b_width = gamma.shape[1]
    n_layers = gamma.shape[0]
    width = _tile(b_width, 256)
    L = _tile(seq, 256, HGRN_CHUNK)
    nw = b_width // width
    nl = seq // L
    blocks = [((L, width), F32)] * 4 + [((L, width), BF16), ((n_layers, width), F32)]
    row = lambda b, h, l: b * nl + l
    return pl.pallas_call(
        functools.partial(_hgrn_kernel, layer=layer, chunk=HGRN_CHUNK),
        grid=(batch, nw, nl),
        in_specs=[
            pl.BlockSpec((n_layers, width), lambda b, h, l: (0, h)),
            pl.BlockSpec((L, width), lambda b, h, l: (row(b, h, l), h)),
            pl.BlockSpec((L, width), lambda b, h, l: (row(b, h, l), h + nw)),
            pl.BlockSpec((L, width), lambda b, h, l: (row(b, h, l), h + 2 * nw)),
            pl.BlockSpec((L, width), lambda b, h, l: (row(b, h, l), h + 3 * nw)),
            pl.BlockSpec((1, LANES), lambda b, h, l: (0, 0)),
        ],
        out_specs=pl.BlockSpec((L, width), lambda b, h, l: (row(b, h, l), h)),
        out_shape=jax.ShapeDtypeStruct((T, b_width), BF16),
        scratch_shapes=[pltpu.VMEM((width // LANES, LANES, LANES), F32)],
        compiler_params=_params(("parallel", "parallel", "arbitrary"),
                                _vmem_limit(blocks, [((width // LANES, LANES, LANES), F32)],
                                            [((L, width), F32)] * 8)),
        name="hgrn2",
    )(gamma, qfig, qfig, qfig, qfig, o_gain)


def _mm_resid_kernel(*refs, n_lhs):
    lhs_refs = refs[:n_lhs]
    w_ref, r_ref, o_ref = refs[n_lhs:]
    acc = r_ref[...]
    k0 = 0
    for a_ref in lhs_refs:
        k = a_ref.shape[1]
        acc = acc + jnp.dot(a_ref[...], w_ref[k0:k0 + k, :], preferred_element_type=F32)
        k0 += k
    o_ref[...] = acc


def _matmul_residual(lhs_list, w, resid, seq, name, tn_pref=512):
    T, N = resid.shape
    K = w.shape[0]
    tm = _tile(seq, 1024)
    tn = _tile(N, tn_pref)
    blocks = [((tm, a.shape[1]), BF16) for a in lhs_list] + [((K, tn), BF16), ((tm, tn), F32), ((tm, tn), F32)]
    return pl.pallas_call(
        functools.partial(_mm_resid_kernel, n_lhs=len(lhs_list)),
        grid=(T // tm, N // tn),
        in_specs=[pl.BlockSpec((tm, a.shape[1]), lambda i, j: (i, 0)) for a in lhs_list] + [
            pl.BlockSpec((K, tn), lambda i, j: (0, j)),
            pl.BlockSpec((tm, tn), lambda i, j: (i, j)),
        ],
        out_specs=pl.BlockSpec((tm, tn), lambda i, j: (i, j)),
        out_shape=jax.ShapeDtypeStruct((T, N), F32),
        compiler_params=_params(("parallel", "arbitrary"), _vmem_limit(blocks, [], [((tm, tn), F32)])),
        name=name,
    )(*lhs_list, w, resid)


def _ffn_up_kernel(x_ref, xh_ref, gain_ref, wa_ref, wb_ref, cwa_ref, cwb_ref, cba_ref, cbb_ref,
                   o_ref, h_ref, *, tiles_per_seq, halo):
    i = pl.program_id(0)

    @pl.when(pl.program_id(1) == 0)
    def _():
        keep = (i % tiles_per_seq != 0).astype(F32)
        h_ref[0:halo, :] = (_rms_rows(xh_ref[...], gain_ref[...]) * keep).astype(h_ref.dtype)
        tm = x_ref.shape[0]
        rows = min(256, tm)

        def body(r, carry):
            src = pl.ds(pl.multiple_of(r * rows, rows), rows)
            dst = pl.ds(pl.multiple_of(r * rows + halo, halo), rows)
            h_ref[dst, :] = _rms_rows(x_ref[src, :], gain_ref[...]).astype(h_ref.dtype)
            return carry

        lax.fori_loop(0, tm // rows, body, 0)

    h = h_ref[...]

    def conv(w_ref, cw_ref, cb_ref):
        z = jnp.dot(h, w_ref[...], preferred_element_type=F32)
        z1 = pltpu.roll(z, 1, 0)
        z2 = pltpu.roll(z, 2, 0)
        y = cw_ref[0:1, :] * z2 + cw_ref[1:2, :] * z1 + cw_ref[2:3, :] * z + cb_ref[...]
        return y[halo:, :]

    a = conv(wa_ref, cwa_ref, cba_ref)
    b = conv(wb_ref, cwb_ref, cbb_ref)
    o_ref[...] = (a * jax.nn.sigmoid(a) * b).astype(o_ref.dtype)


def _ffn_up(x2d, gain, w_up, conv_w, conv_b, seq):
    T, D = x2d.shape
    F = w_up.shape[1] // 2
    halo = 16
    tm = _tile(seq, 1024)
    tn = _tile(F, 512)
    nj = F // tn
    tiles_per_seq = seq // tm
    blocks = [((tm, D), F32), ((halo, D), F32), ((1, D), F32), ((D, tn), BF16), ((D, tn), BF16),
              ((3, tn), F32), ((3, tn), F32), ((1, tn), F32), ((1, tn), F32), ((tm, tn), BF16)]
    hpt = tm // halo
    return pl.pallas_call(
        functools.partial(_ffn_up_kernel, tiles_per_seq=tiles_per_seq, halo=halo),
        grid=(T // tm, nj),
        in_specs=[
            pl.BlockSpec((tm, D), lambda i, j: (i, 0)),
            pl.BlockSpec((halo, D), lambda i, j: (jnp.maximum(i * hpt - 1, 0), 0)),
            pl.BlockSpec((1, D), lambda i, j: (0, 0)),
            pl.BlockSpec((D, tn), lambda i, j: (0, j)),
            pl.BlockSpec((D, tn), lambda i, j: (0, j + nj)),
            pl.BlockSpec((3, tn), lambda i, j: (0, j)),
            pl.BlockSpec((3, tn), lambda i, j: (0, j + nj)),
            pl.BlockSpec((1, tn), lambda i, j: (0, j)),
            pl.BlockSpec((1, tn), lambda i, j: (0, j + nj)),
        ],
        out_specs=pl.BlockSpec((tm, tn), lambda i, j: (i, j)),
        out_shape=jax.ShapeDtypeStruct((T, F), BF16),
        scratch_shapes=[pltpu.VMEM((tm + halo, D), BF16)],
        compiler_params=_params(("parallel", "arbitrary"),
                                _vmem_limit(blocks, [((tm + halo, D), BF16)], [((tm + halo, tn), F32)] * 8)),
        name="ffn_up_conv_gate",
    )(x2d, x2d, gain, w_up, w_up, conv_w, conv_w, conv_b, conv_b)


def _conv_ffn(x2d, gain, w_up, conv_w, conv_b, w_down, seq):
    act = _ffn_up(x2d, gain, w_up, conv_w, conv_b, seq)
    return _matmul_residual([act], w_down, x2d, seq, "ffn_down", tn_pref=256)


def _qkvg_kernel(x_ref, gain_ref, w_ref, hg_ref, o_ref, h_ref, *, n_norm_tiles):
    j = pl.program_id(1)

    @pl.when(j == 0)
    def _():
        _fill_normed(x_ref, gain_ref, h_ref)

    y = jnp.dot(h_ref[...], w_ref[...], preferred_element_type=F32)

    @pl.when(j < n_norm_tiles)
    def _():
        for grp in range(y.shape[1] // LANES):
            lanes = slice(grp * LANES, (grp + 1) * LANES)
            o_ref[:, lanes] = _rms_rows(y[:, lanes], hg_ref[:, lanes]).astype(o_ref.dtype)

    @pl.when(j >= n_norm_tiles)
    def _():
        o_ref[...] = y.astype(o_ref.dtype)


def _qkvg_proj(x2d, gain, w, head_gain, n_cols, n_norm_cols, seq):
    T, D = x2d.shape
    tm = _tile(seq, 1024)
    tn = _tile(n_norm_cols // 2, 1024)
    blocks = [((tm, D), F32), ((1, D), F32), ((D, tn), BF16), ((1, tn), F32), ((tm, tn), BF16)]
    return pl.pallas_call(
        functools.partial(_qkvg_kernel, n_norm_tiles=n_norm_cols // tn),
        grid=(T // tm, n_cols // tn),
        in_specs=[
            pl.BlockSpec((tm, D), lambda i, j: (i, 0)),
            pl.BlockSpec((1, D), lambda i, j: (0, 0)),
            pl.BlockSpec((D, tn), lambda i, j: (0, j)),
            pl.BlockSpec((1, tn), lambda i, j: (0, j)),
        ],
        out_specs=pl.BlockSpec((tm, tn), lambda i, j: (i, j)),
        out_shape=jax.ShapeDtypeStruct((T, n_cols), BF16),
        scratch_shapes=[pltpu.VMEM((tm, D), BF16)],
        compiler_params=_params(("parallel", "arbitrary"),
                                _vmem_limit(blocks, [((tm, D), BF16)], [((tm, tn), F32)] * 2)),
        name="qkvg_proj",
    )(x2d, gain, w, head_gain)


def _fgate_kernel(x_ref, gain_ref, wf_ref, bf_ref, qa_ref, ka_ref, carry_ref, *, n_heads):
    @pl.when(pl.program_id(1) == 0)
    def _():
        carry_ref[...] = jnp.zeros_like(carry_ref)

    tm = x_ref.shape[0]
    h = _rms_rows(x_ref[...], gain_ref[...]).astype(BF16)
    f = jnp.dot(h, wf_ref[...], preferred_element_type=F32) + bf_ref[...]
    log_f = jnp.minimum(f, 0.0) - jnp.log1p(jnp.exp(-jnp.abs(f)))
    c = _cumsum_rows(log_f, _tril_mask(tm).astype(BF16)) + carry_ref[...]
    carry_ref[...] = c[tm - 1:tm, :]
    hi = c.astype(BF16).astype(F32)
    r1 = c - hi
    mid = r1.astype(BF16).astype(F32)
    lo = (r1 - mid).astype(BF16).astype(F32)
    lane = lax.broadcasted_iota(jnp.int32, (tm, LANES), 1)
    for hd in range(n_heads):
        chi, cmid, clo = (t[:, hd:hd + 1] for t in (hi, mid, lo))
        qa = jnp.where(lane == 0, chi, jnp.where(lane == 1, cmid, jnp.where(lane == 2, clo,
                       jnp.where(lane < 6, 1.0, 0.0))))
        ka = jnp.where(lane < 3, 1.0, jnp.where(lane == 3, -chi, jnp.where(lane == 4, -cmid,
                       jnp.where(lane == 5, -clo, 0.0))))
        qa_ref[:, hd * LANES:(hd + 1) * LANES] = qa.astype(qa_ref.dtype)
        ka_ref[:, hd * LANES:(hd + 1) * LANES] = ka.astype(ka_ref.dtype)


def _fgate(x2d, gain, w_f, b_f, n_heads, batch, seq):
    T, D = x2d.shape
    tm = _tile(seq, 512)
    nt = seq // tm
    W = n_heads * LANES
    blocks = [((tm, D), F32), ((1, D), F32), ((D, LANES), BF16), ((1, LANES), F32),
              ((tm, W), BF16), ((tm, W), BF16)]
    return pl.pallas_call(
        functools.partial(_fgate_kernel, n_heads=n_heads),
        grid=(batch, nt),
        in_specs=[
            pl.BlockSpec((tm, D), lambda b, i: (b * nt + i, 0)),
            pl.BlockSpec((1, D), lambda b, i: (0, 0)),
            pl.BlockSpec((D, LANES), lambda b, i: (0, 0)),
            pl.BlockSpec((1, LANES), lambda b, i: (0, 0)),
        ],
        out_specs=[pl.BlockSpec((tm, W), lambda b, i: (b * nt + i, 0)),
                   pl.BlockSpec((tm, W), lambda b, i: (b * nt + i, 0))],
        out_shape=[jax.ShapeDtypeStruct((T, W), BF16), jax.ShapeDtypeStruct((T, W), BF16)],
        scratch_shapes=[pltpu.VMEM((1, LANES), F32)],
        compiler_params=_params(("parallel", "arbitrary"),
                                _vmem_limit(blocks, [], [((tm, D), F32)] * 2 + [((tm, tm), BF16)])),
        name="fox_forget_cumsum",
    )(x2d, gain, w_f, b_f)


def _attn_kernel(q_ref, qa_ref, k_ref, ka_ref, v_ref, o_ref, acc_ref, *, tk):
    qi = pl.program_id(2)
    tq = q_ref.shape[0]
    q = jnp.concatenate([q_ref[...], qa_ref[...]], axis=1)
    acc_ref[...] = jnp.zeros_like(acc_ref)

    def step(j, m, l, masked):
        rows = pl.ds(pl.multiple_of(j * tk, tk), tk)
        k = jnp.concatenate([k_ref[rows, :], ka_ref[rows, :]], axis=1)
        s = _dot_nt(q, k)
        if masked:
            s = jnp.where(_tril_mask(tq), s, -jnp.inf)
        m_new = jnp.maximum(m, jnp.max(s, axis=1, keepdims=True))
        alpha = jnp.exp(m - m_new)
        p = jnp.exp(s - m_new)
        l_new = alpha * l + jnp.sum(p, axis=1, keepdims=True)
        acc_ref[...] = alpha * acc_ref[...] + jnp.dot(p.astype(BF16), v_ref[rows, :],
                                                      preferred_element_type=F32)
        return m_new, l_new

    m0 = jnp.full((tq, 1), -jnp.inf, F32)
    l0 = jnp.zeros((tq, 1), F32)
    m, l = lax.fori_loop(0, qi, lambda j, ml: step(j, ml[0], ml[1], False), (m0, l0))
    m, l = step(qi, m, l, True)
    o_ref[...] = (acc_ref[...] / l).astype(o_ref.dtype)


def _attention(qkvg, qa, ka, n_heads, batch, seq):
    T = qkvg.shape[0]
    tq = _tile(seq, 512)
    nq = seq // tq
    H = n_heads
    blocks = [((tq, LANES), BF16)] * 2 + [((seq, LANES), BF16)] * 3 + [((tq, LANES), BF16)]
    return pl.pallas_call(
        functools.partial(_attn_kernel, tk=tq),
        grid=(batch, H, nq),
        in_specs=[
            pl.BlockSpec((tq, LANES), lambda b, h, i: (b * nq + i, h)),
            pl.BlockSpec((tq, LANES), lambda b, h, i: (b * nq + i, h)),
            pl.BlockSpec((seq, LANES), lambda b, h, i: (b, h + H)),
            pl.BlockSpec((seq, LANES), lambda b, h, i: (b, h)),
            pl.BlockSpec((seq, LANES), lambda b, h, i: (b, h + 2 * H)),
        ],
        out_specs=pl.BlockSpec((tq, LANES), lambda b, h, i: (b * nq + i, h)),
        out_shape=jax.ShapeDtypeStruct((T, H * LANES), BF16),
        scratch_shapes=[pltpu.VMEM((tq, LANES), F32)],
        compiler_params=_params(("parallel", "parallel", "arbitrary"),
                                _vmem_limit(blocks, [((tq, LANES), F32)], [((tq, tq), F32)] * 4)),
        name="fox_attention",
    )(qkvg, qa, qkvg, ka, qkvg)


def _gated_out_kernel(o_ref, g_ref, w_ref, r_ref, out_ref, lhs_ref):
    @pl.when(pl.program_id(1) == 0)
    def _():
        lhs_ref[...] = (o_ref[...].astype(F32) * jax.nn.sigmoid(g_ref[...].astype(F32))).astype(lhs_ref.dtype)

    out_ref[...] = r_ref[...] + jnp.dot(lhs_ref[...], w_ref[...], preferred_element_type=F32)


def _gated_out_proj(o, qkvg, w, resid, seq):
    T, D = o.shape
    N = w.shape[1]
    tm = _tile(seq, 1024)
    tn = _tile(N, 512)
    g_blk = (qkvg.shape[1] - D) // D
    blocks = [((tm, D), BF16), ((tm, D), BF16), ((D, tn), BF16), ((tm, tn), F32), ((tm, tn), F32)]
    return pl.pallas_call(
        _gated_out_kernel,
        grid=(T // tm, N // tn),
        in_specs=[
            pl.BlockSpec((tm, D), lambda i, j: (i, 0)),
            pl.BlockSpec((tm, D), lambda i, j: (i, g_blk)),
            pl.BlockSpec((D, tn), lambda i, j: (0, j)),
            pl.BlockSpec((tm, tn), lambda i, j: (i, j)),
        ],
        out_specs=pl.BlockSpec((tm, tn), lambda i, j: (i, j)),
        out_shape=jax.ShapeDtypeStruct((T, N), F32),
        scratch_shapes=[pltpu.VMEM((tm, D), BF16)],
        compiler_params=_params(("parallel", "arbitrary"),
                                _vmem_limit(blocks, [((tm, D), BF16)], [((tm, D), F32)] * 2)),
        name="attn_out_proj",
    )(o, qkvg, w, resid)


def _mixer_ab(x2d, gain, w_in, sp_w, sp_b, v_gain, gamma, o_gain, w_out, layer, batch, seq):
    a_width = v_gain.shape[1]
    b_width = gamma.shape[1]
    w_in = w_in.astype(BF16)
    y_a = _gmlp_proj(x2d, gain, w_in, v_gain, sp_w, sp_b, seq)
    qfig = _norm_matmul(x2d, gain, w_in, 2 * a_width, 4 * b_width, seq, F32, "hgrn_in_proj")
    y_b = _hgrn(qfig, gamma, o_gain, layer, batch, seq)
    return _matmul_residual([y_a, y_b], w_out.astype(BF16), x2d, seq, "ab_out_proj")


def _mixer_c(x2d, gain, w_in, b_f, q_gain, k_gain, w_out, batch, seq):
    D = x2d.shape[1]
    n_heads = b_f.shape[1]
    head_dim = q_gain.shape[1]
    assert head_dim == LANES and n_heads * head_dim == D
    scale = head_dim ** -0.5
    head_gain = jnp.concatenate([jnp.tile(q_gain * scale, (1, n_heads)), jnp.tile(k_gain, (1, n_heads)),
                                 jnp.ones((1, 2 * D), F32)], axis=1)
    qkvg = _qkvg_proj(x2d, gain, w_in[:, :4 * D].astype(BF16), head_gain, 4 * D, 2 * D, seq)
    w_f = jnp.pad(w_in[:, 4 * D:], ((0, 0), (0, LANES - n_heads))).astype(BF16)
    b_fp = jnp.pad(b_f, ((0, 0), (0, LANES - n_heads)))
    qa, ka = _fgate(x2d, gain, w_f, b_fp, n_heads, batch, seq)
    o = _attention(qkvg, qa, ka, n_heads, batch, seq)
    return _gated_out_proj(o, qkvg, w_out.astype(BF16), x2d, seq)


def kernel(x, mix_norm, ab_w_in, ab_sp_w, ab_sp_b, ab_v_norm, hgrn_gamma, hgrn_o_norm, ab_w_out,
           c_w_in, c_b_f, c_q_norm, c_k_norm, c_w_out, ffn_norm, ffn_w_up, ffn_conv_w, ffn_conv_b,
           ffn_w_down):
    batch, seq, D = x.shape
    depth = mix_norm.shape[0]
    x2d = x.reshape(batch * seq, D)
    for l in range(depth):
        j = l // 2
        gain = mix_norm[l][None, :]
        if l % 2 == 0:
            x2d = _mixer_ab(x2d, gain, ab_w_in[j], ab_sp_w[j], ab_sp_b[j], ab_v_norm[j][None, :],
                            hgrn_gamma, hgrn_o_norm[j][None, :], ab_w_out[j], l, batch, seq)
        else:
            x2d = _mixer_c(x2d, gain, c_w_in[j], c_b_f[j][None, :], c_q_norm[j][None, :],
                           c_k_norm[j][None, :], c_w_out[j], batch, seq)
        x2d = _conv_ffn(x2d, ffn_norm[l][None, :], ffn_w_up[l].astype(BF16), ffn_conv_w[l],
                        ffn_conv_b[l][None, :], ffn_w_down[l].astype(BF16), seq)
    return x2d.reshape(batch, seq, D)
```

```python
import functools
import math

import jax
import jax.numpy as jnp
import numpy as np
from jax import lax
from jax.experimental import pallas as pl
from jax.experimental.pallas import tpu as pltpu

F32 = jnp.float32
BF16 = jnp.bfloat16
RMS_EPS = 1e-6
LANES = 128
SUBLANES = 8
MXU_COLS = 256
HGRN_CHUNK = 64
HGRN_MAX_FACTORED_RANGE = 60.0
LOG2E = 1.4426950408889634
V7X_VMEM_BYTES = 64 * 1024 * 1024
VMEM_CAP_BYTES = V7X_VMEM_BYTES - 6 * 1024 * 1024


def _nbytes(shape, dtype):
    return math.prod(shape) * jnp.dtype(dtype).itemsize


def _vmem_limit(blocks, scratch=(), temps=()):
    est = 2 * sum(_nbytes(s, d) for s, d in blocks)
    est += sum(_nbytes(s, d) for s, d in scratch) + sum(_nbytes(s, d) for s, d in temps)
    return int(min(VMEM_CAP_BYTES, est * 5 // 4 + (4 << 20)))


def _tile(n, pref, mult=LANES):
    if n <= pref:
        return n
    t = (pref // mult) * mult
    while n % t:
        t -= mult
    return t


def _params(sem, limit):
    return pltpu.CompilerParams(dimension_semantics=sem, vmem_limit_bytes=limit)


def _rms_rows(xf, gain):
    ms = jnp.mean(xf * xf, axis=-1, keepdims=True)
    return xf * lax.rsqrt(ms + RMS_EPS) * gain


def _gelu(x):
    return 0.5 * x * (1.0 + lax.erf(x * (2.0 ** -0.5)))


def _fill_normed(x_ref, gain_ref, h_ref, rows=256):
    rows = min(rows, x_ref.shape[0])

    def body(r, carry):
        sl = pl.ds(pl.multiple_of(r * rows, rows), rows)
        h_ref[sl, :] = _rms_rows(x_ref[sl, :], gain_ref[...]).astype(h_ref.dtype)
        return carry

    lax.fori_loop(0, x_ref.shape[0] // rows, body, 0)


def _tril_mask(n):
    t = lax.broadcasted_iota(jnp.int32, (n, n), 0)
    s = lax.broadcasted_iota(jnp.int32, (n, n), 1)
    return s <= t


def _cumsum_rows(x, tril_bf16):
    hi = x.astype(BF16)
    r1 = x - hi.astype(F32)
    mid = r1.astype(BF16)
    lo = (r1 - mid.astype(F32)).astype(BF16)
    y = jnp.dot(tril_bf16, jnp.concatenate([hi, mid, lo], axis=1), preferred_element_type=F32)
    d = x.shape[1]
    return y[:, :d] + y[:, d:2 * d] + y[:, 2 * d:]


def _dot_nt(a, b):
    return lax.dot_general(a, b, (((1,), (1,)), ((), ())), preferred_element_type=F32)


def _dot_tn(a, b):
    return lax.dot_general(a, b, (((0,), (0,)), ((), ())), preferred_element_type=F32)


def _gmlp_kernel(x_ref, gain_ref, wu_ref, wv_ref, vg_ref, spw_ref, spb_ref, o_ref, h_ref, *, chunk):
    @pl.when(pl.program_id(1) == 0)
    def _():
        _fill_normed(x_ref, gain_ref, h_ref)

    h = h_ref[...]
    u = _gelu(jnp.dot(h, wu_ref[...], preferred_element_type=F32))
    v = _gelu(jnp.dot(h, wv_ref[...], preferred_element_type=F32))
    tm, tn = u.shape
    n_chunks = tm // chunk
    tril = _tril_mask(chunk)
    for hh in range(tn // LANES):
        lanes = slice(hh * LANES, (hh + 1) * LANES)
        vh = _rms_rows(v[:, lanes], vg_ref[:, lanes]).astype(BF16)
        vcat = jnp.concatenate([vh[c * chunk:(c + 1) * chunk, :] for c in range(n_chunks)], axis=1)
        w_causal = jnp.where(tril, spw_ref[hh], 0.0).astype(BF16)
        mixed = jnp.dot(w_causal, vcat, preferred_element_type=F32)
        for c in range(n_chunks):
            rows = slice(c * chunk, (c + 1) * chunk)
            m_c = mixed[:, c * LANES:(c + 1) * LANES] + spb_ref[hh]
            o_ref[rows, lanes] = (u[rows, lanes] * m_c).astype(o_ref.dtype)


def _gmlp_proj(x2d, gain, w_in, v_gain, sp_w, sp_b, seq):
    T, D = x2d.shape
    n_heads, chunk, _ = sp_w.shape
    a_width = n_heads * LANES
    tm = _tile(seq, 1024, chunk)
    tn = _tile(a_width, 256)
    nj = a_width // tn
    spb = jnp.broadcast_to(sp_b[:, :, None], (n_heads, chunk, LANES))
    blocks = [((tm, D), F32), ((1, D), F32), ((D, tn), BF16), ((D, tn), BF16), ((1, tn), F32),
              ((tn // LANES, chunk, chunk), F32), ((tn // LANES, chunk, LANES), F32), ((tm, tn), BF16)]
    return pl.pallas_call(
        functools.partial(_gmlp_kernel, chunk=chunk),
        grid=(T // tm, nj),
        in_specs=[
            pl.BlockSpec((tm, D), lambda i, j: (i, 0)),
            pl.BlockSpec((1, D), lambda i, j: (0, 0)),
            pl.BlockSpec((D, tn), lambda i, j: (0, j)),
            pl.BlockSpec((D, tn), lambda i, j: (0, j + nj)),
            pl.BlockSpec((1, tn), lambda i, j: (0, j)),
            pl.BlockSpec((tn // LANES, chunk, chunk), lambda i, j: (j, 0, 0)),
            pl.BlockSpec((tn // LANES, chunk, LANES), lambda i, j: (j, 0, 0)),
        ],
        out_specs=pl.BlockSpec((tm, tn), lambda i, j: (i, j)),
        out_shape=jax.ShapeDtypeStruct((T, a_width), BF16),
        scratch_shapes=[pltpu.VMEM((tm, D), BF16)],
        compiler_params=_params(("parallel", "arbitrary"),
                                _vmem_limit(blocks, [((tm, D), BF16)], [((tm, tn), F32)] * 6)),
        name="gmlp_proj",
    )(x2d, gain, w_in, w_in, v_gain, sp_w, spb)


def _norm_mm_kernel(x_ref, gain_ref, w_ref, o_ref, h_ref):
    @pl.when(pl.program_id(1) == 0)
    def _():
        _fill_normed(x_ref, gain_ref, h_ref)

    o_ref[...] = jnp.dot(h_ref[...], w_ref[...], preferred_element_type=F32).astype(o_ref.dtype)


def _norm_matmul(x2d, gain, w, col0, n_cols, seq, out_dtype, name):
    T, D = x2d.shape
    tm = _tile(seq, 1024)
    tn = _tile(n_cols, 1024)
    assert col0 % tn == 0
    j0 = col0 // tn
    blocks = [((tm, D), F32), ((1, D), F32), ((D, tn), BF16), ((tm, tn), out_dtype)]
    return pl.pallas_call(
        _norm_mm_kernel,
        grid=(T // tm, n_cols // tn),
        in_specs=[
            pl.BlockSpec((tm, D), lambda i, j: (i, 0)),
            pl.BlockSpec((1, D), lambda i, j: (0, 0)),
            pl.BlockSpec((D, tn), lambda i, j: (0, j + j0)),
        ],
        out_specs=pl.BlockSpec((tm, tn), lambda i, j: (i, j)),
        out_shape=jax.ShapeDtypeStruct((T, n_cols), out_dtype),
        scratch_shapes=[pltpu.VMEM((tm, D), BF16)],
        compiler_params=_params(("parallel", "arbitrary"),
                                _vmem_limit(blocks, [((tm, D), BF16)], [((tm, tn), F32)])),
        name=name,
    )(x2d, gain, w)


def _hgrn_kernel(gamma_ref, q_ref, f_ref, i_ref, g_ref, og_ref, o_ref, st_ref, g_scr, k_scr, intra_scr, inter_scr,
                 *, layer, chunk):
    @pl.when(pl.program_id(2) == 0)
    def _():
        st_ref[...] = jnp.zeros_like(st_ref)

    gam = gamma_ref[...]
    ex = jnp.exp(gam - jnp.max(gam, axis=0, keepdims=True))
    lb_all = jnp.sum(ex[:layer + 1], axis=0, keepdims=True) / jnp.sum(ex, axis=0, keepdims=True)

    L, width = q_ref.shape
    n_chunks = L // chunk
    mid = chunk // 2 - 1
    row_chunk = lax.broadcasted_iota(jnp.int32, (L, L), 0) // chunk
    col_chunk = lax.broadcasted_iota(jnp.int32, (L, L), 1) // chunk
    mask = _tril_mask(L) & (row_chunk == col_chunk)
    chunk_of_row = lax.broadcasted_iota(jnp.int32, (L, LANES), 0) // chunk

    fl = f_ref[...]
    e = jnp.exp(-jnp.abs(fl))
    r = 1.0 / (1.0 + e)
    pos = fl >= 0
    sig = jnp.where(pos, r, e * r)
    nsig = jnp.where(pos, e * r, r)
    kk = (1.0 - lb_all) * nsig
    G = _cumsum_rows(jnp.log(lb_all + (1.0 - lb_all) * sig), mask.astype(BF16))
    g_scr[...] = G
    k_scr[...] = kk
    g_mid_rows = [G[c * chunk + mid:c * chunk + mid + 1, :] for c in range(n_chunks)]
    g_end_rows = [G[(c + 1) * chunk - 1:(c + 1) * chunk, :] for c in range(n_chunks)]
    per_row = lambda rows: jnp.concatenate([jnp.broadcast_to(x, (chunk, width)) for x in rows], axis=0)
    g_mid = per_row(g_mid_rows)
    g_end = per_row(g_end_rows)
    qv = q_ref[...]
    q_t = (qv * jnp.exp(G - g_mid)).astype(BF16)
    k_t = (kk * jnp.exp(g_mid - G)).astype(BF16)
    q_g = (qv * jnp.exp(G)).astype(BF16)
    k_end = (kk * jnp.exp(g_end - G)).astype(BF16)
    vv = i_ref[...].astype(BF16)
    decay_range = -jnp.min(jnp.concatenate(g_end_rows, axis=0))
    factorable = decay_range <= HGRN_MAX_FACTORED_RANGE

    def finish(o, lanes):
        gv = g_ref[:, lanes]
        o_ref[:, lanes] = (_rms_rows(o, og_ref[...]) * (gv * jax.nn.sigmoid(gv))).astype(o_ref.dtype)

    for hh in range(width // LANES):
        lanes = slice(hh * LANES, (hh + 1) * LANES)
        scores = jnp.where(mask, _dot_nt(q_t[:, lanes], k_t[:, lanes]), 0.0).astype(BF16)
        intra = jnp.dot(scores, vv[:, lanes], preferred_element_type=F32)
        v_exp = jnp.concatenate([jnp.where(chunk_of_row == c, vv[:, lanes], 0) for c in range(n_chunks)],
                                axis=1)
        u_t = _dot_tn(v_exp, k_end[:, lanes])
        st = st_ref[hh]
        prev = []
        for c in range(n_chunks):
            prev.append(st.astype(BF16))
            st = st * jnp.exp(g_end_rows[c][:, lanes]) + u_t[c * LANES:(c + 1) * LANES, :]
        st_ref[hh] = st
        q_exp = jnp.concatenate([jnp.where(chunk_of_row == c, q_g[:, lanes], 0) for c in range(n_chunks)],
                                axis=1)
        inter = _dot_nt(q_exp, jnp.concatenate(prev, axis=1))
        inter_scr[hh] = inter
        finish(intra + inter, lanes)

    @pl.when(jnp.logical_not(factorable))
    def _():
        for hh in range(width // LANES):
            _hgrn_intra_pairwise(q_ref, i_ref, g_scr, k_scr, intra_scr, hh, chunk)
            finish(intra_scr[hh] + inter_scr[hh], slice(hh * LANES, (hh + 1) * LANES))


def _hgrn_intra_pairwise(q_ref, i_ref, g_scr, k_scr, intra_scr, hh, chunk):
    lanes = slice(hh * LANES, (hh + 1) * LANES)
    t_idx = lax.broadcasted_iota(jnp.int32, (chunk, 1), 0)
    for c in range(q_ref.shape[0] // chunk):
        rows = slice(c * chunk, (c + 1) * chunk)
        g_c = g_scr[rows, lanes]
        q_c = q_ref[rows, lanes]

        def body(grp, acc):
            keys = pl.ds(pl.multiple_of(c * chunk + grp * SUBLANES, SUBLANES), SUBLANES)
            g_s, k_s, v_s = g_scr[keys, lanes], k_scr[keys, lanes], i_ref[keys, lanes]
            for j in range(SUBLANES):
                w = q_c * k_s[j:j + 1] * jnp.exp(jnp.minimum(g_c - g_s[j:j + 1], 0.0))
                score = jnp.where(t_idx >= grp * SUBLANES + j, jnp.sum(w, axis=1, keepdims=True), 0.0)
                acc = acc + score * v_s[j:j + 1]
            return acc

        intra_scr[hh, rows, :] = lax.fori_loop(0, chunk // SUBLANES, body, jnp.zeros((chunk, LANES), F32))


def _hgrn(qfig, gamma, o_gain, layer, batch, seq):
    T = qfig.shape[0]
    b_width = gamma.shape[1]
    n_layers = gamma.shape[0]
    width = _tile(b_width, 256)
    L = _tile(seq, 256, HGRN_CHUNK)
    nw = b_width // width
    nl = seq // L
    blocks = [((L, width), F32)] * 4 + [((L, width), BF16), ((n_layers, width), F32)]
    n_sub = width // LANES
    scratch = [((n_sub, LANES, LANES), F32), ((L, width), F32), ((L, width), F32), ((n_sub, L, LANES), F32),
               ((n_sub, L, LANES), F32)]
    row = lambda b, h, l: b * nl + l
    return pl.pallas_call(
        functools.partial(_hgrn_kernel, layer=layer, chunk=HGRN_CHUNK),
        grid=(batch, nw, nl),
        in_specs=[
            pl.BlockSpec((n_layers, width), lambda b, h, l: (0, h)),
            pl.BlockSpec((L, width), lambda b, h, l: (row(b, h, l), h)),
            pl.BlockSpec((L, width), lambda b, h, l: (row(b, h, l), h + nw)),
            pl.BlockSpec((L, width), lambda b, h, l: (row(b, h, l), h + 2 * nw)),
            pl.BlockSpec((L, width), lambda b, h, l: (row(b, h, l), h + 3 * nw)),
            pl.BlockSpec((1, LANES), lambda b, h, l: (0, 0)),
        ],
        out_specs=pl.BlockSpec((L, width), lambda b, h, l: (row(b, h, l), h)),
        out_shape=jax.ShapeDtypeStruct((T, b_width), BF16),
        scratch_shapes=[pltpu.VMEM(sh, dt) for sh, dt in scratch],
        compiler_params=_params(("parallel", "parallel", "arbitrary"),
                                _vmem_limit(blocks, scratch, [((L, width), F32)] * 12)),
        name="hgrn2",
    )(gamma, qfig, qfig, qfig, qfig, o_gain)


def _mm_resid_kernel(*refs, n_lhs):
    lhs_refs = refs[:n_lhs]
    w_ref, r_ref, o_ref = refs[n_lhs:]
    acc = r_ref[...]
    k0 = 0
    for a_ref in lhs_refs:
        k = a_ref.shape[1]
        acc = acc + jnp.dot(a_ref[...], w_ref[k0:k0 + k, :], preferred_element_type=F32)
        k0 += k
    o_ref[...] = acc


def _matmul_residual(lhs_list, w, resid, seq, name, tn_pref=512):
    T, N = resid.shape
    K = w.shape[0]
    tm = _tile(seq, 1024)
    tn = _tile(N, tn_pref)
    blocks = [((tm, a.shape[1]), BF16) for a in lhs_list] + [((K, tn), BF16), ((tm, tn), F32), ((tm, tn), F32)]
    return pl.pallas_call(
        functools.partial(_mm_resid_kernel, n_lhs=len(lhs_list)),
        grid=(T // tm, N // tn),
        in_specs=[pl.BlockSpec((tm, a.shape[1]), lambda i, j: (i, 0)) for a in lhs_list] + [
            pl.BlockSpec((K, tn), lambda i, j: (0, j)),
            pl.BlockSpec((tm, tn), lambda i, j: (i, j)),
        ],
        out_specs=pl.BlockSpec((tm, tn), lambda i, j: (i, j)),
        out_shape=jax.ShapeDtypeStruct((T, N), F32),
        compiler_params=_params(("parallel", "arbitrary"), _vmem_limit(blocks, [], [((tm, tn), F32)])),
        name=name,
    )(*lhs_list, w, resid)


def _ffn_up_kernel(x_ref, xh_ref, gain_ref, wa_ref, wb_ref, cwa_ref, cwb_ref, cba_ref, cbb_ref,
                   o_ref, h_ref, *, tiles_per_seq, halo):
    i = pl.program_id(0)
    tm = x_ref.shape[0]

    @pl.when(pl.program_id(1) == 0)
    def _():
        keep = (i % tiles_per_seq != 0).astype(F32)
        h_ref[0:halo, :] = (_rms_rows(xh_ref[...], gain_ref[...]) * keep).astype(h_ref.dtype)
        rows = min(256, tm)

        def body(r, carry):
            src = pl.ds(pl.multiple_of(r * rows, rows), rows)
            dst = pl.ds(pl.multiple_of(r * rows + halo, halo), rows)
            h_ref[dst, :] = _rms_rows(x_ref[src, :], gain_ref[...]).astype(h_ref.dtype)
            return carry

        lax.fori_loop(0, tm // rows, body, 0)

    h = h_ref[...]

    def conv(w_ref, cw_ref, cb_ref):
        z = jnp.dot(h, w_ref[...], preferred_element_type=F32)
        z1 = pltpu.roll(z, 1, 0)
        z2 = pltpu.roll(z, 2, 0)
        y = cw_ref[0:1, :] * z2 + cw_ref[1:2, :] * z1 + cw_ref[2:3, :] * z + cb_ref[...]
        return y[halo:, :]

    a = conv(wa_ref, cwa_ref, cba_ref)
    b = conv(wb_ref, cwb_ref, cbb_ref)
    o_ref[...] = (a * jax.nn.sigmoid(a) * b).astype(o_ref.dtype)


def _ffn_up(x2d, gain, w_up, conv_w, conv_b, seq):
    T, D = x2d.shape
    F = w_up.shape[1] // 2
    halo = 16
    tm = _tile(seq, 1024)
    tn = _tile(F, 512, MXU_COLS)
    nj = F // tn
    tiles_per_seq = seq // tm
    blocks = [((tm, D), F32), ((halo, D), F32), ((1, D), F32), ((D, tn), BF16), ((D, tn), BF16),
              ((3, tn), F32), ((3, tn), F32), ((1, tn), F32), ((1, tn), F32), ((tm, tn), BF16)]
    scratch = [((tm + halo, D), BF16)]
    hpt = tm // halo
    return pl.pallas_call(
        functools.partial(_ffn_up_kernel, tiles_per_seq=tiles_per_seq, halo=halo),
        grid=(T // tm, nj),
        in_specs=[
            pl.BlockSpec((tm, D), lambda i, j: (i, 0)),
            pl.BlockSpec((halo, D), lambda i, j: (jnp.maximum(i * hpt - 1, 0), 0)),
            pl.BlockSpec((1, D), lambda i, j: (0, 0)),
            pl.BlockSpec((D, tn), lambda i, j: (0, j)),
            pl.BlockSpec((D, tn), lambda i, j: (0, j + nj)),
            pl.BlockSpec((3, tn), lambda i, j: (0, j)),
            pl.BlockSpec((3, tn), lambda i, j: (0, j + nj)),
            pl.BlockSpec((1, tn), lambda i, j: (0, j)),
            pl.BlockSpec((1, tn), lambda i, j: (0, j + nj)),
        ],
        out_specs=pl.BlockSpec((tm, tn), lambda i, j: (i, j)),
        out_shape=jax.ShapeDtypeStruct((T, F), BF16),
        scratch_shapes=[pltpu.VMEM(sh, dt) for sh, dt in scratch],
        compiler_params=_params(("parallel", "arbitrary"),
                                _vmem_limit(blocks, scratch, [((tm + halo, MXU_COLS), F32)] * 8)),
        name="ffn_up_conv_gate",
    )(x2d, x2d, gain, w_up, w_up, conv_w, conv_w, conv_b, conv_b)


def _conv_ffn(x2d, gain, w_up, conv_w, conv_b, w_down, seq):
    act = _ffn_up(x2d, gain, w_up, conv_w, conv_b, seq)
    return _matmul_residual([act], w_down, x2d, seq, "ffn_down", tn_pref=512)


def _qkvg_kernel(x_ref, gain_ref, w_ref, hg_ref, o_ref, h_ref, *, n_norm_tiles):
    j = pl.program_id(1)
    n_sub = o_ref.shape[1] // MXU_COLS

    @pl.when(j == 0)
    def _():
        _fill_normed(x_ref, gain_ref, h_ref)

    @pl.when(j < n_norm_tiles)
    def _():
        h = h_ref[...]
        for c in range(n_sub):
            y = jnp.dot(h, w_ref[:, c * MXU_COLS:(c + 1) * MXU_COLS], preferred_element_type=F32)
            for grp in range(MXU_COLS // LANES):
                src = slice(grp * LANES, (grp + 1) * LANES)
                dst = slice(c * MXU_COLS + grp * LANES, c * MXU_COLS + (grp + 1) * LANES)
                o_ref[:, dst] = _rms_rows(y[:, src], hg_ref[:, dst]).astype(o_ref.dtype)

    @pl.when(j >= n_norm_tiles)
    def _():
        h = h_ref[...]
        for c in range(n_sub):
            cols = slice(c * MXU_COLS, (c + 1) * MXU_COLS)
            o_ref[:, cols] = jnp.dot(h, w_ref[:, cols], preferred_element_type=F32).astype(o_ref.dtype)


def _qkvg_proj(x2d, gain, w, head_gain, n_cols, n_norm_cols, seq):
    T, D = x2d.shape
    tm = _tile(seq, 1024)
    tn = _tile(n_norm_cols // 2, 1024, MXU_COLS)
    blocks = [((tm, D), F32), ((1, D), F32), ((D, tn), BF16), ((1, tn), F32), ((tm, tn), BF16)]
    return pl.pallas_call(
        functools.partial(_qkvg_kernel, n_norm_tiles=n_norm_cols // tn),
        grid=(T // tm, n_cols // tn),
        in_specs=[
            pl.BlockSpec((tm, D), lambda i, j: (i, 0)),
            pl.BlockSpec((1, D), lambda i, j: (0, 0)),
            pl.BlockSpec((D, tn), lambda i, j: (0, j)),
            pl.BlockSpec((1, tn), lambda i, j: (0, j)),
        ],
        out_specs=pl.BlockSpec((tm, tn), lambda i, j: (i, j)),
        out_shape=jax.ShapeDtypeStruct((T, n_cols), BF16),
        scratch_shapes=[pltpu.VMEM((tm, D), BF16)],
        compiler_params=_params(("parallel", "arbitrary"),
                                _vmem_limit(blocks, [((tm, D), BF16)], [((tm, MXU_COLS), F32)] * 4)),
        name="qkvg_proj",
    )(x2d, gain, w, head_gain)


def _fgate_kernel(x_ref, gain_ref, wf_ref, bf_ref, pq_ref, pk_ref, cq_ref, ck_ref, qa_ref, ka_ref,
                  carry_ref, *, n_heads):
    @pl.when(pl.program_id(1) == 0)
    def _():
        carry_ref[...] = jnp.zeros_like(carry_ref)

    tm = x_ref.shape[0]
    h = _rms_rows(x_ref[...], gain_ref[...]).astype(BF16)
    f = jnp.dot(h, wf_ref[...], preferred_element_type=F32) + bf_ref[...]
    log_f = (jnp.minimum(f, 0.0) - jnp.log1p(jnp.exp(-jnp.abs(f)))) * LOG2E
    c = _cumsum_rows(log_f, _tril_mask(tm).astype(BF16)) + carry_ref[...]
    carry_ref[...] = c[tm - 1:tm, :]
    hi = c.astype(BF16)
    r1 = c - hi.astype(F32)
    mid = r1.astype(BF16)
    lo = (r1 - mid.astype(F32)).astype(BF16)
    lane = lax.broadcasted_iota(jnp.int32, c.shape, 1)
    parts = jnp.where(lane < n_heads, hi, jnp.where(lane < 2 * n_heads, mid, lo))
    qa_ref[...] = (jnp.dot(parts, pq_ref[...], preferred_element_type=F32) + cq_ref[...]).astype(qa_ref.dtype)
    ka_ref[...] = (jnp.dot(parts, pk_ref[...], preferred_element_type=F32) + ck_ref[...]).astype(ka_ref.dtype)


def _fgate_selectors(n_heads):
    W = n_heads * LANES
    pq = np.zeros((LANES, W), np.float32)
    pk = np.zeros((LANES, W), np.float32)
    cq = np.zeros((1, W), np.float32)
    ck = np.zeros((1, W), np.float32)
    for hd in range(n_heads):
        for part in range(3):
            pq[part * n_heads + hd, hd * LANES + part] = 1.0
            pk[part * n_heads + hd, hd * LANES + 3 + part] = -1.0
            cq[0, hd * LANES + 3 + part] = 1.0
            ck[0, hd * LANES + part] = 1.0
    return jnp.asarray(pq, BF16), jnp.asarray(pk, BF16), jnp.asarray(cq), jnp.asarray(ck)


def _fgate(x2d, gain, w_f, b_f, n_heads, batch, seq):
    T, D = x2d.shape
    tm = _tile(seq, 512)
    nt = seq // tm
    W = n_heads * LANES
    pq, pk, cq, ck = _fgate_selectors(n_heads)
    blocks = [((tm, D), F32), ((1, D), F32), ((D, LANES), BF16), ((1, LANES), F32),
              ((LANES, W), BF16), ((LANES, W), BF16), ((1, W), F32), ((1, W), F32),
              ((tm, W), BF16), ((tm, W), BF16)]
    const = lambda b, i: (0, 0)
    return pl.pallas_call(
        functools.partial(_fgate_kernel, n_heads=n_heads),
        grid=(batch, nt),
        in_specs=[
            pl.BlockSpec((tm, D), lambda b, i: (b * nt + i, 0)),
            pl.BlockSpec((1, D), const),
            pl.BlockSpec((D, LANES), const),
            pl.BlockSpec((1, LANES), const),
            pl.BlockSpec((LANES, W), const),
            pl.BlockSpec((LANES, W), const),
            pl.BlockSpec((1, W), const),
            pl.BlockSpec((1, W), const),
        ],
        out_specs=[pl.BlockSpec((tm, W), lambda b, i: (b * nt + i, 0)),
                   pl.BlockSpec((tm, W), lambda b, i: (b * nt + i, 0))],
        out_shape=[jax.ShapeDtypeStruct((T, W), BF16), jax.ShapeDtypeStruct((T, W), BF16)],
        scratch_shapes=[pltpu.VMEM((1, LANES), F32)],
        compiler_params=_params(("parallel", "arbitrary"),
                                _vmem_limit(blocks, [], [((tm, D), F32)] * 3 + [((tm, tm), BF16)])),
        name="fox_forget_cumsum",
    )(x2d, gain, w_f, b_f, pq, pk, cq, ck)


def _attn_kernel(q_ref, qa_ref, k_ref, ka_ref, v_ref, g_ref, o_ref, qs_ref, acc_ref, *, tk):
    qi = pl.program_id(2)
    tq = q_ref.shape[0]
    n_sub = q_ref.shape[1] // LANES
    for hh in range(n_sub):
        lanes = slice(hh * LANES, (hh + 1) * LANES)
        qs_ref[hh] = jnp.concatenate([q_ref[:, lanes], qa_ref[:, lanes]], axis=1)
    acc_ref[...] = jnp.zeros_like(acc_ref)

    def step(j, carry, masked):
        rows = pl.ds(pl.multiple_of(j * tk, tk), tk)
        out = []
        for hh in range(n_sub):
            lanes = slice(hh * LANES, (hh + 1) * LANES)
            m, l = carry[2 * hh], carry[2 * hh + 1]
            k = jnp.concatenate([k_ref[rows, lanes], ka_ref[rows, lanes]], axis=1)
            s = _dot_nt(qs_ref[hh], k)
            if masked:
                s = jnp.where(_tril_mask(tq), s, -jnp.inf)
            m_new = jnp.maximum(m, jnp.max(s, axis=1, keepdims=True))
            alpha = jnp.exp2(m - m_new)
            p = jnp.exp2(s - m_new)
            l_new = alpha * l + jnp.sum(p, axis=1, keepdims=True)
            acc_ref[hh] = alpha * acc_ref[hh] + jnp.dot(p.astype(BF16), v_ref[rows, lanes],
                                                        preferred_element_type=F32)
            out += [m_new, l_new]
        return tuple(out)

    init = (jnp.full((tq, 1), -jnp.inf, F32), jnp.zeros((tq, 1), F32)) * n_sub
    carry = lax.fori_loop(0, qi, lambda j, c: step(j, c, False), init)
    carry = step(qi, carry, True)
    for hh in range(n_sub):
        lanes = slice(hh * LANES, (hh + 1) * LANES)
        gate = jax.nn.sigmoid(g_ref[:, lanes].astype(F32))
        o_ref[:, lanes] = (acc_ref[hh] / carry[2 * hh + 1] * gate).astype(o_ref.dtype)


def _attention(qkvg, qa, ka, n_heads, batch, seq):
    T = qkvg.shape[0]
    tq = _tile(seq, 512)
    nq = seq // tq
    width = 2 * LANES
    ng = n_heads * LANES // width
    blocks = [((tq, width), BF16)] * 2 + [((seq, width), BF16)] * 3 + [((tq, width), BF16)] * 2
    scratch = [((width // LANES, tq, 2 * LANES), BF16), ((width // LANES, tq, LANES), F32)]
    return pl.pallas_call(
        functools.partial(_attn_kernel, tk=tq),
        grid=(batch, ng, nq),
        in_specs=[
            pl.BlockSpec((tq, width), lambda b, h, i: (b * nq + i, h)),
            pl.BlockSpec((tq, width), lambda b, h, i: (b * nq + i, h)),
            pl.BlockSpec((seq, width), lambda b, h, i: (b, h + ng)),
            pl.BlockSpec((seq, width), lambda b, h, i: (b, h)),
            pl.BlockSpec((seq, width), lambda b, h, i: (b, h + 2 * ng)),
            pl.BlockSpec((tq, width), lambda b, h, i: (b * nq + i, h + 3 * ng)),
        ],
        out_specs=pl.BlockSpec((tq, width), lambda b, h, i: (b * nq + i, h)),
        out_shape=jax.ShapeDtypeStruct((T, n_heads * LANES), BF16),
        scratch_shapes=[pltpu.VMEM(sh, dt) for sh, dt in scratch],
        compiler_params=_params(("parallel", "parallel", "arbitrary"),
                                _vmem_limit(blocks, scratch, [((tq, tq), F32)] * 6)),
        name="fox_attention",
    )(qkvg, qa, qkvg, ka, qkvg, qkvg)


def _mixer_ab(x2d, gain, w_in, sp_w, sp_b, v_gain, gamma, o_gain, w_out, layer, batch, seq):
    a_width = v_gain.shape[1]
    b_width = gamma.shape[1]
    w_in = w_in.astype(BF16)
    y_a = _gmlp_proj(x2d, gain, w_in, v_gain, sp_w, sp_b, seq)
    qfig = _norm_matmul(x2d, gain, w_in, 2 * a_width, 4 * b_width, seq, F32, "hgrn_in_proj")
    y_b = _hgrn(qfig, gamma, o_gain, layer, batch, seq)
    return _matmul_residual([y_a, y_b], w_out.astype(BF16), x2d, seq, "ab_out_proj")


def _mixer_c(x2d, gain, w_in, b_f, q_gain, k_gain, w_out, batch, seq):
    D = x2d.shape[1]
    n_heads = b_f.shape[1]
    head_dim = q_gain.shape[1]
    assert head_dim == LANES and n_heads * head_dim == D
    scale = head_dim ** -0.5 * LOG2E
    head_gain = jnp.concatenate([jnp.tile(q_gain * scale, (1, n_heads)), jnp.tile(k_gain, (1, n_heads)),
                                 jnp.ones((1, 2 * D), F32)], axis=1)
    qkvg = _qkvg_proj(x2d, gain, w_in[:, :4 * D].astype(BF16), head_gain, 4 * D, 2 * D, seq)
    assert 3 * n_heads <= LANES
    pad = ((0, 0), (0, LANES - 3 * n_heads))
    w_f = jnp.pad(jnp.tile(w_in[:, 4 * D:], (1, 3)), pad).astype(BF16)
    b_fp = jnp.pad(jnp.tile(b_f, (1, 3)), pad)
    qa, ka = _fgate(x2d, gain, w_f, b_fp, n_heads, batch, seq)
    o = _attention(qkvg, qa, ka, n_heads, batch, seq)
    return _matmul_residual([o], w_out.astype(BF16), x2d, seq, "attn_out_proj")


def kernel(x, mix_norm, ab_w_in, ab_sp_w, ab_sp_b, ab_v_norm, hgrn_gamma, hgrn_o_norm, ab_w_out,
           c_w_in, c_b_f, c_q_norm, c_k_norm, c_w_out, ffn_norm, ffn_w_up, ffn_conv_w, ffn_conv_b,
           ffn_w_down):
    batch, seq, D = x.shape
    depth = mix_norm.shape[0]
    x2d = x.reshape(batch * seq, D)
    for l in range(depth):
        j = l // 2
        gain = mix_norm[l][None, :]
        if l % 2 == 0:
            x2d = _mixer_ab(x2d, gain, ab_w_in[j], ab_sp_w[j], ab_sp_b[j], ab_v_norm[j][None, :],
                            hgrn_gamma, hgrn_o_norm[j][None, :], ab_w_out[j], l, batch, seq)
        else:
            x2d = _mixer_c(x2d, gain, c_w_in[j], c_b_f[j][None, :], c_q_norm[j][None, :],
                           c_k_norm[j][None, :], c_w_out[j], batch, seq)
        x2d = _conv_ffn(x2d, ffn_norm[l][None, :], ffn_w_up[l].astype(BF16), ffn_conv_w[l],
                        ffn_conv_b[l][None, :], ffn_w_down[l].astype(BF16), seq)
    return x2d.reshape(batch, seq, D)
```

```python
import functools
import math

import jax
import jax.numpy as jnp
import numpy as np
from jax import lax
from jax.experimental import pallas as pl
from jax.experimental.pallas import tpu as pltpu

F32 = jnp.float32
BF16 = jnp.bfloat16
RMS_EPS = 1e-6
LANES = 128
SUBLANES = 8
MXU_COLS = 256
HGRN_CHUNK = 64
HGRN_MAX_FACTORED_RANGE = 60.0
ATTN_KEY_BLOCK = 512
ATTN_HEADS_PER_STEP = 4
LOG2E = 1.4426950408889634
V7X_VMEM_BYTES = 64 * 1024 * 1024
VMEM_CAP_BYTES = V7X_VMEM_BYTES - 6 * 1024 * 1024


def _nbytes(shape, dtype):
    return math.prod(shape) * jnp.dtype(dtype).itemsize


def _vmem_limit(blocks, scratch=(), temps=()):
    est = 2 * sum(_nbytes(s, d) for s, d in blocks)
    est += sum(_nbytes(s, d) for s, d in scratch) + sum(_nbytes(s, d) for s, d in temps)
    return int(min(VMEM_CAP_BYTES, est * 5 // 4 + (4 << 20)))


def _tile(n, pref, mult=LANES):
    if n <= pref:
        return n
    t = (pref // mult) * mult
    while n % t:
        t -= mult
    return t


def _params(sem, limit):
    return pltpu.CompilerParams(dimension_semantics=sem, vmem_limit_bytes=limit)


def _rms_rows(xf, gain):
    ms = jnp.mean(xf * xf, axis=-1, keepdims=True)
    return xf * lax.rsqrt(ms + RMS_EPS) * gain


def _gelu(x):
    return 0.5 * x * (1.0 + lax.erf(x * (2.0 ** -0.5)))


def _fill_normed(x_ref, gain_ref, h_ref, rows=256):
    rows = min(rows, x_ref.shape[0])

    def body(r, carry):
        sl = pl.ds(pl.multiple_of(r * rows, rows), rows)
        h_ref[sl, :] = _rms_rows(x_ref[sl, :], gain_ref[...]).astype(h_ref.dtype)
        return carry

    lax.fori_loop(0, x_ref.shape[0] // rows, body, 0)


def _tril_mask(n):
    t = lax.broadcasted_iota(jnp.int32, (n, n), 0)
    s = lax.broadcasted_iota(jnp.int32, (n, n), 1)
    return s <= t


def _cumsum_rows(x, tril_bf16):
    hi = x.astype(BF16)
    r1 = x - hi.astype(F32)
    mid = r1.astype(BF16)
    lo = (r1 - mid.astype(F32)).astype(BF16)
    y = jnp.dot(tril_bf16, jnp.concatenate([hi, mid, lo], axis=1), preferred_element_type=F32)
    d = x.shape[1]
    return y[:, :d] + y[:, d:2 * d] + y[:, 2 * d:]


def _dot_nt(a, b):
    return lax.dot_general(a, b, (((1,), (1,)), ((), ())), preferred_element_type=F32)


def _dot_tn(a, b):
    return lax.dot_general(a, b, (((0,), (0,)), ((), ())), preferred_element_type=F32)


def _gmlp_kernel(x_ref, gain_ref, wu_ref, wv_ref, vg_ref, spw_ref, spb_ref, o_ref, h_ref, *, chunk):
    @pl.when(pl.program_id(1) == 0)
    def _():
        _fill_normed(x_ref, gain_ref, h_ref)

    h = h_ref[...]
    u = _gelu(jnp.dot(h, wu_ref[...].astype(BF16), preferred_element_type=F32))
    v = _gelu(jnp.dot(h, wv_ref[...].astype(BF16), preferred_element_type=F32))
    tm, tn = u.shape
    n_chunks = tm // chunk
    tril = _tril_mask(chunk)
    for hh in range(tn // LANES):
        lanes = slice(hh * LANES, (hh + 1) * LANES)
        vh = _rms_rows(v[:, lanes], vg_ref[:, lanes]).astype(BF16)
        vcat = jnp.concatenate([vh[c * chunk:(c + 1) * chunk, :] for c in range(n_chunks)], axis=1)
        w_causal = jnp.where(tril, spw_ref[hh], 0.0).astype(BF16)
        mixed = jnp.dot(w_causal, vcat, preferred_element_type=F32)
        for c in range(n_chunks):
            rows = slice(c * chunk, (c + 1) * chunk)
            m_c = mixed[:, c * LANES:(c + 1) * LANES] + spb_ref[hh]
            o_ref[rows, lanes] = (u[rows, lanes] * m_c).astype(o_ref.dtype)


def _gmlp_proj(x2d, gain, w_in, layer, v_gain, sp_w, sp_b, seq):
    T, D = x2d.shape
    n_heads, chunk, _ = sp_w.shape
    a_width = n_heads * LANES
    tm = _tile(seq, 1024, chunk)
    tn = _tile(a_width, 256)
    nj = a_width // tn
    spb = jnp.broadcast_to(sp_b[:, :, None], (n_heads, chunk, LANES))
    blocks = [((tm, D), F32), ((1, D), F32), ((D, tn), F32), ((D, tn), F32), ((1, tn), F32),
              ((tn // LANES, chunk, chunk), F32), ((tn // LANES, chunk, LANES), F32), ((tm, tn), BF16)]
    return pl.pallas_call(
        functools.partial(_gmlp_kernel, chunk=chunk),
        grid=(T // tm, nj),
        in_specs=[
            pl.BlockSpec((tm, D), lambda i, j: (i, 0)),
            pl.BlockSpec((1, D), lambda i, j: (0, 0)),
            pl.BlockSpec((None, D, tn), lambda i, j: (layer, 0, j)),
            pl.BlockSpec((None, D, tn), lambda i, j: (layer, 0, j + nj)),
            pl.BlockSpec((1, tn), lambda i, j: (0, j)),
            pl.BlockSpec((tn // LANES, chunk, chunk), lambda i, j: (j, 0, 0)),
            pl.BlockSpec((tn // LANES, chunk, LANES), lambda i, j: (j, 0, 0)),
        ],
        out_specs=pl.BlockSpec((tm, tn), lambda i, j: (i, j)),
        out_shape=jax.ShapeDtypeStruct((T, a_width), BF16),
        scratch_shapes=[pltpu.VMEM((tm, D), BF16)],
        compiler_params=_params(("parallel", "arbitrary"),
                                _vmem_limit(blocks, [((tm, D), BF16)], [((tm, tn), F32)] * 6)),
        name="gmlp_proj",
    )(x2d, gain, w_in, w_in, v_gain, sp_w, spb)


def _norm_mm_kernel(x_ref, gain_ref, w_ref, o_ref, h_ref):
    @pl.when(pl.program_id(1) == 0)
    def _():
        _fill_normed(x_ref, gain_ref, h_ref)

    o_ref[...] = jnp.dot(h_ref[...], w_ref[...].astype(BF16), preferred_element_type=F32).astype(o_ref.dtype)


def _norm_matmul(x2d, gain, w, layer, col0, n_cols, seq, out_dtype, name):
    T, D = x2d.shape
    tm = _tile(seq, 1024)
    tn = _tile(n_cols, 1024)
    assert col0 % tn == 0
    j0 = col0 // tn
    blocks = [((tm, D), F32), ((1, D), F32), ((D, tn), F32), ((tm, tn), out_dtype)]
    return pl.pallas_call(
        _norm_mm_kernel,
        grid=(T // tm, n_cols // tn),
        in_specs=[
            pl.BlockSpec((tm, D), lambda i, j: (i, 0)),
            pl.BlockSpec((1, D), lambda i, j: (0, 0)),
            pl.BlockSpec((None, D, tn), lambda i, j: (layer, 0, j + j0)),
        ],
        out_specs=pl.BlockSpec((tm, tn), lambda i, j: (i, j)),
        out_shape=jax.ShapeDtypeStruct((T, n_cols), out_dtype),
        scratch_shapes=[pltpu.VMEM((tm, D), BF16)],
        compiler_params=_params(("parallel", "arbitrary"),
                                _vmem_limit(blocks, [((tm, D), BF16)], [((tm, tn), F32)])),
        name=name,
    )(x2d, gain, w)


def _hgrn_kernel(gamma_ref, q_ref, f_ref, i_ref, g_ref, og_ref, o_ref, st_ref, g_scr, k_scr, intra_scr, inter_scr,
                 *, layer, chunk):
    @pl.when(pl.program_id(2) == 0)
    def _():
        st_ref[...] = jnp.zeros_like(st_ref)

    gam = gamma_ref[...]
    ex = jnp.exp(gam - jnp.max(gam, axis=0, keepdims=True))
    lb_all = jnp.sum(ex[:layer + 1], axis=0, keepdims=True) / jnp.sum(ex, axis=0, keepdims=True)

    L, width = q_ref.shape
    n_chunks = L // chunk
    mid = chunk // 2 - 1
    row_chunk = lax.broadcasted_iota(jnp.int32, (L, L), 0) // chunk
    col_chunk = lax.broadcasted_iota(jnp.int32, (L, L), 1) // chunk
    mask = _tril_mask(L) & (row_chunk == col_chunk)
    chunk_of_row = lax.broadcasted_iota(jnp.int32, (L, LANES), 0) // chunk

    fl = f_ref[...]
    e = jnp.exp(-jnp.abs(fl))
    r = 1.0 / (1.0 + e)
    pos = fl >= 0
    sig = jnp.where(pos, r, e * r)
    nsig = jnp.where(pos, e * r, r)
    kk = (1.0 - lb_all) * nsig
    G = _cumsum_rows(jnp.log(lb_all + (1.0 - lb_all) * sig), mask.astype(BF16))
    g_scr[...] = G
    k_scr[...] = kk
    g_mid_rows = [G[c * chunk + mid:c * chunk + mid + 1, :] for c in range(n_chunks)]
    g_end_rows = [G[(c + 1) * chunk - 1:(c + 1) * chunk, :] for c in range(n_chunks)]
    per_row = lambda rows: jnp.concatenate([jnp.broadcast_to(x, (chunk, width)) for x in rows], axis=0)
    g_mid = per_row(g_mid_rows)
    g_end = per_row(g_end_rows)
    qv = q_ref[...]
    q_t = (qv * jnp.exp(G - g_mid)).astype(BF16)
    k_t = (kk * jnp.exp(g_mid - G)).astype(BF16)
    q_g = (qv * jnp.exp(G)).astype(BF16)
    k_end = (kk * jnp.exp(g_end - G)).astype(BF16)
    vv = i_ref[...].astype(BF16)
    decay_range = -jnp.min(jnp.concatenate(g_end_rows, axis=0))
    factorable = decay_range <= HGRN_MAX_FACTORED_RANGE

    def finish(o, lanes):
        gv = g_ref[:, lanes]
        o_ref[:, lanes] = (_rms_rows(o, og_ref[...]) * (gv * jax.nn.sigmoid(gv))).astype(o_ref.dtype)

    for hh in range(width // LANES):
        lanes = slice(hh * LANES, (hh + 1) * LANES)
        scores = jnp.where(mask, _dot_nt(q_t[:, lanes], k_t[:, lanes]), 0.0).astype(BF16)
        intra = jnp.dot(scores, vv[:, lanes], preferred_element_type=F32)
        v_exp = jnp.concatenate([jnp.where(chunk_of_row == c, vv[:, lanes], 0) for c in range(n_chunks)],
                                axis=1)
        u_t = _dot_tn(v_exp, k_end[:, lanes])
        st = st_ref[hh]
        prev = []
        for c in range(n_chunks):
            prev.append(st.astype(BF16))
            st = st * jnp.exp(g_end_rows[c][:, lanes]) + u_t[c * LANES:(c + 1) * LANES, :]
        st_ref[hh] = st
        q_exp = jnp.concatenate([jnp.where(chunk_of_row == c, q_g[:, lanes], 0) for c in range(n_chunks)],
                                axis=1)
        inter = _dot_nt(q_exp, jnp.concatenate(prev, axis=1))
        inter_scr[hh] = inter
        finish(intra + inter, lanes)

    @pl.when(jnp.logical_not(factorable))
    def _():
        for hh in range(width // LANES):
            _hgrn_intra_pairwise(q_ref, i_ref, g_scr, k_scr, intra_scr, hh, chunk)
            finish(intra_scr[hh] + inter_scr[hh], slice(hh * LANES, (hh + 1) * LANES))


def _hgrn_intra_pairwise(q_ref, i_ref, g_scr, k_scr, intra_scr, hh, chunk):
    lanes = slice(hh * LANES, (hh + 1) * LANES)
    t_idx = lax.broadcasted_iota(jnp.int32, (chunk, 1), 0)
    for c in range(q_ref.shape[0] // chunk):
        rows = slice(c * chunk, (c + 1) * chunk)
        g_c = g_scr[rows, lanes]
        q_c = q_ref[rows, lanes]

        def body(grp, acc):
            keys = pl.ds(pl.multiple_of(c * chunk + grp * SUBLANES, SUBLANES), SUBLANES)
            g_s, k_s, v_s = g_scr[keys, lanes], k_scr[keys, lanes], i_ref[keys, lanes]
            for j in range(SUBLANES):
                w = q_c * k_s[j:j + 1] * jnp.exp(jnp.minimum(g_c - g_s[j:j + 1], 0.0))
                score = jnp.where(t_idx >= grp * SUBLANES + j, jnp.sum(w, axis=1, keepdims=True), 0.0)
                acc = acc + score * v_s[j:j + 1]
            return acc

        intra_scr[hh, rows, :] = lax.fori_loop(0, chunk // SUBLANES, body, jnp.zeros((chunk, LANES), F32))


def _hgrn(qfig, gamma, o_gain, layer, batch, seq):
    T = qfig.shape[0]
    b_width = gamma.shape[1]
    n_layers = gamma.shape[0]
    width = _tile(b_width, 512)
    L = _tile(seq, 256, HGRN_CHUNK)
    nw = b_width // width
    nl = seq // L
    blocks = [((L, width), F32)] * 4 + [((L, width), BF16), ((n_layers, width), F32)]
    n_sub = width // LANES
    scratch = [((n_sub, LANES, LANES), F32), ((L, width), F32), ((L, width), F32), ((n_sub, L, LANES), F32),
               ((n_sub, L, LANES), F32)]
    row = lambda b, h, l: b * nl + l
    return pl.pallas_call(
        functools.partial(_hgrn_kernel, layer=layer, chunk=HGRN_CHUNK),
        grid=(batch, nw, nl),
        in_specs=[
            pl.BlockSpec((n_layers, width), lambda b, h, l: (0, h)),
            pl.BlockSpec((L, width), lambda b, h, l: (row(b, h, l), h)),
            pl.BlockSpec((L, width), lambda b, h, l: (row(b, h, l), h + nw)),
            pl.BlockSpec((L, width), lambda b, h, l: (row(b, h, l), h + 2 * nw)),
            pl.BlockSpec((L, width), lambda b, h, l: (row(b, h, l), h + 3 * nw)),
            pl.BlockSpec((1, LANES), lambda b, h, l: (0, 0)),
        ],
        out_specs=pl.BlockSpec((L, width), lambda b, h, l: (row(b, h, l), h)),
        out_shape=jax.ShapeDtypeStruct((T, b_width), BF16),
        scratch_shapes=[pltpu.VMEM(sh, dt) for sh, dt in scratch],
        compiler_params=_params(("parallel", "parallel", "arbitrary"),
                                _vmem_limit(blocks, scratch, [((L, width), F32)] * 12)),
        name="hgrn2",
    )(gamma, qfig, qfig, qfig, qfig, o_gain)


def _mm_resid_kernel(*refs, n_lhs):
    lhs_refs = refs[:n_lhs]
    w_ref, r_ref, o_ref = refs[n_lhs:]
    acc = r_ref[...]
    k0 = 0
    for a_ref in lhs_refs:
        k = a_ref.shape[1]
        acc = acc + jnp.dot(a_ref[...], w_ref[k0:k0 + k, :].astype(BF16), preferred_element_type=F32)
        k0 += k
    o_ref[...] = acc


def _matmul_residual(lhs_list, w, layer, resid, seq, name, tn_pref=512):
    T, N = resid.shape
    K = w.shape[1]
    tm = _tile(seq, 1024)
    tn = _tile(N, tn_pref)
    blocks = [((tm, a.shape[1]), BF16) for a in lhs_list] + [((K, tn), F32), ((tm, tn), F32), ((tm, tn), F32)]
    return pl.pallas_call(
        functools.partial(_mm_resid_kernel, n_lhs=len(lhs_list)),
        grid=(T // tm, N // tn),
        in_specs=[pl.BlockSpec((tm, a.shape[1]), lambda i, j: (i, 0)) for a in lhs_list] + [
            pl.BlockSpec((None, K, tn), lambda i, j: (layer, 0, j)),
            pl.BlockSpec((tm, tn), lambda i, j: (i, j)),
        ],
        out_specs=pl.BlockSpec((tm, tn), lambda i, j: (i, j)),
        out_shape=jax.ShapeDtypeStruct((T, N), F32),
        compiler_params=_params(("parallel", "arbitrary"), _vmem_limit(blocks, [], [((tm, tn), F32)])),
        name=name,
    )(*lhs_list, w, resid)


def _ffn_up_kernel(x_ref, xh_ref, gain_ref, wa_ref, wb_ref, cwa_ref, cwb_ref, cba_ref, cbb_ref,
                   o_ref, h_ref, *, tiles_per_seq, halo):
    i = pl.program_id(0)
    tm = x_ref.shape[0]

    @pl.when(pl.program_id(1) == 0)
    def _():
        keep = (i % tiles_per_seq != 0).astype(F32)
        h_ref[0:halo, :] = (_rms_rows(xh_ref[...], gain_ref[...]) * keep).astype(h_ref.dtype)
        rows = min(256, tm)

        def body(r, carry):
            src = pl.ds(pl.multiple_of(r * rows, rows), rows)
            dst = pl.ds(pl.multiple_of(r * rows + halo, halo), rows)
            h_ref[dst, :] = _rms_rows(x_ref[src, :], gain_ref[...]).astype(h_ref.dtype)
            return carry

        lax.fori_loop(0, tm // rows, body, 0)

    h = h_ref[...]

    def conv(w_ref, cw_ref, cb_ref):
        z = jnp.dot(h, w_ref[...].astype(BF16), preferred_element_type=F32)
        z1 = pltpu.roll(z, 1, 0)
        z2 = pltpu.roll(z, 2, 0)
        y = cw_ref[0:1, :] * z2 + cw_ref[1:2, :] * z1 + cw_ref[2:3, :] * z + cb_ref[...]
        return y[halo:, :]

    a = conv(wa_ref, cwa_ref, cba_ref)
    b = conv(wb_ref, cwb_ref, cbb_ref)
    o_ref[...] = (a * jax.nn.sigmoid(a) * b).astype(o_ref.dtype)


def _ffn_up(x2d, gain, w_up, layer, conv_w, conv_b, seq):
    T, D = x2d.shape
    F = w_up.shape[2] // 2
    halo = 16
    tm = _tile(seq, 1024)
    tn = _tile(F, 512, MXU_COLS)
    nj = F // tn
    tiles_per_seq = seq // tm
    blocks = [((tm, D), F32), ((halo, D), F32), ((1, D), F32), ((D, tn), F32), ((D, tn), F32),
              ((3, tn), F32), ((3, tn), F32), ((1, tn), F32), ((1, tn), F32), ((tm, tn), BF16)]
    scratch = [((tm + halo, D), BF16)]
    hpt = tm // halo
    return pl.pallas_call(
        functools.partial(_ffn_up_kernel, tiles_per_seq=tiles_per_seq, halo=halo),
        grid=(T // tm, nj),
        in_specs=[
            pl.BlockSpec((tm, D), lambda i, j: (i, 0)),
            pl.BlockSpec((halo, D), lambda i, j: (jnp.maximum(i * hpt - 1, 0), 0)),
            pl.BlockSpec((1, D), lambda i, j: (0, 0)),
            pl.BlockSpec((None, D, tn), lambda i, j: (layer, 0, j)),
            pl.BlockSpec((None, D, tn), lambda i, j: (layer, 0, j + nj)),
            pl.BlockSpec((3, tn), lambda i, j: (0, j)),
            pl.BlockSpec((3, tn), lambda i, j: (0, j + nj)),
            pl.BlockSpec((1, tn), lambda i, j: (0, j)),
            pl.BlockSpec((1, tn), lambda i, j: (0, j + nj)),
        ],
        out_specs=pl.BlockSpec((tm, tn), lambda i, j: (i, j)),
        out_shape=jax.ShapeDtypeStruct((T, F), BF16),
        scratch_shapes=[pltpu.VMEM(sh, dt) for sh, dt in scratch],
        compiler_params=_params(("parallel", "arbitrary"),
                                _vmem_limit(blocks, scratch, [((tm + halo, MXU_COLS), F32)] * 8)),
        name="ffn_up_conv_gate",
    )(x2d, x2d, gain, w_up, w_up, conv_w, conv_w, conv_b, conv_b)


def _conv_ffn(x2d, gain, w_up, layer, conv_w, conv_b, w_down, seq):
    act = _ffn_up(x2d, gain, w_up, layer, conv_w, conv_b, seq)
    return _matmul_residual([act], w_down, layer, x2d, seq, "ffn_down", tn_pref=256)


def _qkvg_kernel(x_ref, gain_ref, w_ref, hg_ref, o_ref, h_ref, *, n_norm_tiles):
    j = pl.program_id(1)
    n_sub = o_ref.shape[1] // MXU_COLS

    @pl.when(j == 0)
    def _():
        _fill_normed(x_ref, gain_ref, h_ref)

    @pl.when(j < n_norm_tiles)
    def _():
        h = h_ref[...]
        for c in range(n_sub):
            y = jnp.dot(h, w_ref[:, c * MXU_COLS:(c + 1) * MXU_COLS].astype(BF16), preferred_element_type=F32)
            for grp in range(MXU_COLS // LANES):
                src = slice(grp * LANES, (grp + 1) * LANES)
                dst = slice(c * MXU_COLS + grp * LANES, c * MXU_COLS + (grp + 1) * LANES)
                o_ref[:, dst] = _rms_rows(y[:, src], hg_ref[:, dst]).astype(o_ref.dtype)

    @pl.when(j >= n_norm_tiles)
    def _():
        h = h_ref[...]
        for c in range(n_sub):
            cols = slice(c * MXU_COLS, (c + 1) * MXU_COLS)
            o_ref[:, cols] = jnp.dot(h, w_ref[:, cols].astype(BF16),
                                     preferred_element_type=F32).astype(o_ref.dtype)


def _qkvg_proj(x2d, gain, w, layer, head_gain, n_cols, n_norm_cols, seq):
    T, D = x2d.shape
    tm = _tile(seq, 1024)
    tn = _tile(n_norm_cols // 2, 1024, MXU_COLS)
    blocks = [((tm, D), F32), ((1, D), F32), ((D, tn), F32), ((1, tn), F32), ((tm, tn), BF16)]
    return pl.pallas_call(
        functools.partial(_qkvg_kernel, n_norm_tiles=n_norm_cols // tn),
        grid=(T // tm, n_cols // tn),
        in_specs=[
            pl.BlockSpec((tm, D), lambda i, j: (i, 0)),
            pl.BlockSpec((1, D), lambda i, j: (0, 0)),
            pl.BlockSpec((None, D, tn), lambda i, j: (layer, 0, j)),
            pl.BlockSpec((1, tn), lambda i, j: (0, j)),
        ],
        out_specs=pl.BlockSpec((tm, tn), lambda i, j: (i, j)),
        out_shape=jax.ShapeDtypeStruct((T, n_cols), BF16),
        scratch_shapes=[pltpu.VMEM((tm, D), BF16)],
        compiler_params=_params(("parallel", "arbitrary"),
                                _vmem_limit(blocks, [((tm, D), BF16)], [((tm, MXU_COLS), F32)] * 4)),
        name="qkvg_proj",
    )(x2d, gain, w, head_gain)


def _fgate_kernel(x_ref, gain_ref, wf_ref, bf_ref, pq_ref, pk_ref, cq_ref, ck_ref, qa_ref, ka_ref,
                  carry_ref, *, n_heads):
    @pl.when(pl.program_id(1) == 0)
    def _():
        carry_ref[...] = jnp.zeros_like(carry_ref)

    tm = x_ref.shape[0]
    h = _rms_rows(x_ref[...], gain_ref[...]).astype(BF16)
    f = jnp.dot(h, wf_ref[...], preferred_element_type=F32) + bf_ref[...]
    log_f = (jnp.minimum(f, 0.0) - jnp.log1p(jnp.exp(-jnp.abs(f)))) * LOG2E
    c = _cumsum_rows(log_f, _tril_mask(tm).astype(BF16)) + carry_ref[...]
    carry_ref[...] = c[tm - 1:tm, :]
    hi = c.astype(BF16)
    r1 = c - hi.astype(F32)
    mid = r1.astype(BF16)
    lo = (r1 - mid.astype(F32)).astype(BF16)
    lane = lax.broadcasted_iota(jnp.int32, c.shape, 1)
    parts = jnp.where(lane < n_heads, hi, jnp.where(lane < 2 * n_heads, mid, lo))
    qa_ref[...] = (jnp.dot(parts, pq_ref[...], preferred_element_type=F32) + cq_ref[...]).astype(qa_ref.dtype)
    ka_ref[...] = (jnp.dot(parts, pk_ref[...], preferred_element_type=F32) + ck_ref[...]).astype(ka_ref.dtype)


def _fgate_selectors(n_heads):
    W = n_heads * LANES
    pq = np.zeros((LANES, W), np.float32)
    pk = np.zeros((LANES, W), np.float32)
    cq = np.zeros((1, W), np.float32)
    ck = np.zeros((1, W), np.float32)
    for hd in range(n_heads):
        for part in range(3):
            pq[part * n_heads + hd, hd * LANES + part] = 1.0
            pk[part * n_heads + hd, hd * LANES + 3 + part] = -1.0
            cq[0, hd * LANES + 3 + part] = 1.0
            ck[0, hd * LANES + part] = 1.0
    return jnp.asarray(pq, BF16), jnp.asarray(pk, BF16), jnp.asarray(cq), jnp.asarray(ck)


def _fgate(x2d, gain, w_f, b_f, n_heads, batch, seq):
    T, D = x2d.shape
    tm = _tile(seq, 512)
    nt = seq // tm
    W = n_heads * LANES
    pq, pk, cq, ck = _fgate_selectors(n_heads)
    blocks = [((tm, D), F32), ((1, D), F32), ((D, LANES), BF16), ((1, LANES), F32),
              ((LANES, W), BF16), ((LANES, W), BF16), ((1, W), F32), ((1, W), F32),
              ((tm, W), BF16), ((tm, W), BF16)]
    const = lambda b, i: (0, 0)
    return pl.pallas_call(
        functools.partial(_fgate_kernel, n_heads=n_heads),
        grid=(batch, nt),
        in_specs=[
            pl.BlockSpec((tm, D), lambda b, i: (b * nt + i, 0)),
            pl.BlockSpec((1, D), const),
            pl.BlockSpec((D, LANES), const),
            pl.BlockSpec((1, LANES), const),
            pl.BlockSpec((LANES, W), const),
            pl.BlockSpec((LANES, W), const),
            pl.BlockSpec((1, W), const),
            pl.BlockSpec((1, W), const),
        ],
        out_specs=[pl.BlockSpec((tm, W), lambda b, i: (b * nt + i, 0)),
                   pl.BlockSpec((tm, W), lambda b, i: (b * nt + i, 0))],
        out_shape=[jax.ShapeDtypeStruct((T, W), BF16), jax.ShapeDtypeStruct((T, W), BF16)],
        scratch_shapes=[pltpu.VMEM((1, LANES), F32)],
        compiler_params=_params(("parallel", "arbitrary"),
                                _vmem_limit(blocks, [], [((tm, D), F32)] * 3 + [((tm, tm), BF16)])),
        name="fox_forget_cumsum",
    )(x2d, gain, w_f, b_f, pq, pk, cq, ck)


def _attn_kernel(q_ref, qa_ref, k_ref, ka_ref, v_ref, g_ref, o_ref, qs_ref, acc_ref, *, tk):
    qi = pl.program_id(2)
    tq = q_ref.shape[0]
    n_sub = q_ref.shape[1] // LANES
    for hh in range(n_sub):
        lanes = slice(hh * LANES, (hh + 1) * LANES)
        qs_ref[hh] = jnp.concatenate([q_ref[:, lanes], qa_ref[:, lanes]], axis=1)
    acc_ref[...] = jnp.zeros_like(acc_ref)

    def step(first_key, n_keys, carry, diag_offset=None):
        rows = pl.ds(pl.multiple_of(first_key, tk), n_keys)
        out = []
        for hh in range(n_sub):
            lanes = slice(hh * LANES, (hh + 1) * LANES)
            m, l = carry[2 * hh], carry[2 * hh + 1]
            k = jnp.concatenate([k_ref[rows, lanes], ka_ref[rows, lanes]], axis=1)
            s = _dot_nt(qs_ref[hh], k)
            if diag_offset is not None:
                t_pos = lax.broadcasted_iota(jnp.int32, (tq, n_keys), 0)
                s_pos = lax.broadcasted_iota(jnp.int32, (tq, n_keys), 1) + diag_offset
                s = jnp.where(s_pos <= t_pos, s, -jnp.inf)
            m_new = jnp.maximum(m, jnp.max(s, axis=1, keepdims=True))
            alpha = jnp.exp2(m - m_new)
            p = jnp.exp2(s - m_new)
            l_new = alpha * l + jnp.sum(p, axis=1, keepdims=True)
            acc_ref[hh] = alpha * acc_ref[hh] + jnp.dot(p.astype(BF16), v_ref[rows, lanes],
                                                        preferred_element_type=F32)
            out += [m_new, l_new]
        return tuple(out)

    init = (jnp.full((tq, 1), -jnp.inf, F32), jnp.zeros((tq, 1), F32)) * n_sub
    n_diag = tq // tk
    n_full = qi * n_diag
    carry = lax.fori_loop(0, n_full // 2, lambda j, c: step(j * 2 * tk, 2 * tk, c), init)
    carry = lax.fori_loop(0, n_full % 2, lambda j, c: step((n_full - 1) * tk, tk, c), carry)
    for d in range(n_diag):
        carry = step((n_full + d) * tk, tk, carry, d * tk)
    for hh in range(n_sub):
        lanes = slice(hh * LANES, (hh + 1) * LANES)
        gate = jax.nn.sigmoid(g_ref[:, lanes].astype(F32))
        o_ref[:, lanes] = (acc_ref[hh] / carry[2 * hh + 1] * gate).astype(o_ref.dtype)


def _attention(qkvg, qa, ka, n_heads, batch, seq):
    T = qkvg.shape[0]
    tq = _tile(seq, 512)
    nq = seq // tq
    width = ATTN_HEADS_PER_STEP * LANES
    ng = n_heads * LANES // width
    blocks = [((tq, width), BF16)] * 2 + [((seq, width), BF16)] * 3 + [((tq, width), BF16)] * 2
    scratch = [((width // LANES, tq, 2 * LANES), BF16), ((width // LANES, tq, LANES), F32)]
    return pl.pallas_call(
        functools.partial(_attn_kernel, tk=_tile(tq, ATTN_KEY_BLOCK)),
        grid=(batch, ng, nq),
        in_specs=[
            pl.BlockSpec((tq, width), lambda b, h, i: (b * nq + i, h)),
            pl.BlockSpec((tq, width), lambda b, h, i: (b * nq + i, h)),
            pl.BlockSpec((seq, width), lambda b, h, i: (b, h + ng)),
            pl.BlockSpec((seq, width), lambda b, h, i: (b, h)),
            pl.BlockSpec((seq, width), lambda b, h, i: (b, h + 2 * ng)),
            pl.BlockSpec((tq, width), lambda b, h, i: (b * nq + i, h + 3 * ng)),
        ],
        out_specs=pl.BlockSpec((tq, width), lambda b, h, i: (b * nq + i, h)),
        out_shape=jax.ShapeDtypeStruct((T, n_heads * LANES), BF16),
        scratch_shapes=[pltpu.VMEM(sh, dt) for sh, dt in scratch],
        compiler_params=_params(("parallel", "parallel", "arbitrary"),
                                _vmem_limit(blocks, scratch, [((tq, 2 * tq), F32)] * 2 * ATTN_HEADS_PER_STEP)),
        name="fox_attention",
    )(qkvg, qa, qkvg, ka, qkvg, qkvg)


def _mixer_ab(x2d, gain, w_in, j, sp_w, sp_b, v_gain, gamma, o_gain, w_out, layer, batch, seq):
    a_width = v_gain.shape[1]
    b_width = gamma.shape[1]
    y_a = _gmlp_proj(x2d, gain, w_in, j, v_gain, sp_w, sp_b, seq)
    qfig = _norm_matmul(x2d, gain, w_in, j, 2 * a_width, 4 * b_width, seq, F32, "hgrn_in_proj")
    y_b = _hgrn(qfig, gamma, o_gain, layer, batch, seq)
    return _matmul_residual([y_a, y_b], w_out, j, x2d, seq, "ab_out_proj")


def _mixer_c(x2d, gain, w_in, j, b_f, q_gain, k_gain, w_out, batch, seq):
    D = x2d.shape[1]
    n_heads = b_f.shape[1]
    head_dim = q_gain.shape[1]
    assert head_dim == LANES and n_heads * head_dim == D
    scale = head_dim ** -0.5 * LOG2E
    head_gain = jnp.concatenate([jnp.tile(q_gain * scale, (1, n_heads)), jnp.tile(k_gain, (1, n_heads)),
                                 jnp.ones((1, 2 * D), F32)], axis=1)
    qkvg = _qkvg_proj(x2d, gain, w_in, j, head_gain, 4 * D, 2 * D, seq)
    assert 3 * n_heads <= LANES
    pad = ((0, 0), (0, LANES - 3 * n_heads))
    w_f = jnp.pad(jnp.tile(w_in[j, :, 4 * D:], (1, 3)), pad).astype(BF16)
    b_fp = jnp.pad(jnp.tile(b_f, (1, 3)), pad)
    qa, ka = _fgate(x2d, gain, w_f, b_fp, n_heads, batch, seq)
    o = _attention(qkvg, qa, ka, n_heads, batch, seq)
    return _matmul_residual([o], w_out, j, x2d, seq, "attn_out_proj")


def kernel(x, mix_norm, ab_w_in, ab_sp_w, ab_sp_b, ab_v_norm, hgrn_gamma, hgrn_o_norm, ab_w_out,
           c_w_in, c_b_f, c_q_norm, c_k_norm, c_w_out, ffn_norm, ffn_w_up, ffn_conv_w, ffn_conv_b,
           ffn_w_down):
    batch, seq, D = x.shape
    depth = mix_norm.shape[0]
    x2d = x.reshape(batch * seq, D)
    for l in range(depth):
        j = l // 2
        gain = mix_norm[l][None, :]
        if l % 2 == 0:
            x2d = _mixer_ab(x2d, gain, ab_w_in, j, ab_sp_w[j], ab_sp_b[j], ab_v_norm[j][None, :],
                            hgrn_gamma, hgrn_o_norm[j][None, :], ab_w_out, l, batch, seq)
        else:
            x2d = _mixer_c(x2d, gain, c_w_in, j, c_b_f[j][None, :], c_q_norm[j][None, :],
                           c_k_norm[j][None, :], c_w_out, batch, seq)
        x2d = _conv_ffn(x2d, ffn_norm[l][None, :], ffn_w_up, l, ffn_conv_w[l],
                        ffn_conv_b[l][None, :], ffn_w_down, seq)
    return x2d.reshape(batch, seq, D)
```

```python
import functools
import math

import jax
import jax.numpy as jnp
import numpy as np
from jax import lax
from jax.experimental import pallas as pl
from jax.experimental.pallas import tpu as pltpu

F32 = jnp.float32
BF16 = jnp.bfloat16
RMS_EPS = 1e-6
LANES = 128
SUBLANES = 8
MXU_COLS = 256
HGRN_CHUNK = 64
HGRN_MAX_FACTORED_RANGE = 60.0
ATTN_KEY_BLOCK = 512
ATTN_HEADS_PER_STEP = 4
LOG2E = 1.4426950408889634
V7X_VMEM_BYTES = 64 * 1024 * 1024
VMEM_CAP_BYTES = V7X_VMEM_BYTES - 6 * 1024 * 1024


def _nbytes(shape, dtype):
    return math.prod(shape) * jnp.dtype(dtype).itemsize


def _vmem_limit(blocks, scratch=(), temps=()):
    est = 2 * sum(_nbytes(s, d) for s, d in blocks)
    est += sum(_nbytes(s, d) for s, d in scratch) + sum(_nbytes(s, d) for s, d in temps)
    return int(min(VMEM_CAP_BYTES, est * 5 // 4 + (4 << 20)))


def _tile(n, pref, mult=LANES):
    if n <= pref:
        return n
    t = (pref // mult) * mult
    while n % t:
        t -= mult
    return t


def _params(sem, limit):
    return pltpu.CompilerParams(dimension_semantics=sem, vmem_limit_bytes=limit)


def _rms_rows(xf, gain):
    ms = jnp.mean(xf * xf, axis=-1, keepdims=True)
    return xf * lax.rsqrt(ms + RMS_EPS) * gain


def _gelu(x):
    return 0.5 * x * (1.0 + lax.erf(x * (2.0 ** -0.5)))


def _fill_normed(x_ref, gain_ref, h_ref, rows=256):
    rows = min(rows, x_ref.shape[0])

    def body(r, carry):
        sl = pl.ds(pl.multiple_of(r * rows, rows), rows)
        h_ref[sl, :] = _rms_rows(x_ref[sl, :], gain_ref[...]).astype(h_ref.dtype)
        return carry

    lax.fori_loop(0, x_ref.shape[0] // rows, body, 0)


def _tril_mask(n):
    t = lax.broadcasted_iota(jnp.int32, (n, n), 0)
    s = lax.broadcasted_iota(jnp.int32, (n, n), 1)
    return s <= t


def _cumsum_rows(x, tril_bf16):
    hi = x.astype(BF16)
    r1 = x - hi.astype(F32)
    mid = r1.astype(BF16)
    lo = (r1 - mid.astype(F32)).astype(BF16)
    y = jnp.dot(tril_bf16, jnp.concatenate([hi, mid, lo], axis=1), preferred_element_type=F32)
    d = x.shape[1]
    return y[:, :d] + y[:, d:2 * d] + y[:, 2 * d:]


def _dot_nt(a, b):
    return lax.dot_general(a, b, (((1,), (1,)), ((), ())), preferred_element_type=F32)


def _dot_tn(a, b):
    return lax.dot_general(a, b, (((0,), (0,)), ((), ())), preferred_element_type=F32)


def _gmlp_kernel(x_ref, gain_ref, wu_ref, wv_ref, vg_ref, spw_ref, spb_ref, o_ref, h_ref, *, chunk):
    @pl.when(pl.program_id(1) == 0)
    def _():
        _fill_normed(x_ref, gain_ref, h_ref)

    h = h_ref[...]
    u = _gelu(jnp.dot(h, wu_ref[...].astype(BF16), preferred_element_type=F32))
    v = _gelu(jnp.dot(h, wv_ref[...].astype(BF16), preferred_element_type=F32))
    tm, tn = u.shape
    n_chunks = tm // chunk
    tril = _tril_mask(chunk)
    for hh in range(tn // LANES):
        lanes = slice(hh * LANES, (hh + 1) * LANES)
        vh = _rms_rows(v[:, lanes], vg_ref[:, lanes]).astype(BF16)
        vcat = jnp.concatenate([vh[c * chunk:(c + 1) * chunk, :] for c in range(n_chunks)], axis=1)
        w_causal = jnp.where(tril, spw_ref[hh], 0.0).astype(BF16)
        mixed = jnp.dot(w_causal, vcat, preferred_element_type=F32)
        for c in range(n_chunks):
            rows = slice(c * chunk, (c + 1) * chunk)
            m_c = mixed[:, c * LANES:(c + 1) * LANES] + spb_ref[hh]
            o_ref[rows, lanes] = (u[rows, lanes] * m_c).astype(o_ref.dtype)


def _gmlp_proj(x2d, gain, w_in, layer, v_gain, sp_w, sp_b, seq):
    T, D = x2d.shape
    n_heads, chunk, _ = sp_w.shape
    a_width = n_heads * LANES
    tm = _tile(seq, 1024, chunk)
    tn = _tile(a_width, 256)
    nj = a_width // tn
    spb = jnp.broadcast_to(sp_b[:, :, None], (n_heads, chunk, LANES))
    blocks = [((tm, D), F32), ((1, D), F32), ((D, tn), w_in.dtype), ((D, tn), w_in.dtype), ((1, tn), F32),
              ((tn // LANES, chunk, chunk), F32), ((tn // LANES, chunk, LANES), F32), ((tm, tn), BF16)]
    return pl.pallas_call(
        functools.partial(_gmlp_kernel, chunk=chunk),
        grid=(T // tm, nj),
        in_specs=[
            pl.BlockSpec((tm, D), lambda i, j: (i, 0)),
            pl.BlockSpec((1, D), lambda i, j: (0, 0)),
            pl.BlockSpec((None, D, tn), lambda i, j: (layer, 0, j)),
            pl.BlockSpec((None, D, tn), lambda i, j: (layer, 0, j + nj)),
            pl.BlockSpec((1, tn), lambda i, j: (0, j)),
            pl.BlockSpec((tn // LANES, chunk, chunk), lambda i, j: (j, 0, 0)),
            pl.BlockSpec((tn // LANES, chunk, LANES), lambda i, j: (j, 0, 0)),
        ],
        out_specs=pl.BlockSpec((tm, tn), lambda i, j: (i, j)),
        out_shape=jax.ShapeDtypeStruct((T, a_width), BF16),
        scratch_shapes=[pltpu.VMEM((tm, D), BF16)],
        compiler_params=_params(("parallel", "arbitrary"),
                                _vmem_limit(blocks, [((tm, D), BF16)], [((tm, tn), F32)] * 6)),
        name="gmlp_proj",
    )(x2d, gain, w_in, w_in, v_gain, sp_w, spb)


def _norm_mm_kernel(x_ref, gain_ref, w_ref, o_ref, h_ref):
    @pl.when(pl.program_id(1) == 0)
    def _():
        _fill_normed(x_ref, gain_ref, h_ref)

    o_ref[...] = jnp.dot(h_ref[...], w_ref[...].astype(BF16), preferred_element_type=F32).astype(o_ref.dtype)


def _norm_matmul(x2d, gain, w, layer, col0, n_cols, seq, out_dtype, name):
    T, D = x2d.shape
    tm = _tile(seq, 1024)
    tn = _tile(n_cols, 1024)
    assert col0 % tn == 0
    j0 = col0 // tn
    blocks = [((tm, D), F32), ((1, D), F32), ((D, tn), w.dtype), ((tm, tn), out_dtype)]
    return pl.pallas_call(
        _norm_mm_kernel,
        grid=(T // tm, n_cols // tn),
        in_specs=[
            pl.BlockSpec((tm, D), lambda i, j: (i, 0)),
            pl.BlockSpec((1, D), lambda i, j: (0, 0)),
            pl.BlockSpec((None, D, tn), lambda i, j: (layer, 0, j + j0)),
        ],
        out_specs=pl.BlockSpec((tm, tn), lambda i, j: (i, j)),
        out_shape=jax.ShapeDtypeStruct((T, n_cols), out_dtype),
        scratch_shapes=[pltpu.VMEM((tm, D), BF16)],
        compiler_params=_params(("parallel", "arbitrary"),
                                _vmem_limit(blocks, [((tm, D), BF16)], [((tm, tn), F32)])),
        name=name,
    )(x2d, gain, w)


def _hgrn_kernel(gamma_ref, q_ref, f_ref, i_ref, g_ref, og_ref, o_ref, st_ref, g_scr, k_scr, intra_scr, inter_scr,
                 *, layer, chunk):
    @pl.when(pl.program_id(2) == 0)
    def _():
        st_ref[...] = jnp.zeros_like(st_ref)

    gam = gamma_ref[...]
    ex = jnp.exp(gam - jnp.max(gam, axis=0, keepdims=True))
    lb_all = jnp.sum(ex[:layer + 1], axis=0, keepdims=True) / jnp.sum(ex, axis=0, keepdims=True)

    L, width = q_ref.shape
    n_chunks = L // chunk
    mid = chunk // 2 - 1
    row_chunk = lax.broadcasted_iota(jnp.int32, (L, L), 0) // chunk
    col_chunk = lax.broadcasted_iota(jnp.int32, (L, L), 1) // chunk
    mask = _tril_mask(L) & (row_chunk == col_chunk)
    chunk_of_row = lax.broadcasted_iota(jnp.int32, (L, LANES), 0) // chunk

    fl = f_ref[...]
    e = jnp.exp(-jnp.abs(fl))
    r = 1.0 / (1.0 + e)
    pos = fl >= 0
    sig = jnp.where(pos, r, e * r)
    nsig = jnp.where(pos, e * r, r)
    kk = (1.0 - lb_all) * nsig
    G = _cumsum_rows(jnp.log(lb_all + (1.0 - lb_all) * sig), mask.astype(BF16))
    g_scr[...] = G
    k_scr[...] = kk
    g_mid_rows = [G[c * chunk + mid:c * chunk + mid + 1, :] for c in range(n_chunks)]
    g_end_rows = [G[(c + 1) * chunk - 1:(c + 1) * chunk, :] for c in range(n_chunks)]
    per_row = lambda rows: jnp.concatenate([jnp.broadcast_to(x, (chunk, width)) for x in rows], axis=0)
    g_mid = per_row(g_mid_rows)
    g_end = per_row(g_end_rows)
    qv = q_ref[...]
    q_t = (qv * jnp.exp(G - g_mid)).astype(BF16)
    k_t = (kk * jnp.exp(g_mid - G)).astype(BF16)
    q_g = (qv * jnp.exp(G)).astype(BF16)
    k_end = (kk * jnp.exp(g_end - G)).astype(BF16)
    vv = i_ref[...].astype(BF16)
    decay_range = -jnp.min(jnp.concatenate(g_end_rows, axis=0))
    factorable = decay_range <= HGRN_MAX_FACTORED_RANGE

    def finish(o, lanes):
        gv = g_ref[:, lanes]
        o_ref[:, lanes] = (_rms_rows(o, og_ref[...]) * (gv * jax.nn.sigmoid(gv))).astype(o_ref.dtype)

    for hh in range(width // LANES):
        lanes = slice(hh * LANES, (hh + 1) * LANES)
        scores = jnp.where(mask, _dot_nt(q_t[:, lanes], k_t[:, lanes]), 0.0).astype(BF16)
        intra = jnp.dot(scores, vv[:, lanes], preferred_element_type=F32)
        v_exp = jnp.concatenate([jnp.where(chunk_of_row == c, vv[:, lanes], 0) for c in range(n_chunks)],
                                axis=1)
        u_t = _dot_tn(v_exp, k_end[:, lanes])
        st = st_ref[hh]
        prev = []
        for c in range(n_chunks):
            prev.append(st.astype(BF16))
            st = st * jnp.exp(g_end_rows[c][:, lanes]) + u_t[c * LANES:(c + 1) * LANES, :]
        st_ref[hh] = st
        q_exp = jnp.concatenate([jnp.where(chunk_of_row == c, q_g[:, lanes], 0) for c in range(n_chunks)],
                                axis=1)
        inter = _dot_nt(q_exp, jnp.concatenate(prev, axis=1))
        inter_scr[hh] = inter
        finish(intra + inter, lanes)

    @pl.when(jnp.logical_not(factorable))
    def _():
        for hh in range(width // LANES):
            _hgrn_intra_pairwise(q_ref, i_ref, g_scr, k_scr, intra_scr, hh, chunk)
            finish(intra_scr[hh] + inter_scr[hh], slice(hh * LANES, (hh + 1) * LANES))


def _hgrn_intra_pairwise(q_ref, i_ref, g_scr, k_scr, intra_scr, hh, chunk):
    lanes = slice(hh * LANES, (hh + 1) * LANES)
    t_idx = lax.broadcasted_iota(jnp.int32, (chunk, 1), 0)
    for c in range(q_ref.shape[0] // chunk):
        rows = slice(c * chunk, (c + 1) * chunk)
        g_c = g_scr[rows, lanes]
        q_c = q_ref[rows, lanes]

        def body(grp, acc):
            keys = pl.ds(pl.multiple_of(c * chunk + grp * SUBLANES, SUBLANES), SUBLANES)
            g_s, k_s, v_s = g_scr[keys, lanes], k_scr[keys, lanes], i_ref[keys, lanes]
            for j in range(SUBLANES):
                w = q_c * k_s[j:j + 1] * jnp.exp(jnp.minimum(g_c - g_s[j:j + 1], 0.0))
                score = jnp.where(t_idx >= grp * SUBLANES + j, jnp.sum(w, axis=1, keepdims=True), 0.0)
                acc = acc + score * v_s[j:j + 1]
            return acc

        intra_scr[hh, rows, :] = lax.fori_loop(0, chunk // SUBLANES, body, jnp.zeros((chunk, LANES), F32))


def _hgrn(qfig, gamma, o_gain, layer, batch, seq):
    T = qfig.shape[0]
    b_width = gamma.shape[1]
    n_layers = gamma.shape[0]
    width = _tile(b_width, 512)
    L = _tile(seq, 256, HGRN_CHUNK)
    nw = b_width // width
    nl = seq // L
    blocks = [((L, width), F32)] * 4 + [((L, width), BF16), ((n_layers, width), F32)]
    n_sub = width // LANES
    scratch = [((n_sub, LANES, LANES), F32), ((L, width), F32), ((L, width), F32), ((n_sub, L, LANES), F32),
               ((n_sub, L, LANES), F32)]
    row = lambda b, h, l: b * nl + l
    return pl.pallas_call(
        functools.partial(_hgrn_kernel, layer=layer, chunk=HGRN_CHUNK),
        grid=(batch, nw, nl),
        in_specs=[
            pl.BlockSpec((n_layers, width), lambda b, h, l: (0, h)),
            pl.BlockSpec((L, width), lambda b, h, l: (row(b, h, l), h)),
            pl.BlockSpec((L, width), lambda b, h, l: (row(b, h, l), h + nw)),
            pl.BlockSpec((L, width), lambda b, h, l: (row(b, h, l), h + 2 * nw)),
            pl.BlockSpec((L, width), lambda b, h, l: (row(b, h, l), h + 3 * nw)),
            pl.BlockSpec((1, LANES), lambda b, h, l: (0, 0)),
        ],
        out_specs=pl.BlockSpec((L, width), lambda b, h, l: (row(b, h, l), h)),
        out_shape=jax.ShapeDtypeStruct((T, b_width), BF16),
        scratch_shapes=[pltpu.VMEM(sh, dt) for sh, dt in scratch],
        compiler_params=_params(("parallel", "parallel", "arbitrary"),
                                _vmem_limit(blocks, scratch, [((L, width), F32)] * 12)),
        name="hgrn2",
    )(gamma, qfig, qfig, qfig, qfig, o_gain)


def _mm_resid_kernel(*refs, n_lhs):
    lhs_refs = refs[:n_lhs]
    w_ref, r_ref, o_ref = refs[n_lhs:]
    acc = r_ref[...]
    k0 = 0
    for a_ref in lhs_refs:
        k = a_ref.shape[1]
        acc = acc + jnp.dot(a_ref[...], w_ref[k0:k0 + k, :].astype(BF16), preferred_element_type=F32)
        k0 += k
    o_ref[...] = acc


def _matmul_residual(lhs_list, w, layer, resid, seq, name, tn_pref=512):
    T, N = resid.shape
    K = w.shape[1]
    tm = _tile(seq, 1024)
    tn = _tile(N, tn_pref)
    blocks = [((tm, a.shape[1]), BF16) for a in lhs_list] + [((K, tn), w.dtype), ((tm, tn), F32), ((tm, tn), F32)]
    return pl.pallas_call(
        functools.partial(_mm_resid_kernel, n_lhs=len(lhs_list)),
        grid=(T // tm, N // tn),
        in_specs=[pl.BlockSpec((tm, a.shape[1]), lambda i, j: (i, 0)) for a in lhs_list] + [
            pl.BlockSpec((None, K, tn), lambda i, j: (layer, 0, j)),
            pl.BlockSpec((tm, tn), lambda i, j: (i, j)),
        ],
        out_specs=pl.BlockSpec((tm, tn), lambda i, j: (i, j)),
        out_shape=jax.ShapeDtypeStruct((T, N), F32),
        compiler_params=_params(("parallel", "arbitrary"), _vmem_limit(blocks, [], [((tm, tn), F32)])),
        name=name,
    )(*lhs_list, w, resid)


def _ffn_up_kernel(x_ref, xh_ref, gain_ref, wa_ref, wb_ref, cwa_ref, cwb_ref, cba_ref, cbb_ref,
                   o_ref, h_ref, *, tiles_per_seq, halo):
    i = pl.program_id(0)
    tm = x_ref.shape[0]

    @pl.when(pl.program_id(1) == 0)
    def _():
        keep = (i % tiles_per_seq != 0).astype(F32)
        h_ref[0:halo, :] = (_rms_rows(xh_ref[...], gain_ref[...]) * keep).astype(h_ref.dtype)
        rows = min(256, tm)

        def body(r, carry):
            src = pl.ds(pl.multiple_of(r * rows, rows), rows)
            dst = pl.ds(pl.multiple_of(r * rows + halo, halo), rows)
            h_ref[dst, :] = _rms_rows(x_ref[src, :], gain_ref[...]).astype(h_ref.dtype)
            return carry

        lax.fori_loop(0, tm // rows, body, 0)

    h = h_ref[...]

    def conv(w_ref, cw_ref, cb_ref):
        z = jnp.dot(h, w_ref[...].astype(BF16), preferred_element_type=F32)
        z1 = pltpu.roll(z, 1, 0)
        z2 = pltpu.roll(z, 2, 0)
        y = cw_ref[0:1, :] * z2 + cw_ref[1:2, :] * z1 + cw_ref[2:3, :] * z + cb_ref[...]
        return y[halo:, :]

    a = conv(wa_ref, cwa_ref, cba_ref)
    b = conv(wb_ref, cwb_ref, cbb_ref)
    o_ref[...] = (a * jax.nn.sigmoid(a) * b).astype(o_ref.dtype)


def _ffn_up(x2d, gain, w_up, layer, conv_w, conv_b, seq):
    T, D = x2d.shape
    F = w_up.shape[2] // 2
    halo = 16
    tm = _tile(seq, 1024)
    tn = _tile(F, 512, MXU_COLS)
    nj = F // tn
    tiles_per_seq = seq // tm
    blocks = [((tm, D), F32), ((halo, D), F32), ((1, D), F32), ((D, tn), F32), ((D, tn), F32),
              ((3, tn), F32), ((3, tn), F32), ((1, tn), F32), ((1, tn), F32), ((tm, tn), BF16)]
    scratch = [((tm + halo, D), BF16)]
    hpt = tm // halo
    return pl.pallas_call(
        functools.partial(_ffn_up_kernel, tiles_per_seq=tiles_per_seq, halo=halo),
        grid=(T // tm, nj),
        in_specs=[
            pl.BlockSpec((tm, D), lambda i, j: (i, 0)),
            pl.BlockSpec((halo, D), lambda i, j: (jnp.maximum(i * hpt - 1, 0), 0)),
            pl.BlockSpec((1, D), lambda i, j: (0, 0)),
            pl.BlockSpec((None, D, tn), lambda i, j: (layer, 0, j)),
            pl.BlockSpec((None, D, tn), lambda i, j: (layer, 0, j + nj)),
            pl.BlockSpec((3, tn), lambda i, j: (0, j)),
            pl.BlockSpec((3, tn), lambda i, j: (0, j + nj)),
            pl.BlockSpec((1, tn), lambda i, j: (0, j)),
            pl.BlockSpec((1, tn), lambda i, j: (0, j + nj)),
        ],
        out_specs=pl.BlockSpec((tm, tn), lambda i, j: (i, j)),
        out_shape=jax.ShapeDtypeStruct((T, F), BF16),
        scratch_shapes=[pltpu.VMEM(sh, dt) for sh, dt in scratch],
        compiler_params=_params(("parallel", "arbitrary"),
                                _vmem_limit(blocks, scratch, [((tm + halo, MXU_COLS), F32)] * 8)),
        name="ffn_up_conv_gate",
    )(x2d, x2d, gain, w_up, w_up, conv_w, conv_w, conv_b, conv_b)


def _conv_ffn(x2d, gain, w_up, layer, conv_w, conv_b, w_down, seq):
    act = _ffn_up(x2d, gain, w_up, layer, conv_w, conv_b, seq)
    return _matmul_residual([act], w_down, layer, x2d, seq, "ffn_down", tn_pref=512)


def _qkvg_kernel(x_ref, gain_ref, w_ref, hg_ref, o_ref, h_ref, *, n_norm_tiles):
    j = pl.program_id(1)
    n_sub = o_ref.shape[1] // MXU_COLS

    @pl.when(j == 0)
    def _():
        _fill_normed(x_ref, gain_ref, h_ref)

    @pl.when(j < n_norm_tiles)
    def _():
        h = h_ref[...]
        for c in range(n_sub):
            y = jnp.dot(h, w_ref[:, c * MXU_COLS:(c + 1) * MXU_COLS].astype(BF16), preferred_element_type=F32)
            for grp in range(MXU_COLS // LANES):
                src = slice(grp * LANES, (grp + 1) * LANES)
                dst = slice(c * MXU_COLS + grp * LANES, c * MXU_COLS + (grp + 1) * LANES)
                o_ref[:, dst] = _rms_rows(y[:, src], hg_ref[:, dst]).astype(o_ref.dtype)

    @pl.when(j >= n_norm_tiles)
    def _():
        h = h_ref[...]
        for c in range(n_sub):
            cols = slice(c * MXU_COLS, (c + 1) * MXU_COLS)
            o_ref[:, cols] = jnp.dot(h, w_ref[:, cols].astype(BF16),
                                     preferred_element_type=F32).astype(o_ref.dtype)


def _qkvg_proj(x2d, gain, w, layer, head_gain, n_cols, n_norm_cols, seq):
    T, D = x2d.shape
    tm = _tile(seq, 1024)
    tn = _tile(n_norm_cols // 2, 1024, MXU_COLS)
    blocks = [((tm, D), F32), ((1, D), F32), ((D, tn), w.dtype), ((1, tn), F32), ((tm, tn), BF16)]
    return pl.pallas_call(
        functools.partial(_qkvg_kernel, n_norm_tiles=n_norm_cols // tn),
        grid=(T // tm, n_cols // tn),
        in_specs=[
            pl.BlockSpec((tm, D), lambda i, j: (i, 0)),
            pl.BlockSpec((1, D), lambda i, j: (0, 0)),
            pl.BlockSpec((None, D, tn), lambda i, j: (layer, 0, j)),
            pl.BlockSpec((1, tn), lambda i, j: (0, j)),
        ],
        out_specs=pl.BlockSpec((tm, tn), lambda i, j: (i, j)),
        out_shape=jax.ShapeDtypeStruct((T, n_cols), BF16),
        scratch_shapes=[pltpu.VMEM((tm, D), BF16)],
        compiler_params=_params(("parallel", "arbitrary"),
                                _vmem_limit(blocks, [((tm, D), BF16)], [((tm, MXU_COLS), F32)] * 4)),
        name="qkvg_proj",
    )(x2d, gain, w, head_gain)


def _fgate_kernel(x_ref, gain_ref, wf_ref, bf_ref, pq_ref, pk_ref, cq_ref, ck_ref, qa_ref, ka_ref,
                  carry_ref, *, n_heads):
    @pl.when(pl.program_id(1) == 0)
    def _():
        carry_ref[...] = jnp.zeros_like(carry_ref)

    tm = x_ref.shape[0]
    h = _rms_rows(x_ref[...], gain_ref[...]).astype(BF16)
    f = jnp.dot(h, wf_ref[...], preferred_element_type=F32) + bf_ref[...]
    log_f = (jnp.minimum(f, 0.0) - jnp.log1p(jnp.exp(-jnp.abs(f)))) * LOG2E
    c = _cumsum_rows(log_f, _tril_mask(tm).astype(BF16)) + carry_ref[...]
    carry_ref[...] = c[tm - 1:tm, :]
    hi = c.astype(BF16)
    r1 = c - hi.astype(F32)
    mid = r1.astype(BF16)
    lo = (r1 - mid.astype(F32)).astype(BF16)
    lane = lax.broadcasted_iota(jnp.int32, c.shape, 1)
    parts = jnp.where(lane < n_heads, hi, jnp.where(lane < 2 * n_heads, mid, lo))
    qa_ref[...] = (jnp.dot(parts, pq_ref[...], preferred_element_type=F32) + cq_ref[...]).astype(qa_ref.dtype)
    ka_ref[...] = (jnp.dot(parts, pk_ref[...], preferred_element_type=F32) + ck_ref[...]).astype(ka_ref.dtype)


def _fgate_selectors(n_heads):
    W = n_heads * LANES
    pq = np.zeros((LANES, W), np.float32)
    pk = np.zeros((LANES, W), np.float32)
    cq = np.zeros((1, W), np.float32)
    ck = np.zeros((1, W), np.float32)
    for hd in range(n_heads):
        for part in range(3):
            pq[part * n_heads + hd, hd * LANES + part] = 1.0
            pk[part * n_heads + hd, hd * LANES + 3 + part] = -1.0
            cq[0, hd * LANES + 3 + part] = 1.0
            ck[0, hd * LANES + part] = 1.0
    return jnp.asarray(pq, BF16), jnp.asarray(pk, BF16), jnp.asarray(cq), jnp.asarray(ck)


def _fgate(x2d, gain, w_f, b_f, n_heads, batch, seq):
    T, D = x2d.shape
    tm = _tile(seq, 512)
    nt = seq // tm
    W = n_heads * LANES
    pq, pk, cq, ck = _fgate_selectors(n_heads)
    blocks = [((tm, D), F32), ((1, D), F32), ((D, LANES), BF16), ((1, LANES), F32),
              ((LANES, W), BF16), ((LANES, W), BF16), ((1, W), F32), ((1, W), F32),
              ((tm, W), BF16), ((tm, W), BF16)]
    const = lambda b, i: (0, 0)
    return pl.pallas_call(
        functools.partial(_fgate_kernel, n_heads=n_heads),
        grid=(batch, nt),
        in_specs=[
            pl.BlockSpec((tm, D), lambda b, i: (b * nt + i, 0)),
            pl.BlockSpec((1, D), const),
            pl.BlockSpec((D, LANES), const),
            pl.BlockSpec((1, LANES), const),
            pl.BlockSpec((LANES, W), const),
            pl.BlockSpec((LANES, W), const),
            pl.BlockSpec((1, W), const),
            pl.BlockSpec((1, W), const),
        ],
        out_specs=[pl.BlockSpec((tm, W), lambda b, i: (b * nt + i, 0)),
                   pl.BlockSpec((tm, W), lambda b, i: (b * nt + i, 0))],
        out_shape=[jax.ShapeDtypeStruct((T, W), BF16), jax.ShapeDtypeStruct((T, W), BF16)],
        scratch_shapes=[pltpu.VMEM((1, LANES), F32)],
        compiler_params=_params(("parallel", "arbitrary"),
                                _vmem_limit(blocks, [], [((tm, D), F32)] * 3 + [((tm, tm), BF16)])),
        name="fox_forget_cumsum",
    )(x2d, gain, w_f, b_f, pq, pk, cq, ck)


def _attn_kernel(q_ref, qa_ref, k_ref, ka_ref, v_ref, g_ref, o_ref, qs_ref, acc_ref, m_ref, l_ref, *, tk):
    qi = pl.program_id(2)
    tq = q_ref.shape[0]
    n_sub = q_ref.shape[1] // LANES
    for hh in range(n_sub):
        lanes = slice(hh * LANES, (hh + 1) * LANES)
        qs_ref[hh] = jnp.concatenate([q_ref[:, lanes], qa_ref[:, lanes]], axis=1)
    acc_ref[...] = jnp.zeros_like(acc_ref)
    m_ref[...] = jnp.full(m_ref.shape, -jnp.inf, F32)
    l_ref[...] = jnp.zeros_like(l_ref)

    def step(first_key, n_keys, diag_offset=None):
        rows = pl.ds(pl.multiple_of(first_key, tk), n_keys)
        for hh in range(n_sub):
            lanes = slice(hh * LANES, (hh + 1) * LANES)
            k = jnp.concatenate([k_ref[rows, lanes], ka_ref[rows, lanes]], axis=1)
            s = _dot_nt(qs_ref[hh], k)
            if diag_offset is not None:
                t_pos = lax.broadcasted_iota(jnp.int32, (tq, n_keys), 0)
                s_pos = lax.broadcasted_iota(jnp.int32, (tq, n_keys), 1) + diag_offset
                s = jnp.where(s_pos <= t_pos, s, -jnp.inf)
            m_prev = m_ref[hh]
            m_new = jnp.maximum(m_prev, jnp.max(s, axis=1, keepdims=True))
            alpha = jnp.exp2(m_prev - m_new)
            p = jnp.exp2(s - jnp.tile(m_new, (1, n_keys // LANES)))
            m_ref[hh] = m_new
            l_ref[hh] = alpha * l_ref[hh] + jnp.sum(p, axis=1, keepdims=True)
            acc_ref[hh] = alpha * acc_ref[hh] + jnp.dot(p.astype(BF16), v_ref[rows, lanes],
                                                        preferred_element_type=F32)

    def loop(n, body):
        lax.fori_loop(0, n, lambda j, c: (body(j), c)[1], 0)

    n_diag = tq // tk
    n_full = qi * n_diag
    loop(n_full // 2, lambda j: step(j * 2 * tk, 2 * tk))
    loop(n_full % 2, lambda j: step((n_full - 1) * tk, tk))
    for d in range(n_diag):
        step((n_full + d) * tk, tk, d * tk)
    for hh in range(n_sub):
        lanes = slice(hh * LANES, (hh + 1) * LANES)
        gate = jax.nn.sigmoid(g_ref[:, lanes].astype(F32))
        o_ref[:, lanes] = (acc_ref[hh] / l_ref[hh] * gate).astype(o_ref.dtype)


def _attention(qkvg, qa, ka, n_heads, batch, seq):
    T = qkvg.shape[0]
    tq = _tile(seq, 512)
    nq = seq // tq
    width = ATTN_HEADS_PER_STEP * LANES
    ng = n_heads * LANES // width
    blocks = [((tq, width), BF16)] * 2 + [((seq, width), BF16)] * 3 + [((tq, width), BF16)] * 2
    scratch = [((width // LANES, tq, 2 * LANES), BF16)] + [((width // LANES, tq, LANES), F32)] * 3
    return pl.pallas_call(
        functools.partial(_attn_kernel, tk=_tile(tq, ATTN_KEY_BLOCK)),
        grid=(batch, ng, nq),
        in_specs=[
            pl.BlockSpec((tq, width), lambda b, h, i: (b * nq + i, h)),
            pl.BlockSpec((tq, width), lambda b, h, i: (b * nq + i, h)),
            pl.BlockSpec((seq, width), lambda b, h, i: (b, h + ng)),
            pl.BlockSpec((seq, width), lambda b, h, i: (b, h)),
            pl.BlockSpec((seq, width), lambda b, h, i: (b, h + 2 * ng)),
            pl.BlockSpec((tq, width), lambda b, h, i: (b * nq + i, h + 3 * ng)),
        ],
        out_specs=pl.BlockSpec((tq, width), lambda b, h, i: (b * nq + i, h)),
        out_shape=jax.ShapeDtypeStruct((T, n_heads * LANES), BF16),
        scratch_shapes=[pltpu.VMEM(sh, dt) for sh, dt in scratch],
        compiler_params=_params(("parallel", "parallel", "arbitrary"),
                                _vmem_limit(blocks, scratch, [((tq, 2 * tq), F32)] * 2 * ATTN_HEADS_PER_STEP)),
        name="fox_attention",
    )(qkvg, qa, qkvg, ka, qkvg, qkvg)


def _mixer_ab(x2d, gain, w_in, j, sp_w, sp_b, v_gain, gamma, o_gain, w_out, layer, batch, seq):
    a_width = v_gain.shape[1]
    b_width = gamma.shape[1]
    y_a = _gmlp_proj(x2d, gain, w_in, j, v_gain, sp_w, sp_b, seq)
    qfig = _norm_matmul(x2d, gain, w_in, j, 2 * a_width, 4 * b_width, seq, F32, "hgrn_in_proj")
    y_b = _hgrn(qfig, gamma, o_gain, layer, batch, seq)
    return _matmul_residual([y_a, y_b], w_out, j, x2d, seq, "ab_out_proj")


def _mixer_c(x2d, gain, w_in, w_forget, j, b_f, q_gain, k_gain, w_out, batch, seq):
    D = x2d.shape[1]
    n_heads = b_f.shape[1]
    head_dim = q_gain.shape[1]
    assert head_dim == LANES and n_heads * head_dim == D
    scale = head_dim ** -0.5 * LOG2E
    head_gain = jnp.concatenate([jnp.tile(q_gain * scale, (1, n_heads)), jnp.tile(k_gain, (1, n_heads)),
                                 jnp.ones((1, 2 * D), F32)], axis=1)
    qkvg = _qkvg_proj(x2d, gain, w_in, j, head_gain, 4 * D, 2 * D, seq)
    assert 3 * n_heads <= LANES
    pad = ((0, 0), (0, LANES - 3 * n_heads))
    w_f = jnp.pad(jnp.tile(w_forget, (1, 3)), pad).astype(BF16)
    b_fp = jnp.pad(jnp.tile(b_f, (1, 3)), pad)
    qa, ka = _fgate(x2d, gain, w_f, b_fp, n_heads, batch, seq)
    o = _attention(qkvg, qa, ka, n_heads, batch, seq)
    return _matmul_residual([o], w_out, j, x2d, seq, "attn_out_proj")


def kernel(x, mix_norm, ab_w_in, ab_sp_w, ab_sp_b, ab_v_norm, hgrn_gamma, hgrn_o_norm, ab_w_out,
           c_w_in, c_b_f, c_q_norm, c_k_norm, c_w_out, ffn_norm, ffn_w_up, ffn_conv_w, ffn_conv_b,
           ffn_w_down):
    batch, seq, D = x.shape
    depth = mix_norm.shape[0]
    x2d = x.reshape(batch * seq, D)
    ab_w_in_b = ab_w_in.astype(BF16)
    c_w_in_b = c_w_in.astype(BF16)
    ffn_w_down = ffn_w_down.astype(BF16)
    for l in range(depth):
        j = l // 2
        gain = mix_norm[l][None, :]
        if l % 2 == 0:
            x2d = _mixer_ab(x2d, gain, ab_w_in_b, j, ab_sp_w[j], ab_sp_b[j], ab_v_norm[j][None, :],
                            hgrn_gamma, hgrn_o_norm[j][None, :], ab_w_out, l, batch, seq)
        else:
            x2d = _mixer_c(x2d, gain, c_w_in_b, c_w_in[j, :, 4 * D:], j, c_b_f[j][None, :],
                           c_q_norm[j][None, :], c_k_norm[j][None, :], c_w_out, batch, seq)
        x2d = _conv_ffn(x2d, ffn_norm[l][None, :], ffn_w_up, l, ffn_conv_w[l],
                        ffn_conv_b[l][None, :], ffn_w_down, seq)
    return x2d.reshape(batch, seq, D)
```

```python
import functools
import math

import jax
import jax.numpy as jnp
import numpy as np
from jax import lax
from jax.experimental import pallas as pl
from jax.experimental.pallas import tpu as pltpu

F32 = jnp.float32
BF16 = jnp.bfloat16
RMS_EPS = 1e-6
LANES = 128
SUBLANES = 8
MXU_COLS = 256
HGRN_CHUNK = 64
HGRN_MAX_FACTORED_RANGE = 60.0
ATTN_BLOCK = 512
ATTN_HEADS_PER_STEP = 4
LOG2E = 1.4426950408889634
V7X_VMEM_BYTES = 64 * 1024 * 1024
VMEM_CAP_BYTES = V7X_VMEM_BYTES - 6 * 1024 * 1024


def _nbytes(shape, dtype):
    return math.prod(shape) * jnp.dtype(dtype).itemsize


def _vmem_limit(blocks, scratch=(), temps=()):
    est = 2 * sum(_nbytes(s, d) for s, d in blocks)
    est += sum(_nbytes(s, d) for s, d in scratch) + sum(_nbytes(s, d) for s, d in temps)
    return int(min(VMEM_CAP_BYTES, est * 5 // 4 + (4 << 20)))


def _tile(n, pref, mult=LANES):
    if n <= pref:
        return n
    t = (pref // mult) * mult
    while n % t:
        t -= mult
    return t


def _params(sem, limit):
    return pltpu.CompilerParams(dimension_semantics=sem, vmem_limit_bytes=limit)


def _rms_rows(xf, gain):
    ms = jnp.mean(xf * xf, axis=-1, keepdims=True)
    return xf * lax.rsqrt(ms + RMS_EPS) * gain


def _gelu(x):
    return 0.5 * x * (1.0 + lax.erf(x * (2.0 ** -0.5)))


def _fill_normed(x_ref, gain_ref, h_ref, rows=256):
    rows = min(rows, x_ref.shape[0])

    def body(r, carry):
        sl = pl.ds(pl.multiple_of(r * rows, rows), rows)
        h_ref[sl, :] = _rms_rows(x_ref[sl, :], gain_ref[...]).astype(h_ref.dtype)
        return carry

    lax.fori_loop(0, x_ref.shape[0] // rows, body, 0)


def _tril_mask(n):
    t = lax.broadcasted_iota(jnp.int32, (n, n), 0)
    s = lax.broadcasted_iota(jnp.int32, (n, n), 1)
    return s <= t


def _cumsum_rows(x, tril_bf16):
    hi = x.astype(BF16)
    r1 = x - hi.astype(F32)
    mid = r1.astype(BF16)
    lo = (r1 - mid.astype(F32)).astype(BF16)
    y = jnp.dot(tril_bf16, jnp.concatenate([hi, mid, lo], axis=1), preferred_element_type=F32)
    d = x.shape[1]
    return y[:, :d] + y[:, d:2 * d] + y[:, 2 * d:]


def _dot_nt(a, b):
    return lax.dot_general(a, b, (((1,), (1,)), ((), ())), preferred_element_type=F32)


def _dot_tn(a, b):
    return lax.dot_general(a, b, (((0,), (0,)), ((), ())), preferred_element_type=F32)


def _gmlp_kernel(x_ref, gain_ref, wu_ref, wv_ref, vg_ref, spw_ref, spb_ref, o_ref, h_ref, *, chunk):
    @pl.when(pl.program_id(1) == 0)
    def _():
        _fill_normed(x_ref, gain_ref, h_ref)

    h = h_ref[...]
    u = _gelu(jnp.dot(h, wu_ref[...].astype(BF16), preferred_element_type=F32))
    v = _gelu(jnp.dot(h, wv_ref[...].astype(BF16), preferred_element_type=F32))
    tm, tn = u.shape
    n_chunks = tm // chunk
    tril = _tril_mask(chunk)
    for hh in range(tn // LANES):
        lanes = slice(hh * LANES, (hh + 1) * LANES)
        vh = _rms_rows(v[:, lanes], vg_ref[:, lanes]).astype(BF16)
        vcat = jnp.concatenate([vh[c * chunk:(c + 1) * chunk, :] for c in range(n_chunks)], axis=1)
        w_causal = jnp.where(tril, spw_ref[hh], 0.0).astype(BF16)
        mixed = jnp.dot(w_causal, vcat, preferred_element_type=F32)
        for c in range(n_chunks):
            rows = slice(c * chunk, (c + 1) * chunk)
            m_c = mixed[:, c * LANES:(c + 1) * LANES] + spb_ref[hh]
            o_ref[rows, lanes] = (u[rows, lanes] * m_c).astype(o_ref.dtype)


def _gmlp_proj(x2d, gain, w_in, layer, v_gain, sp_w, sp_b, seq):
    T, D = x2d.shape
    n_heads, chunk, _ = sp_w.shape
    a_width = n_heads * LANES
    tm = _tile(seq, 1024, chunk)
    tn = _tile(a_width, 512)
    nj = a_width // tn
    spb = jnp.broadcast_to(sp_b[:, :, None], (n_heads, chunk, LANES))
    blocks = [((tm, D), F32), ((1, D), F32), ((D, tn), w_in.dtype), ((D, tn), w_in.dtype), ((1, tn), F32),
              ((tn // LANES, chunk, chunk), F32), ((tn // LANES, chunk, LANES), F32), ((tm, tn), BF16)]
    return pl.pallas_call(
        functools.partial(_gmlp_kernel, chunk=chunk),
        grid=(T // tm, nj),
        in_specs=[
            pl.BlockSpec((tm, D), lambda i, j: (i, 0)),
            pl.BlockSpec((1, D), lambda i, j: (0, 0)),
            pl.BlockSpec((None, D, tn), lambda i, j: (layer, 0, j)),
            pl.BlockSpec((None, D, tn), lambda i, j: (layer, 0, j + nj)),
            pl.BlockSpec((1, tn), lambda i, j: (0, j)),
            pl.BlockSpec((tn // LANES, chunk, chunk), lambda i, j: (j, 0, 0)),
            pl.BlockSpec((tn // LANES, chunk, LANES), lambda i, j: (j, 0, 0)),
        ],
        out_specs=pl.BlockSpec((tm, tn), lambda i, j: (i, j)),
        out_shape=jax.ShapeDtypeStruct((T, a_width), BF16),
        scratch_shapes=[pltpu.VMEM((tm, D), BF16)],
        compiler_params=_params(("parallel", "arbitrary"),
                                _vmem_limit(blocks, [((tm, D), BF16)], [((tm, tn), F32)] * 6)),
        name="gmlp_proj",
    )(x2d, gain, w_in, w_in, v_gain, sp_w, spb)


def _norm_mm_kernel(x_ref, gain_ref, w_ref, o_ref, h_ref):
    @pl.when(pl.program_id(1) == 0)
    def _():
        _fill_normed(x_ref, gain_ref, h_ref)

    o_ref[...] = jnp.dot(h_ref[...], w_ref[...].astype(BF16), preferred_element_type=F32).astype(o_ref.dtype)


def _norm_matmul(x2d, gain, w, layer, col0, n_cols, seq, out_dtype, name):
    T, D = x2d.shape
    tm = _tile(seq, 1024)
    tn = _tile(n_cols, 1024)
    assert col0 % tn == 0
    j0 = col0 // tn
    blocks = [((tm, D), F32), ((1, D), F32), ((D, tn), w.dtype), ((tm, tn), out_dtype)]
    return pl.pallas_call(
        _norm_mm_kernel,
        grid=(T // tm, n_cols // tn),
        in_specs=[
            pl.BlockSpec((tm, D), lambda i, j: (i, 0)),
            pl.BlockSpec((1, D), lambda i, j: (0, 0)),
            pl.BlockSpec((None, D, tn), lambda i, j: (layer, 0, j + j0)),
        ],
        out_specs=pl.BlockSpec((tm, tn), lambda i, j: (i, j)),
        out_shape=jax.ShapeDtypeStruct((T, n_cols), out_dtype),
        scratch_shapes=[pltpu.VMEM((tm, D), BF16)],
        compiler_params=_params(("parallel", "arbitrary"),
                                _vmem_limit(blocks, [((tm, D), BF16)], [((tm, tn), F32)])),
        name=name,
    )(x2d, gain, w)


def _hgrn_kernel(gamma_ref, q_ref, f_ref, i_ref, g_ref, og_ref, o_ref, st_ref, g_scr, k_scr, intra_scr, inter_scr,
                 *, layer, chunk):
    @pl.when(pl.program_id(2) == 0)
    def _():
        st_ref[...] = jnp.zeros_like(st_ref)

    gam = gamma_ref[...]
    ex = jnp.exp(gam - jnp.max(gam, axis=0, keepdims=True))
    lb_all = jnp.sum(ex[:layer + 1], axis=0, keepdims=True) / jnp.sum(ex, axis=0, keepdims=True)

    L, width = q_ref.shape
    n_chunks = L // chunk
    mid = chunk // 2 - 1
    row_chunk = lax.broadcasted_iota(jnp.int32, (L, L), 0) // chunk
    col_chunk = lax.broadcasted_iota(jnp.int32, (L, L), 1) // chunk
    mask = _tril_mask(L) & (row_chunk == col_chunk)
    chunk_of_row = lax.broadcasted_iota(jnp.int32, (L, LANES), 0) // chunk

    fl = f_ref[...]
    e = jnp.exp(-jnp.abs(fl))
    r = 1.0 / (1.0 + e)
    pos = fl >= 0
    sig = jnp.where(pos, r, e * r)
    nsig = jnp.where(pos, e * r, r)
    kk = (1.0 - lb_all) * nsig
    G = _cumsum_rows(jnp.log(lb_all + (1.0 - lb_all) * sig), mask.astype(BF16))
    g_scr[...] = G
    k_scr[...] = kk
    g_mid_rows = [G[c * chunk + mid:c * chunk + mid + 1, :] for c in range(n_chunks)]
    g_end_rows = [G[(c + 1) * chunk - 1:(c + 1) * chunk, :] for c in range(n_chunks)]
    per_row = lambda rows: jnp.concatenate([jnp.broadcast_to(x, (chunk, width)) for x in rows], axis=0)
    g_mid = per_row(g_mid_rows)
    g_end = per_row(g_end_rows)
    qv = q_ref[...]
    q_t = (qv * jnp.exp(G - g_mid)).astype(BF16)
    k_t = (kk * jnp.exp(g_mid - G)).astype(BF16)
    q_g = (qv * jnp.exp(G)).astype(BF16)
    k_end = (kk * jnp.exp(g_end - G)).astype(BF16)
    vv = i_ref[...].astype(BF16)
    decay_range = -jnp.min(jnp.concatenate(g_end_rows, axis=0))
    factorable = decay_range <= HGRN_MAX_FACTORED_RANGE

    def finish(o, lanes):
        gv = g_ref[:, lanes]
        o_ref[:, lanes] = (_rms_rows(o, og_ref[...]) * (gv * jax.nn.sigmoid(gv))).astype(o_ref.dtype)

    for hh in range(width // LANES):
        lanes = slice(hh * LANES, (hh + 1) * LANES)
        scores = jnp.where(mask, _dot_nt(q_t[:, lanes], k_t[:, lanes]), 0.0).astype(BF16)
        intra = jnp.dot(scores, vv[:, lanes], preferred_element_type=F32)
        v_exp = jnp.concatenate([jnp.where(chunk_of_row == c, vv[:, lanes], 0) for c in range(n_chunks)],
                                axis=1)
        u_t = _dot_tn(v_exp, k_end[:, lanes])
        st = st_ref[hh]
        prev = []
        for c in range(n_chunks):
            prev.append(st.astype(BF16))
            st = st * jnp.exp(g_end_rows[c][:, lanes]) + u_t[c * LANES:(c + 1) * LANES, :]
        st_ref[hh] = st
        q_exp = jnp.concatenate([jnp.where(chunk_of_row == c, q_g[:, lanes], 0) for c in range(n_chunks)],
                                axis=1)
        inter = _dot_nt(q_exp, jnp.concatenate(prev, axis=1))
        inter_scr[hh] = inter
        finish(intra + inter, lanes)

    @pl.when(jnp.logical_not(factorable))
    def _():
        for hh in range(width // LANES):
            _hgrn_intra_pairwise(q_ref, i_ref, g_scr, k_scr, intra_scr, hh, chunk)
            finish(intra_scr[hh] + inter_scr[hh], slice(hh * LANES, (hh + 1) * LANES))


def _hgrn_intra_pairwise(q_ref, i_ref, g_scr, k_scr, intra_scr, hh, chunk):
    lanes = slice(hh * LANES, (hh + 1) * LANES)
    t_idx = lax.broadcasted_iota(jnp.int32, (chunk, 1), 0)
    for c in range(q_ref.shape[0] // chunk):
        rows = slice(c * chunk, (c + 1) * chunk)
        g_c = g_scr[rows, lanes]
        q_c = q_ref[rows, lanes]

        def body(grp, acc):
            keys = pl.ds(pl.multiple_of(c * chunk + grp * SUBLANES, SUBLANES), SUBLANES)
            g_s, k_s, v_s = g_scr[keys, lanes], k_scr[keys, lanes], i_ref[keys, lanes]
            for j in range(SUBLANES):
                w = q_c * k_s[j:j + 1] * jnp.exp(jnp.minimum(g_c - g_s[j:j + 1], 0.0))
                score = jnp.where(t_idx >= grp * SUBLANES + j, jnp.sum(w, axis=1, keepdims=True), 0.0)
                acc = acc + score * v_s[j:j + 1]
            return acc

        intra_scr[hh, rows, :] = lax.fori_loop(0, chunk // SUBLANES, body, jnp.zeros((chunk, LANES), F32))


def _hgrn(qfig, gamma, o_gain, layer, batch, seq):
    T = qfig.shape[0]
    b_width = gamma.shape[1]
    n_layers = gamma.shape[0]
    width = _tile(b_width, 512)
    L = _tile(seq, 256, HGRN_CHUNK)
    nw = b_width // width
    nl = seq // L
    blocks = [((L, width), F32)] * 4 + [((L, width), BF16), ((n_layers, width), F32)]
    n_sub = width // LANES
    scratch = [((n_sub, LANES, LANES), F32), ((L, width), F32), ((L, width), F32), ((n_sub, L, LANES), F32),
               ((n_sub, L, LANES), F32)]
    row = lambda b, h, l: b * nl + l
    return pl.pallas_call(
        functools.partial(_hgrn_kernel, layer=layer, chunk=HGRN_CHUNK),
        grid=(batch, nw, nl),
        in_specs=[
            pl.BlockSpec((n_layers, width), lambda b, h, l: (0, h)),
            pl.BlockSpec((L, width), lambda b, h, l: (row(b, h, l), h)),
            pl.BlockSpec((L, width), lambda b, h, l: (row(b, h, l), h + nw)),
            pl.BlockSpec((L, width), lambda b, h, l: (row(b, h, l), h + 2 * nw)),
            pl.BlockSpec((L, width), lambda b, h, l: (row(b, h, l), h + 3 * nw)),
            pl.BlockSpec((1, LANES), lambda b, h, l: (0, 0)),
        ],
        out_specs=pl.BlockSpec((L, width), lambda b, h, l: (row(b, h, l), h)),
        out_shape=jax.ShapeDtypeStruct((T, b_width), BF16),
        scratch_shapes=[pltpu.VMEM(sh, dt) for sh, dt in scratch],
        compiler_params=_params(("parallel", "parallel", "arbitrary"),
                                _vmem_limit(blocks, scratch, [((L, width), F32)] * 12)),
        name="hgrn2",
    )(gamma, qfig, qfig, qfig, qfig, o_gain)


def _mm_resid_kernel(*refs, n_lhs):
    lhs_refs = refs[:n_lhs]
    w_ref, r_ref, o_ref = refs[n_lhs:]
    acc = r_ref[...]
    k0 = 0
    for a_ref in lhs_refs:
        k = a_ref.shape[1]
        acc = acc + jnp.dot(a_ref[...], w_ref[k0:k0 + k, :].astype(BF16), preferred_element_type=F32)
        k0 += k
    o_ref[...] = acc


def _matmul_residual(lhs_list, w, layer, resid, seq, name, tn_pref=512):
    T, N = resid.shape
    K = w.shape[1]
    tm = _tile(seq, 1024)
    tn = _tile(N, tn_pref)
    blocks = [((tm, a.shape[1]), BF16) for a in lhs_list] + [((K, tn), w.dtype), ((tm, tn), F32), ((tm, tn), F32)]
    return pl.pallas_call(
        functools.partial(_mm_resid_kernel, n_lhs=len(lhs_list)),
        grid=(T // tm, N // tn),
        in_specs=[pl.BlockSpec((tm, a.shape[1]), lambda i, j: (i, 0)) for a in lhs_list] + [
            pl.BlockSpec((None, K, tn), lambda i, j: (layer, 0, j)),
            pl.BlockSpec((tm, tn), lambda i, j: (i, j)),
        ],
        out_specs=pl.BlockSpec((tm, tn), lambda i, j: (i, j)),
        out_shape=jax.ShapeDtypeStruct((T, N), F32),
        compiler_params=_params(("parallel", "arbitrary"), _vmem_limit(blocks, [], [((tm, tn), F32)])),
        name=name,
    )(*lhs_list, w, resid)


def _ffn_up_kernel(x_ref, xh_ref, gain_ref, wa_ref, wb_ref, cwa_ref, cwb_ref, cba_ref, cbb_ref,
                   o_ref, h_ref, za_ref, *, n_tiles, n_col_tiles, tiles_per_seq, halo):
    s = pl.program_id(0)
    cur = jnp.minimum(s, n_tiles - 1)
    prev = jnp.maximum(s - 1, 0)
    row_c = cur // n_col_tiles
    tm = x_ref.shape[0]

    @pl.when(s == 0)
    def _():
        za_ref[...] = jnp.zeros_like(za_ref)

    @pl.when((cur % n_col_tiles == 0) & (s < n_tiles))
    def _():
        slot = row_c % 2
        keep = (row_c % tiles_per_seq != 0).astype(F32)
        h_ref[slot, 0:halo, :] = (_rms_rows(xh_ref[...], gain_ref[...]) * keep).astype(h_ref.dtype)
        rows = min(256, tm)

        def body(r, carry):
            src = pl.ds(pl.multiple_of(r * rows, rows), rows)
            dst = pl.ds(pl.multiple_of(r * rows + halo, halo), rows)
            h_ref[slot, dst, :] = _rms_rows(x_ref[src, :], gain_ref[...]).astype(h_ref.dtype)
            return carry

        lax.fori_loop(0, tm // rows, body, 0)

    def conv(z, cw_ref, cb_ref):
        z1 = pltpu.roll(z, 1, 0)
        z2 = pltpu.roll(z, 2, 0)
        y = cw_ref[0:1, :] * z2 + cw_ref[1:2, :] * z1 + cw_ref[2:3, :] * z + cb_ref[...]
        return y[halo:, :]

    h_prev = h_ref[(prev // n_col_tiles) % 2]
    zb = jnp.dot(h_prev, wb_ref[...].astype(BF16), preferred_element_type=F32)
    a = conv(za_ref[...], cwa_ref, cba_ref)
    o_ref[...] = (a * jax.nn.sigmoid(a) * conv(zb, cwb_ref, cbb_ref)).astype(o_ref.dtype)
    za_ref[...] = jnp.dot(h_ref[row_c % 2], wa_ref[...].astype(BF16), preferred_element_type=F32)


def _ffn_up(x2d, gain, w_up, layer, conv_w, conv_b, seq):
    T, D = x2d.shape
    F = w_up.shape[2] // 2
    halo = 16
    tm = _tile(seq, 1024)
    tn = _tile(F, 512, MXU_COLS)
    nj = F // tn
    n_tiles = (T // tm) * nj
    tiles_per_seq = seq // tm
    blocks = [((tm, D), F32), ((halo, D), F32), ((1, D), F32), ((D, tn), F32), ((D, tn), F32),
              ((3, tn), F32), ((3, tn), F32), ((1, tn), F32), ((1, tn), F32), ((tm, tn), BF16)]
    scratch = [((2, tm + halo, D), BF16), ((tm + halo, tn), F32)]
    hpt = tm // halo
    cur = lambda s: jnp.minimum(s, n_tiles - 1)
    prev = lambda s: jnp.maximum(s - 1, 0)
    return pl.pallas_call(
        functools.partial(_ffn_up_kernel, n_tiles=n_tiles, n_col_tiles=nj, tiles_per_seq=tiles_per_seq,
                          halo=halo),
        grid=(n_tiles + 1,),
        in_specs=[
            pl.BlockSpec((tm, D), lambda s: (cur(s) // nj, 0)),
            pl.BlockSpec((halo, D), lambda s: (jnp.maximum(cur(s) // nj * hpt - 1, 0), 0)),
            pl.BlockSpec((1, D), lambda s: (0, 0)),
            pl.BlockSpec((None, D, tn), lambda s: (layer, 0, cur(s) % nj)),
            pl.BlockSpec((None, D, tn), lambda s: (layer, 0, prev(s) % nj + nj)),
            pl.BlockSpec((3, tn), lambda s: (0, prev(s) % nj)),
            pl.BlockSpec((3, tn), lambda s: (0, prev(s) % nj + nj)),
            pl.BlockSpec((1, tn), lambda s: (0, prev(s) % nj)),
            pl.BlockSpec((1, tn), lambda s: (0, prev(s) % nj + nj)),
        ],
        out_specs=pl.BlockSpec((tm, tn), lambda s: (prev(s) // nj, prev(s) % nj)),
        out_shape=jax.ShapeDtypeStruct((T, F), BF16),
        scratch_shapes=[pltpu.VMEM(sh, dt) for sh, dt in scratch],
        compiler_params=_params(("arbitrary",),
                                _vmem_limit(blocks, scratch, [((tm + halo, MXU_COLS), F32)] * 8)),
        name="ffn_up_conv_gate",
    )(x2d, x2d, gain, w_up, w_up, conv_w, conv_w, conv_b, conv_b)


def _conv_ffn(x2d, gain, w_up, layer, conv_w, conv_b, w_down, seq):
    act = _ffn_up(x2d, gain, w_up, layer, conv_w, conv_b, seq)
    return _matmul_residual([act], w_down, layer, x2d, seq, "ffn_down", tn_pref=512)


def _qkvg_kernel(x_ref, gain_ref, w_ref, hg_ref, o_ref, h_ref, *, n_norm_tiles):
    j = pl.program_id(1)
    n_sub = o_ref.shape[1] // MXU_COLS

    @pl.when(j == 0)
    def _():
        _fill_normed(x_ref, gain_ref, h_ref)

    @pl.when(j < n_norm_tiles)
    def _():
        h = h_ref[...]
        for c in range(n_sub):
            y = jnp.dot(h, w_ref[:, c * MXU_COLS:(c + 1) * MXU_COLS].astype(BF16), preferred_element_type=F32)
            for grp in range(MXU_COLS // LANES):
                src = slice(grp * LANES, (grp + 1) * LANES)
                dst = slice(c * MXU_COLS + grp * LANES, c * MXU_COLS + (grp + 1) * LANES)
                o_ref[:, dst] = _rms_rows(y[:, src], hg_ref[:, dst]).astype(o_ref.dtype)

    @pl.when(j >= n_norm_tiles)
    def _():
        h = h_ref[...]
        for c in range(n_sub):
            cols = slice(c * MXU_COLS, (c + 1) * MXU_COLS)
            o_ref[:, cols] = jnp.dot(h, w_ref[:, cols].astype(BF16),
                                     preferred_element_type=F32).astype(o_ref.dtype)


def _qkvg_proj(x2d, gain, w, layer, head_gain, n_cols, n_norm_cols, seq):
    T, D = x2d.shape
    tm = _tile(seq, 1024)
    tn = _tile(n_norm_cols // 2, 2048, MXU_COLS)
    blocks = [((tm, D), F32), ((1, D), F32), ((D, tn), w.dtype), ((1, tn), F32), ((tm, tn), BF16)]
    return pl.pallas_call(
        functools.partial(_qkvg_kernel, n_norm_tiles=n_norm_cols // tn),
        grid=(T // tm, n_cols // tn),
        in_specs=[
            pl.BlockSpec((tm, D), lambda i, j: (i, 0)),
            pl.BlockSpec((1, D), lambda i, j: (0, 0)),
            pl.BlockSpec((None, D, tn), lambda i, j: (layer, 0, j)),
            pl.BlockSpec((1, tn), lambda i, j: (0, j)),
        ],
        out_specs=pl.BlockSpec((tm, tn), lambda i, j: (i, j)),
        out_shape=jax.ShapeDtypeStruct((T, n_cols), BF16),
        scratch_shapes=[pltpu.VMEM((tm, D), BF16)],
        compiler_params=_params(("parallel", "arbitrary"),
                                _vmem_limit(blocks, [((tm, D), BF16)], [((tm, MXU_COLS), F32)] * 4)),
        name="qkvg_proj",
    )(x2d, gain, w, head_gain)


def _fgate_kernel(x_ref, gain_ref, wf_ref, bf_ref, pq_ref, pk_ref, cq_ref, ck_ref, qa_ref, ka_ref,
                  carry_ref, *, n_heads):
    @pl.when(pl.program_id(1) == 0)
    def _():
        carry_ref[...] = jnp.zeros_like(carry_ref)

    tm = x_ref.shape[0]
    h = _rms_rows(x_ref[...], gain_ref[...]).astype(BF16)
    f = jnp.dot(h, wf_ref[...], preferred_element_type=F32) + bf_ref[...]
    log_f = (jnp.minimum(f, 0.0) - jnp.log1p(jnp.exp(-jnp.abs(f)))) * LOG2E
    c = _cumsum_rows(log_f, _tril_mask(tm).astype(BF16)) + carry_ref[...]
    carry_ref[...] = c[tm - 1:tm, :]
    hi = c.astype(BF16)
    r1 = c - hi.astype(F32)
    mid = r1.astype(BF16)
    lo = (r1 - mid.astype(F32)).astype(BF16)
    lane = lax.broadcasted_iota(jnp.int32, c.shape, 1)
    parts = jnp.where(lane < n_heads, hi, jnp.where(lane < 2 * n_heads, mid, lo))
    qa_ref[...] = (jnp.dot(parts, pq_ref[...], preferred_element_type=F32) + cq_ref[...]).astype(qa_ref.dtype)
    ka_ref[...] = (jnp.dot(parts, pk_ref[...], preferred_element_type=F32) + ck_ref[...]).astype(ka_ref.dtype)


def _fgate_selectors(n_heads):
    W = n_heads * LANES
    pq = np.zeros((LANES, W), np.float32)
    pk = np.zeros((LANES, W), np.float32)
    cq = np.zeros((1, W), np.float32)
    ck = np.zeros((1, W), np.float32)
    for hd in range(n_heads):
        for part in range(3):
            pq[part * n_heads + hd, hd * LANES + part] = 1.0
            pk[part * n_heads + hd, hd * LANES + 3 + part] = -1.0
            cq[0, hd * LANES + 3 + part] = 1.0
            ck[0, hd * LANES + part] = 1.0
    return jnp.asarray(pq, BF16), jnp.asarray(pk, BF16), jnp.asarray(cq), jnp.asarray(ck)


def _fgate(x2d, gain, w_f, b_f, n_heads, batch, seq):
    T, D = x2d.shape
    tm = _tile(seq, 512)
    nt = seq // tm
    W = n_heads * LANES
    pq, pk, cq, ck = _fgate_selectors(n_heads)
    blocks = [((tm, D), F32), ((1, D), F32), ((D, LANES), BF16), ((1, LANES), F32),
              ((LANES, W), BF16), ((LANES, W), BF16), ((1, W), F32), ((1, W), F32),
              ((tm, W), BF16), ((tm, W), BF16)]
    const = lambda b, i: (0, 0)
    return pl.pallas_call(
        functools.partial(_fgate_kernel, n_heads=n_heads),
        grid=(batch, nt),
        in_specs=[
            pl.BlockSpec((tm, D), lambda b, i: (b * nt + i, 0)),
            pl.BlockSpec((1, D), const),
            pl.BlockSpec((D, LANES), const),
            pl.BlockSpec((1, LANES), const),
            pl.BlockSpec((LANES, W), const),
            pl.BlockSpec((LANES, W), const),
            pl.BlockSpec((1, W), const),
            pl.BlockSpec((1, W), const),
        ],
        out_specs=[pl.BlockSpec((tm, W), lambda b, i: (b * nt + i, 0)),
                   pl.BlockSpec((tm, W), lambda b, i: (b * nt + i, 0))],
        out_shape=[jax.ShapeDtypeStruct((T, W), BF16), jax.ShapeDtypeStruct((T, W), BF16)],
        scratch_shapes=[pltpu.VMEM((1, LANES), F32)],
        compiler_params=_params(("parallel", "arbitrary"),
                                _vmem_limit(blocks, [], [((tm, D), F32)] * 3 + [((tm, tm), BF16)])),
        name="fox_forget_cumsum",
    )(x2d, gain, w_f, b_f, pq, pk, cq, ck)


def _attn_kernel(q_ref, qa_ref, k_ref, ka_ref, v_ref, g_ref, o_ref, qs_ref, acc_ref, m_ref, l_ref):
    qi = pl.program_id(2)
    tq = tk = q_ref.shape[0]
    n_sub = q_ref.shape[1] // LANES
    for hh in range(n_sub):
        lanes = slice(hh * LANES, (hh + 1) * LANES)
        qs_ref[hh] = jnp.concatenate([q_ref[:, lanes], qa_ref[:, lanes]], axis=1)
    acc_ref[...] = jnp.zeros_like(acc_ref)
    m_ref[...] = jnp.full(m_ref.shape, -jnp.inf, F32)
    l_ref[...] = jnp.zeros_like(l_ref)

    def step(first_key, n_keys, diag_offset=None):
        rows = pl.ds(pl.multiple_of(first_key, tk), n_keys)
        for hh in range(n_sub):
            lanes = slice(hh * LANES, (hh + 1) * LANES)
            k = jnp.concatenate([k_ref[rows, lanes], ka_ref[rows, lanes]], axis=1)
            s = _dot_nt(qs_ref[hh], k)
            if diag_offset is not None:
                t_pos = lax.broadcasted_iota(jnp.int32, (tq, n_keys), 0)
                s_pos = lax.broadcasted_iota(jnp.int32, (tq, n_keys), 1) + diag_offset
                s = jnp.where(s_pos <= t_pos, s, -jnp.inf)
            m_prev = m_ref[hh]
            m_new = jnp.maximum(m_prev, jnp.max(s, axis=1, keepdims=True))
            alpha = jnp.exp2(m_prev - m_new)
            p = jnp.exp2(s - jnp.tile(m_new, (1, n_keys // LANES)))
            m_ref[hh] = m_new
            l_ref[hh] = alpha * l_ref[hh] + jnp.sum(p, axis=1, keepdims=True)
            acc_ref[hh] = alpha * acc_ref[hh] + jnp.dot(p.astype(BF16), v_ref[rows, lanes],
                                                        preferred_element_type=F32)

    def loop(n, body):
        lax.fori_loop(0, n, lambda j, c: (body(j), c)[1], 0)

    loop(qi // 2, lambda j: step(j * 2 * tk, 2 * tk))

    @pl.when(qi % 2 == 1)
    def _():
        step((qi - 1) * tk, 2 * tk, -tk)

    @pl.when(qi % 2 == 0)
    def _():
        step(qi * tk, tk, 0)
    for hh in range(n_sub):
        lanes = slice(hh * LANES, (hh + 1) * LANES)
        gate = jax.nn.sigmoid(g_ref[:, lanes].astype(F32))
        o_ref[:, lanes] = (acc_ref[hh] / l_ref[hh] * gate).astype(o_ref.dtype)


def _attention(qkvg, qa, ka, n_heads, batch, seq):
    T = qkvg.shape[0]
    tq = _tile(seq, ATTN_BLOCK)
    nq = seq // tq
    width = ATTN_HEADS_PER_STEP * LANES
    ng = n_heads * LANES // width
    blocks = [((tq, width), BF16)] * 2 + [((seq, width), BF16)] * 3 + [((tq, width), BF16)] * 2
    scratch = [((width // LANES, tq, 2 * LANES), BF16)] + [((width // LANES, tq, LANES), F32)] * 3
    return pl.pallas_call(
        _attn_kernel,
        grid=(batch, ng, nq),
        in_specs=[
            pl.BlockSpec((tq, width), lambda b, h, i: (b * nq + i, h)),
            pl.BlockSpec((tq, width), lambda b, h, i: (b * nq + i, h)),
            pl.BlockSpec((seq, width), lambda b, h, i: (b, h + ng)),
            pl.BlockSpec((seq, width), lambda b, h, i: (b, h)),
            pl.BlockSpec((seq, width), lambda b, h, i: (b, h + 2 * ng)),
            pl.BlockSpec((tq, width), lambda b, h, i: (b * nq + i, h + 3 * ng)),
        ],
        out_specs=pl.BlockSpec((tq, width), lambda b, h, i: (b * nq + i, h)),
        out_shape=jax.ShapeDtypeStruct((T, n_heads * LANES), BF16),
        scratch_shapes=[pltpu.VMEM(sh, dt) for sh, dt in scratch],
        compiler_params=_params(("parallel", "parallel", "arbitrary"),
                                _vmem_limit(blocks, scratch, [((tq, 2 * tq), F32)] * 2 * ATTN_HEADS_PER_STEP)),
        name="fox_attention",
    )(qkvg, qa, qkvg, ka, qkvg, qkvg)


def _mixer_ab(x2d, gain, w_in, j, sp_w, sp_b, v_gain, gamma, o_gain, w_out, layer, batch, seq):
    a_width = v_gain.shape[1]
    b_width = gamma.shape[1]
    y_a = _gmlp_proj(x2d, gain, w_in, j, v_gain, sp_w, sp_b, seq)
    qfig = _norm_matmul(x2d, gain, w_in, j, 2 * a_width, 4 * b_width, seq, F32, "hgrn_in_proj")
    y_b = _hgrn(qfig, gamma, o_gain, layer, batch, seq)
    return _matmul_residual([y_a, y_b], w_out, j, x2d, seq, "ab_out_proj", tn_pref=1024)


def _mixer_c(x2d, gain, w_in, w_forget, j, b_f, q_gain, k_gain, w_out, batch, seq):
    D = x2d.shape[1]
    n_heads = b_f.shape[1]
    head_dim = q_gain.shape[1]
    assert head_dim == LANES and n_heads * head_dim == D
    scale = head_dim ** -0.5 * LOG2E
    head_gain = jnp.concatenate([jnp.tile(q_gain * scale, (1, n_heads)), jnp.tile(k_gain, (1, n_heads)),
                                 jnp.ones((1, 2 * D), F32)], axis=1)
    qkvg = _qkvg_proj(x2d, gain, w_in, j, head_gain, 4 * D, 2 * D, seq)
    assert 3 * n_heads <= LANES
    pad = ((0, 0), (0, LANES - 3 * n_heads))
    w_f = jnp.pad(jnp.tile(w_forget, (1, 3)), pad).astype(BF16)
    b_fp = jnp.pad(jnp.tile(b_f, (1, 3)), pad)
    qa, ka = _fgate(x2d, gain, w_f, b_fp, n_heads, batch, seq)
    o = _attention(qkvg, qa, ka, n_heads, batch, seq)
    return _matmul_residual([o], w_out, j, x2d, seq, "attn_out_proj", tn_pref=1024)


def kernel(x, mix_norm, ab_w_in, ab_sp_w, ab_sp_b, ab_v_norm, hgrn_gamma, hgrn_o_norm, ab_w_out,
           c_w_in, c_b_f, c_q_norm, c_k_norm, c_w_out, ffn_norm, ffn_w_up, ffn_conv_w, ffn_conv_b,
           ffn_w_down):
    batch, seq, D = x.shape
    depth = mix_norm.shape[0]
    x2d = x.reshape(batch * seq, D)
    ab_w_in_b = ab_w_in.astype(BF16)
    c_w_in_b = c_w_in.astype(BF16)
    ffn_w_down = ffn_w_down.astype(BF16)
    for l in range(depth):
        j = l // 2
        gain = mix_norm[l][None, :]
        if l % 2 == 0:
            x2d = _mixer_ab(x2d, gain, ab_w_in_b, j, ab_sp_w[j], ab_sp_b[j], ab_v_norm[j][None, :],
                            hgrn_gamma, hgrn_o_norm[j][None, :], ab_w_out, l, batch, seq)
        else:
            x2d = _mixer_c(x2d, gain, c_w_in_b, c_w_in[j, :, 4 * D:], j, c_b_f[j][None, :],
                           c_q_norm[j][None, :], c_k_norm[j][None, :], c_w_out, batch, seq)
        x2d = _conv_ffn(x2d, ffn_norm[l][None, :], ffn_w_up, l, ffn_conv_w[l],
                        ffn_conv_b[l][None, :], ffn_w_down, seq)
    return x2d.reshape(batch, seq, D)
```

```python
import functools
import math

import jax
import jax.numpy as jnp
import numpy as np
from jax import lax
from jax.experimental import pallas as pl
from jax.experimental.pallas import tpu as pltpu

F32 = jnp.float32
BF16 = jnp.bfloat16
RMS_EPS = 1e-6
LANES = 128
SUBLANES = 8
MXU_COLS = 256
HGRN_CHUNK = 64
HGRN_MAX_FACTORED_RANGE = 60.0
OUT_PROJ_ROWS = 2048
ATTN_BLOCK = 512
ATTN_HEADS_PER_STEP = 4
LOG2E = 1.4426950408889634
V7X_VMEM_BYTES = 64 * 1024 * 1024
VMEM_CAP_BYTES = V7X_VMEM_BYTES - 6 * 1024 * 1024


def _nbytes(shape, dtype):
    return math.prod(shape) * jnp.dtype(dtype).itemsize


def _vmem_limit(blocks, scratch=(), temps=()):
    est = 2 * sum(_nbytes(s, d) for s, d in blocks)
    est += sum(_nbytes(s, d) for s, d in scratch) + sum(_nbytes(s, d) for s, d in temps)
    return int(min(VMEM_CAP_BYTES, est * 5 // 4 + (4 << 20)))


def _tile(n, pref, mult=LANES):
    if n <= pref:
        return n
    t = (pref // mult) * mult
    while n % t:
        t -= mult
    return t


def _params(sem, limit):
    return pltpu.CompilerParams(dimension_semantics=sem, vmem_limit_bytes=limit)


def _rms_rows(xf, gain):
    ms = jnp.mean(xf * xf, axis=-1, keepdims=True)
    return xf * lax.rsqrt(ms + RMS_EPS) * gain


def _gelu(x):
    return 0.5 * x * (1.0 + lax.erf(x * (2.0 ** -0.5)))


def _fill_normed(x_ref, gain_ref, h_ref, rows=256):
    rows = min(rows, x_ref.shape[0])

    def body(r, carry):
        sl = pl.ds(pl.multiple_of(r * rows, rows), rows)
        h_ref[sl, :] = _rms_rows(x_ref[sl, :], gain_ref[...]).astype(h_ref.dtype)
        return carry

    lax.fori_loop(0, x_ref.shape[0] // rows, body, 0)


def _tril_mask(n):
    t = lax.broadcasted_iota(jnp.int32, (n, n), 0)
    s = lax.broadcasted_iota(jnp.int32, (n, n), 1)
    return s <= t


def _cumsum_rows(x, tril_bf16):
    hi = x.astype(BF16)
    r1 = x - hi.astype(F32)
    mid = r1.astype(BF16)
    lo = (r1 - mid.astype(F32)).astype(BF16)
    y = jnp.dot(tril_bf16, jnp.concatenate([hi, mid, lo], axis=1), preferred_element_type=F32)
    d = x.shape[1]
    return y[:, :d] + y[:, d:2 * d] + y[:, 2 * d:]


def _dot_nt(a, b):
    return lax.dot_general(a, b, (((1,), (1,)), ((), ())), preferred_element_type=F32)


def _dot_tn(a, b):
    return lax.dot_general(a, b, (((0,), (0,)), ((), ())), preferred_element_type=F32)


def _gmlp_kernel(x_ref, gain_ref, wu_ref, wv_ref, vg_ref, spw_ref, spb_ref, o_ref, h_ref, *, chunk):
    @pl.when(pl.program_id(1) == 0)
    def _():
        _fill_normed(x_ref, gain_ref, h_ref)

    h = h_ref[...]
    u = _gelu(jnp.dot(h, wu_ref[...].astype(BF16), preferred_element_type=F32))
    v = _gelu(jnp.dot(h, wv_ref[...].astype(BF16), preferred_element_type=F32))
    tm, tn = u.shape
    n_chunks = tm // chunk
    tril = _tril_mask(chunk)
    for hh in range(tn // LANES):
        lanes = slice(hh * LANES, (hh + 1) * LANES)
        vh = _rms_rows(v[:, lanes], vg_ref[:, lanes]).astype(BF16)
        vcat = jnp.concatenate([vh[c * chunk:(c + 1) * chunk, :] for c in range(n_chunks)], axis=1)
        w_causal = jnp.where(tril, spw_ref[hh], 0.0).astype(BF16)
        mixed = jnp.dot(w_causal, vcat, preferred_element_type=F32)
        for c in range(n_chunks):
            rows = slice(c * chunk, (c + 1) * chunk)
            m_c = mixed[:, c * LANES:(c + 1) * LANES] + spb_ref[hh]
            o_ref[rows, lanes] = (u[rows, lanes] * m_c).astype(o_ref.dtype)


def _gmlp_proj(x2d, gain, w_in, layer, v_gain, sp_w, sp_b, seq):
    T, D = x2d.shape
    n_heads, chunk, _ = sp_w.shape
    a_width = n_heads * LANES
    tm = _tile(seq, 1024, chunk)
    tn = _tile(a_width, 512)
    nj = a_width // tn
    spb = jnp.broadcast_to(sp_b[:, :, None], (n_heads, chunk, LANES))
    blocks = [((tm, D), F32), ((1, D), F32), ((D, tn), w_in.dtype), ((D, tn), w_in.dtype), ((1, tn), F32),
              ((tn // LANES, chunk, chunk), F32), ((tn // LANES, chunk, LANES), F32), ((tm, tn), BF16)]
    return pl.pallas_call(
        functools.partial(_gmlp_kernel, chunk=chunk),
        grid=(T // tm, nj),
        in_specs=[
            pl.BlockSpec((tm, D), lambda i, j: (i, 0)),
            pl.BlockSpec((1, D), lambda i, j: (0, 0)),
            pl.BlockSpec((None, D, tn), lambda i, j: (layer, 0, j)),
            pl.BlockSpec((None, D, tn), lambda i, j: (layer, 0, j + nj)),
            pl.BlockSpec((1, tn), lambda i, j: (0, j)),
            pl.BlockSpec((tn // LANES, chunk, chunk), lambda i, j: (j, 0, 0)),
            pl.BlockSpec((tn // LANES, chunk, LANES), lambda i, j: (j, 0, 0)),
        ],
        out_specs=pl.BlockSpec((tm, tn), lambda i, j: (i, j)),
        out_shape=jax.ShapeDtypeStruct((T, a_width), BF16),
        scratch_shapes=[pltpu.VMEM((tm, D), BF16)],
        compiler_params=_params(("parallel", "arbitrary"),
                                _vmem_limit(blocks, [((tm, D), BF16)], [((tm, tn), F32)] * 6)),
        name="gmlp_proj",
    )(x2d, gain, w_in, w_in, v_gain, sp_w, spb)


def _norm_mm_kernel(x_ref, gain_ref, w_ref, o_ref, h_ref):
    @pl.when(pl.program_id(1) == 0)
    def _():
        _fill_normed(x_ref, gain_ref, h_ref)

    o_ref[...] = jnp.dot(h_ref[...], w_ref[...].astype(BF16), preferred_element_type=F32).astype(o_ref.dtype)


def _norm_matmul(x2d, gain, w, layer, col0, n_cols, seq, out_dtype, name):
    T, D = x2d.shape
    tm = _tile(seq, 1024)
    tn = _tile(n_cols, 1024)
    assert col0 % tn == 0
    j0 = col0 // tn
    blocks = [((tm, D), F32), ((1, D), F32), ((D, tn), w.dtype), ((tm, tn), out_dtype)]
    return pl.pallas_call(
        _norm_mm_kernel,
        grid=(T // tm, n_cols // tn),
        in_specs=[
            pl.BlockSpec((tm, D), lambda i, j: (i, 0)),
            pl.BlockSpec((1, D), lambda i, j: (0, 0)),
            pl.BlockSpec((None, D, tn), lambda i, j: (layer, 0, j + j0)),
        ],
        out_specs=pl.BlockSpec((tm, tn), lambda i, j: (i, j)),
        out_shape=jax.ShapeDtypeStruct((T, n_cols), out_dtype),
        scratch_shapes=[pltpu.VMEM((tm, D), BF16)],
        compiler_params=_params(("parallel", "arbitrary"),
                                _vmem_limit(blocks, [((tm, D), BF16)], [((tm, tn), F32)])),
        name=name,
    )(x2d, gain, w)


def _hgrn_kernel(gamma_ref, q_ref, f_ref, i_ref, g_ref, og_ref, o_ref, st_ref, g_scr, k_scr, intra_scr, inter_scr,
                 *, layer, chunk):
    @pl.when(pl.program_id(2) == 0)
    def _():
        st_ref[...] = jnp.zeros_like(st_ref)

    gam = gamma_ref[...]
    ex = jnp.exp(gam - jnp.max(gam, axis=0, keepdims=True))
    lb_all = jnp.sum(ex[:layer + 1], axis=0, keepdims=True) / jnp.sum(ex, axis=0, keepdims=True)

    L, width = q_ref.shape
    n_chunks = L // chunk
    mid = chunk // 2 - 1
    row_chunk = lax.broadcasted_iota(jnp.int32, (L, L), 0) // chunk
    col_chunk = lax.broadcasted_iota(jnp.int32, (L, L), 1) // chunk
    mask = _tril_mask(L) & (row_chunk == col_chunk)
    chunk_of_row = lax.broadcasted_iota(jnp.int32, (L, LANES), 0) // chunk

    fl = f_ref[...]
    e = jnp.exp(-jnp.abs(fl))
    r = 1.0 / (1.0 + e)
    pos = fl >= 0
    sig = jnp.where(pos, r, e * r)
    nsig = jnp.where(pos, e * r, r)
    kk = (1.0 - lb_all) * nsig
    G = _cumsum_rows(jnp.log(lb_all + (1.0 - lb_all) * sig), mask.astype(BF16))
    g_scr[...] = G
    k_scr[...] = kk
    g_mid_rows = [G[c * chunk + mid:c * chunk + mid + 1, :] for c in range(n_chunks)]
    g_end_rows = [G[(c + 1) * chunk - 1:(c + 1) * chunk, :] for c in range(n_chunks)]
    per_row = lambda rows: jnp.concatenate([jnp.broadcast_to(x, (chunk, width)) for x in rows], axis=0)
    g_mid = per_row(g_mid_rows)
    g_end = per_row(g_end_rows)
    qv = q_ref[...]
    q_t = (qv * jnp.exp(G - g_mid)).astype(BF16)
    k_t = (kk * jnp.exp(g_mid - G)).astype(BF16)
    q_g = (qv * jnp.exp(G)).astype(BF16)
    k_end = (kk * jnp.exp(g_end - G)).astype(BF16)
    vv = i_ref[...].astype(BF16)
    decay_range = -jnp.min(jnp.concatenate(g_end_rows, axis=0))
    factorable = decay_range <= HGRN_MAX_FACTORED_RANGE

    def finish(o, lanes):
        gv = g_ref[:, lanes]
        o_ref[:, lanes] = (_rms_rows(o, og_ref[...]) * (gv * jax.nn.sigmoid(gv))).astype(o_ref.dtype)

    for hh in range(width // LANES):
        lanes = slice(hh * LANES, (hh + 1) * LANES)
        scores = jnp.where(mask, _dot_nt(q_t[:, lanes], k_t[:, lanes]), 0.0).astype(BF16)
        intra = jnp.dot(scores, vv[:, lanes], preferred_element_type=F32)
        v_exp = jnp.concatenate([jnp.where(chunk_of_row == c, vv[:, lanes], 0) for c in range(n_chunks)],
                                axis=1)
        u_t = _dot_tn(v_exp, k_end[:, lanes])
        st = st_ref[hh]
        prev = []
        for c in range(n_chunks):
            prev.append(st.astype(BF16))
            st = st * jnp.exp(g_end_rows[c][:, lanes]) + u_t[c * LANES:(c + 1) * LANES, :]
        st_ref[hh] = st
        q_exp = jnp.concatenate([jnp.where(chunk_of_row == c, q_g[:, lanes], 0) for c in range(n_chunks)],
                                axis=1)
        inter = _dot_nt(q_exp, jnp.concatenate(prev, axis=1))
        inter_scr[hh] = inter
        finish(intra + inter, lanes)

    @pl.when(jnp.logical_not(factorable))
    def _():
        for hh in range(width // LANES):
            _hgrn_intra_pairwise(q_ref, i_ref, g_scr, k_scr, intra_scr, hh, chunk)
            finish(intra_scr[hh] + inter_scr[hh], slice(hh * LANES, (hh + 1) * LANES))


def _hgrn_intra_pairwise(q_ref, i_ref, g_scr, k_scr, intra_scr, hh, chunk):
    lanes = slice(hh * LANES, (hh + 1) * LANES)
    t_idx = lax.broadcasted_iota(jnp.int32, (chunk, 1), 0)
    for c in range(q_ref.shape[0] // chunk):
        rows = slice(c * chunk, (c + 1) * chunk)
        g_c = g_scr[rows, lanes]
        q_c = q_ref[rows, lanes]

        def body(grp, acc):
            keys = pl.ds(pl.multiple_of(c * chunk + grp * SUBLANES, SUBLANES), SUBLANES)
            g_s, k_s, v_s = g_scr[keys, lanes], k_scr[keys, lanes], i_ref[keys, lanes]
            for j in range(SUBLANES):
                w = q_c * k_s[j:j + 1] * jnp.exp(jnp.minimum(g_c - g_s[j:j + 1], 0.0))
                score = jnp.where(t_idx >= grp * SUBLANES + j, jnp.sum(w, axis=1, keepdims=True), 0.0)
                acc = acc + score * v_s[j:j + 1]
            return acc

        intra_scr[hh, rows, :] = lax.fori_loop(0, chunk // SUBLANES, body, jnp.zeros((chunk, LANES), F32))


def _hgrn(qfig, gamma, o_gain, layer, batch, seq):
    T = qfig.shape[0]
    b_width = gamma.shape[1]
    n_layers = gamma.shape[0]
    width = _tile(b_width, 512)
    L = _tile(seq, 256, HGRN_CHUNK)
    nw = b_width // width
    nl = seq // L
    blocks = [((L, width), F32)] * 4 + [((L, width), BF16), ((n_layers, width), F32)]
    n_sub = width // LANES
    scratch = [((n_sub, LANES, LANES), F32), ((L, width), F32), ((L, width), F32), ((n_sub, L, LANES), F32),
               ((n_sub, L, LANES), F32)]
    row = lambda b, h, l: b * nl + l
    return pl.pallas_call(
        functools.partial(_hgrn_kernel, layer=layer, chunk=HGRN_CHUNK),
        grid=(batch, nw, nl),
        in_specs=[
            pl.BlockSpec((n_layers, width), lambda b, h, l: (0, h)),
            pl.BlockSpec((L, width), lambda b, h, l: (row(b, h, l), h)),
            pl.BlockSpec((L, width), lambda b, h, l: (row(b, h, l), h + nw)),
            pl.BlockSpec((L, width), lambda b, h, l: (row(b, h, l), h + 2 * nw)),
            pl.BlockSpec((L, width), lambda b, h, l: (row(b, h, l), h + 3 * nw)),
            pl.BlockSpec((1, LANES), lambda b, h, l: (0, 0)),
        ],
        out_specs=pl.BlockSpec((L, width), lambda b, h, l: (row(b, h, l), h)),
        out_shape=jax.ShapeDtypeStruct((T, b_width), BF16),
        scratch_shapes=[pltpu.VMEM(sh, dt) for sh, dt in scratch],
        compiler_params=_params(("parallel", "parallel", "arbitrary"),
                                _vmem_limit(blocks, scratch, [((L, width), F32)] * 12)),
        name="hgrn2",
    )(gamma, qfig, qfig, qfig, qfig, o_gain)


def _mm_resid_kernel(*refs, n_lhs):
    lhs_refs = refs[:n_lhs]
    w_ref, r_ref, o_ref = refs[n_lhs:]
    acc = r_ref[...]
    k0 = 0
    for a_ref in lhs_refs:
        k = a_ref.shape[1]
        acc = acc + jnp.dot(a_ref[...], w_ref[k0:k0 + k, :].astype(BF16), preferred_element_type=F32)
        k0 += k
    o_ref[...] = acc


def _matmul_residual(lhs_list, w, layer, resid, seq, name, tm_pref=1024, tn_pref=512):
    T, N = resid.shape
    K = w.shape[1]
    tm = _tile(seq, tm_pref)
    tn = _tile(N, tn_pref)
    blocks = [((tm, a.shape[1]), BF16) for a in lhs_list] + [((K, tn), w.dtype), ((tm, tn), F32), ((tm, tn), F32)]
    return pl.pallas_call(
        functools.partial(_mm_resid_kernel, n_lhs=len(lhs_list)),
        grid=(T // tm, N // tn),
        in_specs=[pl.BlockSpec((tm, a.shape[1]), lambda i, j: (i, 0)) for a in lhs_list] + [
            pl.BlockSpec((None, K, tn), lambda i, j: (layer, 0, j)),
            pl.BlockSpec((tm, tn), lambda i, j: (i, j)),
        ],
        out_specs=pl.BlockSpec((tm, tn), lambda i, j: (i, j)),
        out_shape=jax.ShapeDtypeStruct((T, N), F32),
        compiler_params=_params(("parallel", "arbitrary"), _vmem_limit(blocks, [], [((tm, tn), F32)])),
        name=name,
    )(*lhs_list, w, resid)


def _ffn_up_kernel(x_ref, xh_ref, gain_ref, wa_ref, wb_ref, cwa_ref, cwb_ref, cba_ref, cbb_ref,
                   o_ref, h_ref, za_ref, *, n_tiles, n_col_tiles, tiles_per_seq, halo):
    s = pl.program_id(0)
    cur = jnp.minimum(s, n_tiles - 1)
    prev = jnp.maximum(s - 1, 0)
    row_c = cur // n_col_tiles
    tm = x_ref.shape[0]

    @pl.when(s == 0)
    def _():
        za_ref[...] = jnp.zeros_like(za_ref)

    @pl.when((cur % n_col_tiles == 0) & (s < n_tiles))
    def _():
        slot = row_c % 2
        keep = (row_c % tiles_per_seq != 0).astype(F32)
        h_ref[slot, 0:halo, :] = (_rms_rows(xh_ref[...], gain_ref[...]) * keep).astype(h_ref.dtype)
        rows = min(256, tm)

        def body(r, carry):
            src = pl.ds(pl.multiple_of(r * rows, rows), rows)
            dst = pl.ds(pl.multiple_of(r * rows + halo, halo), rows)
            h_ref[slot, dst, :] = _rms_rows(x_ref[src, :], gain_ref[...]).astype(h_ref.dtype)
            return carry

        lax.fori_loop(0, tm // rows, body, 0)

    def conv(z, cw_ref, cb_ref):
        z1 = pltpu.roll(z, 1, 0)
        z2 = pltpu.roll(z, 2, 0)
        y = cw_ref[0:1, :] * z2 + cw_ref[1:2, :] * z1 + cw_ref[2:3, :] * z + cb_ref[...]
        return y[halo:, :]

    h_prev = h_ref[(prev // n_col_tiles) % 2]
    zb = jnp.dot(h_prev, wb_ref[...].astype(BF16), preferred_element_type=F32)
    a = conv(za_ref[...], cwa_ref, cba_ref)
    o_ref[...] = (a * jax.nn.sigmoid(a) * conv(zb, cwb_ref, cbb_ref)).astype(o_ref.dtype)
    za_ref[...] = jnp.dot(h_ref[row_c % 2], wa_ref[...].astype(BF16), preferred_element_type=F32)


def _ffn_up(x2d, gain, w_up, layer, conv_w, conv_b, seq):
    T, D = x2d.shape
    F = w_up.shape[2] // 2
    halo = 16
    tm = _tile(seq, 1024)
    tn = _tile(F, 512, MXU_COLS)
    nj = F // tn
    n_tiles = (T // tm) * nj
    tiles_per_seq = seq // tm
    blocks = [((tm, D), F32), ((halo, D), F32), ((1, D), F32), ((D, tn), F32), ((D, tn), F32),
              ((3, tn), F32), ((3, tn), F32), ((1, tn), F32), ((1, tn), F32), ((tm, tn), BF16)]
    scratch = [((2, tm + halo, D), BF16), ((tm + halo, tn), F32)]
    hpt = tm // halo
    cur = lambda s: jnp.minimum(s, n_tiles - 1)
    prev = lambda s: jnp.maximum(s - 1, 0)
    return pl.pallas_call(
        functools.partial(_ffn_up_kernel, n_tiles=n_tiles, n_col_tiles=nj, tiles_per_seq=tiles_per_seq,
                          halo=halo),
        grid=(n_tiles + 1,),
        in_specs=[
            pl.BlockSpec((tm, D), lambda s: (cur(s) // nj, 0)),
            pl.BlockSpec((halo, D), lambda s: (jnp.maximum(cur(s) // nj * hpt - 1, 0), 0)),
            pl.BlockSpec((1, D), lambda s: (0, 0)),
            pl.BlockSpec((None, D, tn), lambda s: (layer, 0, cur(s) % nj)),
            pl.BlockSpec((None, D, tn), lambda s: (layer, 0, prev(s) % nj + nj)),
            pl.BlockSpec((3, tn), lambda s: (0, prev(s) % nj)),
            pl.BlockSpec((3, tn), lambda s: (0, prev(s) % nj + nj)),
            pl.BlockSpec((1, tn), lambda s: (0, prev(s) % nj)),
            pl.BlockSpec((1, tn), lambda s: (0, prev(s) % nj + nj)),
        ],
        out_specs=pl.BlockSpec((tm, tn), lambda s: (prev(s) // nj, prev(s) % nj)),
        out_shape=jax.ShapeDtypeStruct((T, F), BF16),
        scratch_shapes=[pltpu.VMEM(sh, dt) for sh, dt in scratch],
        compiler_params=_params(("arbitrary",),
                                _vmem_limit(blocks, scratch, [((tm + halo, MXU_COLS), F32)] * 8)),
        name="ffn_up_conv_gate",
    )(x2d, x2d, gain, w_up, w_up, conv_w, conv_w, conv_b, conv_b)


def _conv_ffn(x2d, gain, w_up, layer, conv_w, conv_b, w_down, seq):
    act = _ffn_up(x2d, gain, w_up, layer, conv_w, conv_b, seq)
    return _matmul_residual([act], w_down, layer, x2d, seq, "ffn_down", tn_pref=512)


def _qkvg_kernel(x_ref, gain_ref, w_ref, hg_ref, o_ref, h_ref, *, n_norm_tiles):
    j = pl.program_id(1)
    n_sub = o_ref.shape[1] // MXU_COLS

    @pl.when(j == 0)
    def _():
        _fill_normed(x_ref, gain_ref, h_ref)

    @pl.when(j < n_norm_tiles)
    def _():
        h = h_ref[...]
        for c in range(n_sub):
            y = _dot_nt(h, w_ref[c * MXU_COLS:(c + 1) * MXU_COLS, :])
            for grp in range(MXU_COLS // LANES):
                src = slice(grp * LANES, (grp + 1) * LANES)
                dst = slice(c * MXU_COLS + grp * LANES, c * MXU_COLS + (grp + 1) * LANES)
                o_ref[:, dst] = _rms_rows(y[:, src], hg_ref[:, dst]).astype(o_ref.dtype)

    @pl.when(j >= n_norm_tiles)
    def _():
        h = h_ref[...]
        for c in range(n_sub):
            cols = slice(c * MXU_COLS, (c + 1) * MXU_COLS)
            o_ref[:, cols] = _dot_nt(h, w_ref[cols, :]).astype(o_ref.dtype)


def _qkvg_proj(x2d, gain, w, layer, head_gain, n_cols, n_norm_cols, seq):
    T, D = x2d.shape
    tm = _tile(seq, 1024)
    tn = _tile(n_norm_cols // 2, 2048, MXU_COLS)
    blocks = [((tm, D), F32), ((1, D), F32), ((D, tn), w.dtype), ((1, tn), F32), ((tm, tn), BF16)]
    return pl.pallas_call(
        functools.partial(_qkvg_kernel, n_norm_tiles=n_norm_cols // tn),
        grid=(T // tm, n_cols // tn),
        in_specs=[
            pl.BlockSpec((tm, D), lambda i, j: (i, 0)),
            pl.BlockSpec((1, D), lambda i, j: (0, 0)),
            pl.BlockSpec((None, tn, D), lambda i, j: (layer, j, 0)),
            pl.BlockSpec((1, tn), lambda i, j: (0, j)),
        ],
        out_specs=pl.BlockSpec((tm, tn), lambda i, j: (i, j)),
        out_shape=jax.ShapeDtypeStruct((T, n_cols), BF16),
        scratch_shapes=[pltpu.VMEM((tm, D), BF16)],
        compiler_params=_params(("parallel", "arbitrary"),
                                _vmem_limit(blocks, [((tm, D), BF16)], [((tm, MXU_COLS), F32)] * 4)),
        name="qkvg_proj",
    )(x2d, gain, w, head_gain)


def _fgate_kernel(x_ref, gain_ref, wf_ref, bf_ref, pq_ref, pk_ref, cq_ref, ck_ref, qa_ref, ka_ref,
                  carry_ref, *, n_heads):
    @pl.when(pl.program_id(1) == 0)
    def _():
        carry_ref[...] = jnp.zeros_like(carry_ref)

    tm = x_ref.shape[0]
    h = _rms_rows(x_ref[...], gain_ref[...]).astype(BF16)
    f = jnp.dot(h, wf_ref[...], preferred_element_type=F32) + bf_ref[...]
    log_f = (jnp.minimum(f, 0.0) - jnp.log1p(jnp.exp(-jnp.abs(f)))) * LOG2E
    c = _cumsum_rows(log_f, _tril_mask(tm).astype(BF16)) + carry_ref[...]
    carry_ref[...] = c[tm - 1:tm, :]
    hi = c.astype(BF16)
    r1 = c - hi.astype(F32)
    mid = r1.astype(BF16)
    lo = (r1 - mid.astype(F32)).astype(BF16)
    lane = lax.broadcasted_iota(jnp.int32, c.shape, 1)
    parts = jnp.where(lane < n_heads, hi, jnp.where(lane < 2 * n_heads, mid, lo))
    qa_ref[...] = (jnp.dot(parts, pq_ref[...], preferred_element_type=F32) + cq_ref[...]).astype(qa_ref.dtype)
    ka_ref[...] = (jnp.dot(parts, pk_ref[...], preferred_element_type=F32) + ck_ref[...]).astype(ka_ref.dtype)


def _fgate_selectors(n_heads):
    W = n_heads * LANES
    pq = np.zeros((LANES, W), np.float32)
    pk = np.zeros((LANES, W), np.float32)
    cq = np.zeros((1, W), np.float32)
    ck = np.zeros((1, W), np.float32)
    for hd in range(n_heads):
        for part in range(3):
            pq[part * n_heads + hd, hd * LANES + part] = 1.0
            pk[part * n_heads + hd, hd * LANES + 3 + part] = -1.0
            cq[0, hd * LANES + 3 + part] = 1.0
            ck[0, hd * LANES + part] = 1.0
    return jnp.asarray(pq, BF16), jnp.asarray(pk, BF16), jnp.asarray(cq), jnp.asarray(ck)


def _fgate(x2d, gain, w_f, b_f, n_heads, batch, seq):
    T, D = x2d.shape
    tm = _tile(seq, 512)
    nt = seq // tm
    W = n_heads * LANES
    pq, pk, cq, ck = _fgate_selectors(n_heads)
    blocks = [((tm, D), F32), ((1, D), F32), ((D, LANES), BF16), ((1, LANES), F32),
              ((LANES, W), BF16), ((LANES, W), BF16), ((1, W), F32), ((1, W), F32),
              ((tm, W), BF16), ((tm, W), BF16)]
    const = lambda b, i: (0, 0)
    return pl.pallas_call(
        functools.partial(_fgate_kernel, n_heads=n_heads),
        grid=(batch, nt),
        in_specs=[
            pl.BlockSpec((tm, D), lambda b, i: (b * nt + i, 0)),
            pl.BlockSpec((1, D), const),
            pl.BlockSpec((D, LANES), const),
            pl.BlockSpec((1, LANES), const),
            pl.BlockSpec((LANES, W), const),
            pl.BlockSpec((LANES, W), const),
            pl.BlockSpec((1, W), const),
            pl.BlockSpec((1, W), const),
        ],
        out_specs=[pl.BlockSpec((tm, W), lambda b, i: (b * nt + i, 0)),
                   pl.BlockSpec((tm, W), lambda b, i: (b * nt + i, 0))],
        out_shape=[jax.ShapeDtypeStruct((T, W), BF16), jax.ShapeDtypeStruct((T, W), BF16)],
        scratch_shapes=[pltpu.VMEM((1, LANES), F32)],
        compiler_params=_params(("parallel", "arbitrary"),
                                _vmem_limit(blocks, [], [((tm, D), F32)] * 3 + [((tm, tm), BF16)])),
        name="fox_forget_cumsum",
    )(x2d, gain, w_f, b_f, pq, pk, cq, ck)


def _attn_kernel(q_ref, qa_ref, k_ref, ka_ref, v_ref, g_ref, o_ref, qs_ref, acc_ref, m_ref, l_ref):
    qi = pl.program_id(2)
    tq = tk = q_ref.shape[0]
    n_sub = q_ref.shape[1] // LANES
    for hh in range(n_sub):
        lanes = slice(hh * LANES, (hh + 1) * LANES)
        qs_ref[hh] = jnp.concatenate([q_ref[:, lanes], qa_ref[:, lanes]], axis=1)
    acc_ref[...] = jnp.zeros_like(acc_ref)
    m_ref[...] = jnp.full(m_ref.shape, -jnp.inf, F32)
    l_ref[...] = jnp.zeros_like(l_ref)

    def step(first_key, n_keys, diag_offset=None):
        rows = pl.ds(pl.multiple_of(first_key, tk), n_keys)
        for hh in range(n_sub):
            lanes = slice(hh * LANES, (hh + 1) * LANES)
            k = jnp.concatenate([k_ref[rows, lanes], ka_ref[rows, lanes]], axis=1)
            s = _dot_nt(qs_ref[hh], k)
            if diag_offset is not None:
                t_pos = lax.broadcasted_iota(jnp.int32, (tq, n_keys), 0)
                s_pos = lax.broadcasted_iota(jnp.int32, (tq, n_keys), 1) + diag_offset
                s = jnp.where(s_pos <= t_pos, s, -jnp.inf)
            m_prev = m_ref[hh]
            m_new = jnp.maximum(m_prev, jnp.max(s, axis=1, keepdims=True))
            alpha = jnp.exp2(m_prev - m_new)
            p = jnp.exp2(s - jnp.tile(m_new, (1, n_keys // LANES)))
            m_ref[hh] = m_new
            l_ref[hh] = alpha * l_ref[hh] + jnp.sum(p, axis=1, keepdims=True)
            acc_ref[hh] = alpha * acc_ref[hh] + jnp.dot(p.astype(BF16), v_ref[rows, lanes],
                                                        preferred_element_type=F32)

    def loop(n, body):
        lax.fori_loop(0, n, lambda j, c: (body(j), c)[1], 0)

    loop(qi // 2, lambda j: step(j * 2 * tk, 2 * tk))

    @pl.when(qi % 2 == 1)
    def _():
        step((qi - 1) * tk, 2 * tk, -tk)

    @pl.when(qi % 2 == 0)
    def _():
        step(qi * tk, tk, 0)
    for hh in range(n_sub):
        lanes = slice(hh * LANES, (hh + 1) * LANES)
        gate = jax.nn.sigmoid(g_ref[:, lanes].astype(F32))
        o_ref[:, lanes] = (acc_ref[hh] / l_ref[hh] * gate).astype(o_ref.dtype)


def _attention(qkvg, qa, ka, n_heads, batch, seq):
    T = qkvg.shape[0]
    tq = _tile(seq, ATTN_BLOCK)
    nq = seq // tq
    width = ATTN_HEADS_PER_STEP * LANES
    ng = n_heads * LANES // width
    blocks = [((tq, width), BF16)] * 2 + [((seq, width), BF16)] * 3 + [((tq, width), BF16)] * 2
    scratch = [((width // LANES, tq, 2 * LANES), BF16)] + [((width // LANES, tq, LANES), F32)] * 3
    return pl.pallas_call(
        _attn_kernel,
        grid=(batch, ng, nq),
        in_specs=[
            pl.BlockSpec((tq, width), lambda b, h, i: (b * nq + i, h)),
            pl.BlockSpec((tq, width), lambda b, h, i: (b * nq + i, h)),
            pl.BlockSpec((seq, width), lambda b, h, i: (b, h + ng)),
            pl.BlockSpec((seq, width), lambda b, h, i: (b, h)),
            pl.BlockSpec((seq, width), lambda b, h, i: (b, h + 2 * ng)),
            pl.BlockSpec((tq, width), lambda b, h, i: (b * nq + i, h + 3 * ng)),
        ],
        out_specs=pl.BlockSpec((tq, width), lambda b, h, i: (b * nq + i, h)),
        out_shape=jax.ShapeDtypeStruct((T, n_heads * LANES), BF16),
        scratch_shapes=[pltpu.VMEM(sh, dt) for sh, dt in scratch],
        compiler_params=_params(("parallel", "parallel", "arbitrary"),
                                _vmem_limit(blocks, scratch, [((tq, 2 * tq), F32)] * 2 * ATTN_HEADS_PER_STEP)),
        name="fox_attention",
    )(qkvg, qa, qkvg, ka, qkvg, qkvg)


def _mixer_ab(x2d, gain, w_in, j, sp_w, sp_b, v_gain, gamma, o_gain, w_out, layer, batch, seq):
    a_width = v_gain.shape[1]
    b_width = gamma.shape[1]
    y_a = _gmlp_proj(x2d, gain, w_in, j, v_gain, sp_w, sp_b, seq)
    qfig = _norm_matmul(x2d, gain, w_in, j, 2 * a_width, 4 * b_width, seq, F32, "hgrn_in_proj")
    y_b = _hgrn(qfig, gamma, o_gain, layer, batch, seq)
    return _matmul_residual([y_a, y_b], w_out, j, x2d, seq, "ab_out_proj", tm_pref=OUT_PROJ_ROWS)


def _mixer_c(x2d, gain, w_in, w_forget, j, b_f, q_gain, k_gain, w_out, batch, seq):
    D = x2d.shape[1]
    n_heads = b_f.shape[1]
    head_dim = q_gain.shape[1]
    assert head_dim == LANES and n_heads * head_dim == D
    scale = head_dim ** -0.5 * LOG2E
    head_gain = jnp.concatenate([jnp.tile(q_gain * scale, (1, n_heads)), jnp.tile(k_gain, (1, n_heads)),
                                 jnp.ones((1, 2 * D), F32)], axis=1)
    qkvg = _qkvg_proj(x2d, gain, w_in, j, head_gain, 4 * D, 2 * D, seq)
    assert 3 * n_heads <= LANES
    pad = ((0, 0), (0, LANES - 3 * n_heads))
    w_f = jnp.pad(jnp.tile(w_forget, (1, 3)), pad).astype(BF16)
    b_fp = jnp.pad(jnp.tile(b_f, (1, 3)), pad)
    qa, ka = _fgate(x2d, gain, w_f, b_fp, n_heads, batch, seq)
    o = _attention(qkvg, qa, ka, n_heads, batch, seq)
    return _matmul_residual([o], w_out, j, x2d, seq, "attn_out_proj", tm_pref=OUT_PROJ_ROWS)


def kernel(x, mix_norm, ab_w_in, ab_sp_w, ab_sp_b, ab_v_norm, hgrn_gamma, hgrn_o_norm, ab_w_out,
           c_w_in, c_b_f, c_q_norm, c_k_norm, c_w_out, ffn_norm, ffn_w_up, ffn_conv_w, ffn_conv_b,
           ffn_w_down):
    batch, seq, D = x.shape
    depth = mix_norm.shape[0]
    x2d = x.reshape(batch * seq, D)
    ab_w_in_b = ab_w_in.astype(BF16)
    c_w_in_t = jnp.swapaxes(c_w_in, 1, 2).astype(BF16)
    ffn_w_down = ffn_w_down.astype(BF16)
    ab_w_out = ab_w_out.astype(BF16)
    c_w_out = c_w_out.astype(BF16)
    for l in range(depth):
        j = l // 2
        gain = mix_norm[l][None, :]
        if l % 2 == 0:
            x2d = _mixer_ab(x2d, gain, ab_w_in_b, j, ab_sp_w[j], ab_sp_b[j], ab_v_norm[j][None, :],
                            hgrn_gamma, hgrn_o_norm[j][None, :], ab_w_out, l, batch, seq)
        else:
            x2d = _mixer_c(x2d, gain, c_w_in_t, c_w_in[j, :, 4 * D:], j, c_b_f[j][None, :],
                           c_q_norm[j][None, :], c_k_norm[j][None, :], c_w_out, batch, seq)
        x2d = _conv_ffn(x2d, ffn_norm[l][None, :], ffn_w_up, l, ffn_conv_w[l],
                        ffn_conv_b[l][None, :], ffn_w_down, seq)
    return x2d.reshape(batch, seq, D)
```

```python
import functools
import math

import jax
import jax.numpy as jnp
import numpy as np
from jax import lax
from jax.experimental import pallas as pl
from jax.experimental.pallas import tpu as pltpu

F32 = jnp.float32
BF16 = jnp.bfloat16
RMS_EPS = 1e-6
LANES = 128
SUBLANES = 8
MXU_COLS = 256
HGRN_CHUNK = 64
HGRN_MAX_FACTORED_RANGE = 60.0
OUT_PROJ_ROWS = 2048
ATTN_BLOCK = 512
ATTN_HEADS_PER_STEP = 4
LOG2E = 1.4426950408889634
V7X_VMEM_BYTES = 64 * 1024 * 1024
VMEM_CAP_BYTES = V7X_VMEM_BYTES - 6 * 1024 * 1024


def _nbytes(shape, dtype):
    return math.prod(shape) * jnp.dtype(dtype).itemsize


def _vmem_limit(blocks, scratch=(), temps=()):
    est = 2 * sum(_nbytes(s, d) for s, d in blocks)
    est += sum(_nbytes(s, d) for s, d in scratch) + sum(_nbytes(s, d) for s, d in temps)
    return int(min(VMEM_CAP_BYTES, est * 5 // 4 + (4 << 20)))


def _tile(n, pref, mult=LANES):
    if n <= pref:
        return n
    t = (pref // mult) * mult
    while n % t:
        t -= mult
    return t


def _params(sem, limit):
    return pltpu.CompilerParams(dimension_semantics=sem, vmem_limit_bytes=limit)


def _rms_rows(xf, gain):
    ms = jnp.mean(xf * xf, axis=-1, keepdims=True)
    return xf * lax.rsqrt(ms + RMS_EPS) * gain


def _gelu(x):
    return 0.5 * x * (1.0 + lax.erf(x * (2.0 ** -0.5)))


def _fill_normed(x_ref, gain_ref, h_ref, rows=256):
    rows = min(rows, x_ref.shape[0])

    def body(r, carry):
        sl = pl.ds(pl.multiple_of(r * rows, rows), rows)
        h_ref[sl, :] = _rms_rows(x_ref[sl, :], gain_ref[...]).astype(h_ref.dtype)
        return carry

    lax.fori_loop(0, x_ref.shape[0] // rows, body, 0)


def _tril_mask(n):
    t = lax.broadcasted_iota(jnp.int32, (n, n), 0)
    s = lax.broadcasted_iota(jnp.int32, (n, n), 1)
    return s <= t


def _cumsum_rows(x, tril_bf16):
    hi = x.astype(BF16)
    r1 = x - hi.astype(F32)
    mid = r1.astype(BF16)
    lo = (r1 - mid.astype(F32)).astype(BF16)
    y = jnp.dot(tril_bf16, jnp.concatenate([hi, mid, lo], axis=1), preferred_element_type=F32)
    d = x.shape[1]
    return y[:, :d] + y[:, d:2 * d] + y[:, 2 * d:]


def _dot_nt(a, b):
    return lax.dot_general(a, b, (((1,), (1,)), ((), ())), preferred_element_type=F32)


def _dot_tn(a, b):
    return lax.dot_general(a, b, (((0,), (0,)), ((), ())), preferred_element_type=F32)


def _gmlp_kernel(x_ref, gain_ref, wu_ref, wv_ref, vg_ref, spw_ref, spb_ref, o_ref, h_ref, *, chunk):
    @pl.when(pl.program_id(1) == 0)
    def _():
        _fill_normed(x_ref, gain_ref, h_ref)

    h = h_ref[...]
    u = _gelu(jnp.dot(h, wu_ref[...].astype(BF16), preferred_element_type=F32))
    v = _gelu(jnp.dot(h, wv_ref[...].astype(BF16), preferred_element_type=F32))
    tm, tn = u.shape
    n_chunks = tm // chunk
    tril = _tril_mask(chunk)
    for hh in range(tn // LANES):
        lanes = slice(hh * LANES, (hh + 1) * LANES)
        vh = _rms_rows(v[:, lanes], vg_ref[:, lanes]).astype(BF16)
        vcat = jnp.concatenate([vh[c * chunk:(c + 1) * chunk, :] for c in range(n_chunks)], axis=1)
        w_causal = jnp.where(tril, spw_ref[hh], 0.0).astype(BF16)
        mixed = jnp.dot(w_causal, vcat, preferred_element_type=F32)
        for c in range(n_chunks):
            rows = slice(c * chunk, (c + 1) * chunk)
            m_c = mixed[:, c * LANES:(c + 1) * LANES] + spb_ref[hh]
            o_ref[rows, lanes] = (u[rows, lanes] * m_c).astype(o_ref.dtype)


def _gmlp_proj(x2d, gain, w_in, layer, v_gain, sp_w, sp_b, seq):
    T, D = x2d.shape
    n_heads, chunk, _ = sp_w.shape
    a_width = n_heads * LANES
    tm = _tile(seq, 1024, chunk)
    tn = _tile(a_width, 512)
    nj = a_width // tn
    spb = jnp.broadcast_to(sp_b[:, :, None], (n_heads, chunk, LANES))
    blocks = [((tm, D), F32), ((1, D), F32), ((D, tn), w_in.dtype), ((D, tn), w_in.dtype), ((1, tn), F32),
              ((tn // LANES, chunk, chunk), F32), ((tn // LANES, chunk, LANES), F32), ((tm, tn), BF16)]
    return pl.pallas_call(
        functools.partial(_gmlp_kernel, chunk=chunk),
        grid=(T // tm, nj),
        in_specs=[
            pl.BlockSpec((tm, D), lambda i, j: (i, 0)),
            pl.BlockSpec((1, D), lambda i, j: (0, 0)),
            pl.BlockSpec((None, D, tn), lambda i, j: (layer, 0, j)),
            pl.BlockSpec((None, D, tn), lambda i, j: (layer, 0, j + nj)),
            pl.BlockSpec((1, tn), lambda i, j: (0, j)),
            pl.BlockSpec((tn // LANES, chunk, chunk), lambda i, j: (j, 0, 0)),
            pl.BlockSpec((tn // LANES, chunk, LANES), lambda i, j: (j, 0, 0)),
        ],
        out_specs=[pl.BlockSpec((tm, tn), lambda i, j: (i, j)), pl.BlockSpec((tm, D), lambda i, j: (i, 0))],
        out_shape=[jax.ShapeDtypeStruct((T, a_width), BF16), jax.ShapeDtypeStruct((T, D), BF16)],
        compiler_params=_params(("parallel", "arbitrary"),
                                _vmem_limit(blocks + [((tm, D), BF16)], [], [((tm, tn), F32)] * 6)),
        name="gmlp_proj",
    )(x2d, gain, w_in, w_in, v_gain, sp_w, spb)


def _mm_kernel(h_ref, w_ref, o_ref):
    o_ref[...] = jnp.dot(h_ref[...], w_ref[...].astype(BF16), preferred_element_type=F32).astype(o_ref.dtype)


def _matmul(h, w, layer, col0, n_cols, seq, out_dtype, name):
    T, D = h.shape
    tm = _tile(seq, 1024)
    tn = _tile(n_cols, 2048)
    assert col0 % tn == 0
    j0 = col0 // tn
    blocks = [((tm, D), BF16), ((D, tn), w.dtype), ((tm, tn), out_dtype)]
    return pl.pallas_call(
        _mm_kernel,
        grid=(T // tm, n_cols // tn),
        in_specs=[
            pl.BlockSpec((tm, D), lambda i, j: (i, 0)),
            pl.BlockSpec((None, D, tn), lambda i, j: (layer, 0, j + j0)),
        ],
        out_specs=pl.BlockSpec((tm, tn), lambda i, j: (i, j)),
        out_shape=jax.ShapeDtypeStruct((T, n_cols), out_dtype),
        compiler_params=_params(("parallel", "arbitrary"), _vmem_limit(blocks, [], [((tm, tn), F32)])),
        name=name,
    )(h, w)


def _hgrn_kernel(gamma_ref, q_ref, f_ref, i_ref, g_ref, og_ref, o_ref, st_ref, g_scr, k_scr, intra_scr, inter_scr,
                 *, layer, chunk):
    @pl.when(pl.program_id(2) == 0)
    def _():
        st_ref[...] = jnp.zeros_like(st_ref)

    gam = gamma_ref[...]
    ex = jnp.exp(gam - jnp.max(gam, axis=0, keepdims=True))
    lb_all = jnp.sum(ex[:layer + 1], axis=0, keepdims=True) / jnp.sum(ex, axis=0, keepdims=True)

    L, width = q_ref.shape
    n_chunks = L // chunk
    mid = chunk // 2 - 1
    row_chunk = lax.broadcasted_iota(jnp.int32, (L, L), 0) // chunk
    col_chunk = lax.broadcasted_iota(jnp.int32, (L, L), 1) // chunk
    mask = _tril_mask(L) & (row_chunk == col_chunk)
    chunk_of_row = lax.broadcasted_iota(jnp.int32, (L, LANES), 0) // chunk

    fl = f_ref[...]
    e = jnp.exp(-jnp.abs(fl))
    r = 1.0 / (1.0 + e)
    pos = fl >= 0
    sig = jnp.where(pos, r, e * r)
    nsig = jnp.where(pos, e * r, r)
    kk = (1.0 - lb_all) * nsig
    G = _cumsum_rows(jnp.log(lb_all + (1.0 - lb_all) * sig), mask.astype(BF16))
    g_scr[...] = G
    k_scr[...] = kk
    g_mid_rows = [G[c * chunk + mid:c * chunk + mid + 1, :] for c in range(n_chunks)]
    g_end_rows = [G[(c + 1) * chunk - 1:(c + 1) * chunk, :] for c in range(n_chunks)]
    per_row = lambda rows: jnp.concatenate([jnp.broadcast_to(x, (chunk, width)) for x in rows], axis=0)
    g_mid = per_row(g_mid_rows)
    g_end = per_row(g_end_rows)
    qv = q_ref[...]
    q_t = (qv * jnp.exp(G - g_mid)).astype(BF16)
    k_t = (kk * jnp.exp(g_mid - G)).astype(BF16)
    q_g = (qv * jnp.exp(G)).astype(BF16)
    k_end = (kk * jnp.exp(g_end - G)).astype(BF16)
    vv = i_ref[...].astype(BF16)
    decay_range = -jnp.min(jnp.concatenate(g_end_rows, axis=0))
    factorable = decay_range <= HGRN_MAX_FACTORED_RANGE

    def finish(o, lanes):
        gv = g_ref[:, lanes]
        o_ref[:, lanes] = (_rms_rows(o, og_ref[...]) * (gv * jax.nn.sigmoid(gv))).astype(o_ref.dtype)

    for hh in range(width // LANES):
        lanes = slice(hh * LANES, (hh + 1) * LANES)
        scores = jnp.where(mask, _dot_nt(q_t[:, lanes], k_t[:, lanes]), 0.0).astype(BF16)
        intra = jnp.dot(scores, vv[:, lanes], preferred_element_type=F32)
        v_exp = jnp.concatenate([jnp.where(chunk_of_row == c, vv[:, lanes], 0) for c in range(n_chunks)],
                                axis=1)
        u_t = _dot_tn(v_exp, k_end[:, lanes])
        st = st_ref[hh]
        prev = []
        for c in range(n_chunks):
            prev.append(st.astype(BF16))
            st = st * jnp.exp(g_end_rows[c][:, lanes]) + u_t[c * LANES:(c + 1) * LANES, :]
        st_ref[hh] = st
        q_exp = jnp.concatenate([jnp.where(chunk_of_row == c, q_g[:, lanes], 0) for c in range(n_chunks)],
                                axis=1)
        inter = _dot_nt(q_exp, jnp.concatenate(prev, axis=1))
        inter_scr[hh] = inter
        finish(intra + inter, lanes)

    @pl.when(jnp.logical_not(factorable))
    def _():
        for hh in range(width // LANES):
            _hgrn_intra_pairwise(q_ref, i_ref, g_scr, k_scr, intra_scr, hh, chunk)
            finish(intra_scr[hh] + inter_scr[hh], slice(hh * LANES, (hh + 1) * LANES))


def _hgrn_intra_pairwise(q_ref, i_ref, g_scr, k_scr, intra_scr, hh, chunk):
    lanes = slice(hh * LANES, (hh + 1) * LANES)
    t_idx = lax.broadcasted_iota(jnp.int32, (chunk, 1), 0)
    for c in range(q_ref.shape[0] // chunk):
        rows = slice(c * chunk, (c + 1) * chunk)
        g_c = g_scr[rows, lanes]
        q_c = q_ref[rows, lanes]

        def body(grp, acc):
            keys = pl.ds(pl.multiple_of(c * chunk + grp * SUBLANES, SUBLANES), SUBLANES)
            g_s, k_s, v_s = g_scr[keys, lanes], k_scr[keys, lanes], i_ref[keys, lanes]
            for j in range(SUBLANES):
                w = q_c * k_s[j:j + 1] * jnp.exp(jnp.minimum(g_c - g_s[j:j + 1], 0.0))
                score = jnp.where(t_idx >= grp * SUBLANES + j, jnp.sum(w, axis=1, keepdims=True), 0.0)
                acc = acc + score * v_s[j:j + 1]
            return acc

        intra_scr[hh, rows, :] = lax.fori_loop(0, chunk // SUBLANES, body, jnp.zeros((chunk, LANES), F32))


def _hgrn(qfig, gamma, o_gain, layer, batch, seq):
    T = qfig.shape[0]
    b_width = gamma.shape[1]
    n_layers = gamma.shape[0]
    width = _tile(b_width, 512)
    L = _tile(seq, 256, HGRN_CHUNK)
    nw = b_width // width
    nl = seq // L
    blocks = [((L, width), F32)] * 4 + [((L, width), BF16), ((n_layers, width), F32)]
    n_sub = width // LANES
    scratch = [((n_sub, LANES, LANES), F32), ((L, width), F32), ((L, width), F32), ((n_sub, L, LANES), F32),
               ((n_sub, L, LANES), F32)]
    row = lambda b, h, l: b * nl + l
    return pl.pallas_call(
        functools.partial(_hgrn_kernel, layer=layer, chunk=HGRN_CHUNK),
        grid=(batch, nw, nl),
        in_specs=[
            pl.BlockSpec((n_layers, width), lambda b, h, l: (0, h)),
            pl.BlockSpec((L, width), lambda b, h, l: (row(b, h, l), h)),
            pl.BlockSpec((L, width), lambda b, h, l: (row(b, h, l), h + nw)),
            pl.BlockSpec((L, width), lambda b, h, l: (row(b, h, l), h + 2 * nw)),
            pl.BlockSpec((L, width), lambda b, h, l: (row(b, h, l), h + 3 * nw)),
            pl.BlockSpec((1, LANES), lambda b, h, l: (0, 0)),
        ],
        out_specs=pl.BlockSpec((L, width), lambda b, h, l: (row(b, h, l), h)),
        out_shape=jax.ShapeDtypeStruct((T, b_width), BF16),
        scratch_shapes=[pltpu.VMEM(sh, dt) for sh, dt in scratch],
        compiler_params=_params(("parallel", "parallel", "arbitrary"),
                                _vmem_limit(blocks, scratch, [((L, width), F32)] * 12)),
        name="hgrn2",
    )(gamma, qfig, qfig, qfig, qfig, o_gain)


def _mm_resid_kernel(*refs, n_lhs):
    lhs_refs = refs[:n_lhs]
    w_ref, r_ref, o_ref = refs[n_lhs:]
    acc = r_ref[...]
    k0 = 0
    for a_ref in lhs_refs:
        k = a_ref.shape[1]
        acc = acc + jnp.dot(a_ref[...], w_ref[k0:k0 + k, :].astype(BF16), preferred_element_type=F32)
        k0 += k
    o_ref[...] = acc


def _matmul_residual(lhs_list, w, layer, resid, seq, name, tm_pref=1024, tn_pref=512):
    T, N = resid.shape
    K = w.shape[1]
    tm = _tile(seq, tm_pref)
    tn = _tile(N, tn_pref)
    blocks = [((tm, a.shape[1]), BF16) for a in lhs_list] + [((K, tn), w.dtype), ((tm, tn), F32), ((tm, tn), F32)]
    return pl.pallas_call(
        functools.partial(_mm_resid_kernel, n_lhs=len(lhs_list)),
        grid=(T // tm, N // tn),
        in_specs=[pl.BlockSpec((tm, a.shape[1]), lambda i, j: (i, 0)) for a in lhs_list] + [
            pl.BlockSpec((None, K, tn), lambda i, j: (layer, 0, j)),
            pl.BlockSpec((tm, tn), lambda i, j: (i, j)),
        ],
        out_specs=pl.BlockSpec((tm, tn), lambda i, j: (i, j)),
        out_shape=jax.ShapeDtypeStruct((T, N), F32),
        compiler_params=_params(("parallel", "arbitrary"), _vmem_limit(blocks, [], [((tm, tn), F32)])),
        name=name,
    )(*lhs_list, w, resid)


def _ffn_up_kernel(x_ref, xh_ref, gain_ref, wa_ref, wb_ref, cwa_ref, cwb_ref, cba_ref, cbb_ref,
                   o_ref, h_ref, za_ref, *, n_tiles, n_col_tiles, tiles_per_seq, halo):
    s = pl.program_id(0)
    cur = jnp.minimum(s, n_tiles - 1)
    prev = jnp.maximum(s - 1, 0)
    row_c = cur // n_col_tiles
    tm = x_ref.shape[0]

    @pl.when(s == 0)
    def _():
        za_ref[...] = jnp.zeros_like(za_ref)

    @pl.when((cur % n_col_tiles == 0) & (s < n_tiles))
    def _():
        slot = row_c % 2
        keep = (row_c % tiles_per_seq != 0).astype(F32)
        h_ref[slot, 0:halo, :] = (_rms_rows(xh_ref[...], gain_ref[...]) * keep).astype(h_ref.dtype)
        rows = min(256, tm)

        def body(r, carry):
            src = pl.ds(pl.multiple_of(r * rows, rows), rows)
            dst = pl.ds(pl.multiple_of(r * rows + halo, halo), rows)
            h_ref[slot, dst, :] = _rms_rows(x_ref[src, :], gain_ref[...]).astype(h_ref.dtype)
            return carry

        lax.fori_loop(0, tm // rows, body, 0)

    def conv(z, cw_ref, cb_ref):
        z1 = pltpu.roll(z, 1, 0)
        z2 = pltpu.roll(z, 2, 0)
        y = cw_ref[0:1, :] * z2 + cw_ref[1:2, :] * z1 + cw_ref[2:3, :] * z + cb_ref[...]
        return y[halo:, :]

    h_prev = h_ref[(prev // n_col_tiles) % 2]
    zb = jnp.dot(h_prev, wb_ref[...].astype(BF16), preferred_element_type=F32)
    a = conv(za_ref[...], cwa_ref, cba_ref)
    o_ref[...] = (a * jax.nn.sigmoid(a) * conv(zb, cwb_ref, cbb_ref)).astype(o_ref.dtype)
    za_ref[...] = jnp.dot(h_ref[row_c % 2], wa_ref[...].astype(BF16), preferred_element_type=F32)


def _ffn_up(x2d, gain, w_up, layer, conv_w, conv_b, seq):
    T, D = x2d.shape
    F = w_up.shape[2] // 2
    halo = 16
    tm = _tile(seq, 1024)
    tn = _tile(F, 512, MXU_COLS)
    nj = F // tn
    n_tiles = (T // tm) * nj
    tiles_per_seq = seq // tm
    blocks = [((tm, D), F32), ((halo, D), F32), ((1, D), F32), ((D, tn), F32), ((D, tn), F32),
              ((3, tn), F32), ((3, tn), F32), ((1, tn), F32), ((1, tn), F32), ((tm, tn), BF16)]
    scratch = [((2, tm + halo, D), BF16), ((tm + halo, tn), F32)]
    hpt = tm // halo
    cur = lambda s: jnp.minimum(s, n_tiles - 1)
    prev = lambda s: jnp.maximum(s - 1, 0)
    return pl.pallas_call(
        functools.partial(_ffn_up_kernel, n_tiles=n_tiles, n_col_tiles=nj, tiles_per_seq=tiles_per_seq,
                          halo=halo),
        grid=(n_tiles + 1,),
        in_specs=[
            pl.BlockSpec((tm, D), lambda s: (cur(s) // nj, 0)),
            pl.BlockSpec((halo, D), lambda s: (jnp.maximum(cur(s) // nj * hpt - 1, 0), 0)),
            pl.BlockSpec((1, D), lambda s: (0, 0)),
            pl.BlockSpec((None, D, tn), lambda s: (layer, 0, cur(s) % nj)),
            pl.BlockSpec((None, D, tn), lambda s: (layer, 0, prev(s) % nj + nj)),
            pl.BlockSpec((3, tn), lambda s: (0, prev(s) % nj)),
            pl.BlockSpec((3, tn), lambda s: (0, prev(s) % nj + nj)),
            pl.BlockSpec((1, tn), lambda s: (0, prev(s) % nj)),
            pl.BlockSpec((1, tn), lambda s: (0, prev(s) % nj + nj)),
        ],
        out_specs=pl.BlockSpec((tm, tn), lambda s: (prev(s) // nj, prev(s) % nj)),
        out_shape=jax.ShapeDtypeStruct((T, F), BF16),
        scratch_shapes=[pltpu.VMEM(sh, dt) for sh, dt in scratch],
        compiler_params=_params(("arbitrary",),
                                _vmem_limit(blocks, scratch, [((tm + halo, MXU_COLS), F32)] * 8)),
        name="ffn_up_conv_gate",
    )(x2d, x2d, gain, w_up, w_up, conv_w, conv_w, conv_b, conv_b)


def _conv_ffn(x2d, gain, w_up, layer, conv_w, conv_b, w_down, seq):
    act = _ffn_up(x2d, gain, w_up, layer, conv_w, conv_b, seq)
    return _matmul_residual([act], w_down, layer, x2d, seq, "ffn_down", tn_pref=512)


def _qkvg_kernel(x_ref, gain_ref, w_ref, hg_ref, o_ref, h_ref, *, n_norm_tiles):
    j = pl.program_id(1)
    n_sub = o_ref.shape[1] // MXU_COLS

    @pl.when(j == 0)
    def _():
        _fill_normed(x_ref, gain_ref, h_ref)

    @pl.when(j < n_norm_tiles)
    def _():
        h = h_ref[...]
        for c in range(n_sub):
            y = _dot_nt(h, w_ref[c * MXU_COLS:(c + 1) * MXU_COLS, :])
            for grp in range(MXU_COLS // LANES):
                src = slice(grp * LANES, (grp + 1) * LANES)
                dst = slice(c * MXU_COLS + grp * LANES, c * MXU_COLS + (grp + 1) * LANES)
                o_ref[:, dst] = _rms_rows(y[:, src], hg_ref[:, dst]).astype(o_ref.dtype)

    @pl.when(j >= n_norm_tiles)
    def _():
        h = h_ref[...]
        for c in range(n_sub):
            cols = slice(c * MXU_COLS, (c + 1) * MXU_COLS)
            o_ref[:, cols] = _dot_nt(h, w_ref[cols, :]).astype(o_ref.dtype)


def _qkvg_proj(x2d, gain, w, layer, head_gain, n_cols, n_norm_cols, seq):
    T, D = x2d.shape
    tm = _tile(seq, 1024)
    tn = _tile(n_norm_cols // 2, 2048, MXU_COLS)
    blocks = [((tm, D), F32), ((1, D), F32), ((D, tn), w.dtype), ((1, tn), F32), ((tm, tn), BF16)]
    return pl.pallas_call(
        functools.partial(_qkvg_kernel, n_norm_tiles=n_norm_cols // tn),
        grid=(T // tm, n_cols // tn),
        in_specs=[
            pl.BlockSpec((tm, D), lambda i, j: (i, 0)),
            pl.BlockSpec((1, D), lambda i, j: (0, 0)),
            pl.BlockSpec((None, tn, D), lambda i, j: (layer, j, 0)),
            pl.BlockSpec((1, tn), lambda i, j: (0, j)),
        ],
        out_specs=[pl.BlockSpec((tm, tn), lambda i, j: (i, j)), pl.BlockSpec((tm, D), lambda i, j: (i, 0))],
        out_shape=[jax.ShapeDtypeStruct((T, n_cols), BF16), jax.ShapeDtypeStruct((T, D), BF16)],
        compiler_params=_params(("parallel", "arbitrary"),
                                _vmem_limit(blocks + [((tm, D), BF16)], [], [((tm, MXU_COLS), F32)] * 4)),
        name="qkvg_proj",
    )(x2d, gain, w, head_gain)


def _fgate_kernel(h_ref, wf_ref, bf_ref, pq_ref, pk_ref, cq_ref, ck_ref, qa_ref, ka_ref,
                  carry_ref, *, n_heads):
    @pl.when(pl.program_id(1) == 0)
    def _():
        carry_ref[...] = jnp.zeros_like(carry_ref)

    tm = h_ref.shape[0]
    f = jnp.dot(h_ref[...], wf_ref[...], preferred_element_type=F32) + bf_ref[...]
    log_f = (jnp.minimum(f, 0.0) - jnp.log1p(jnp.exp(-jnp.abs(f)))) * LOG2E
    c = _cumsum_rows(log_f, _tril_mask(tm).astype(BF16)) + carry_ref[...]
    carry_ref[...] = c[tm - 1:tm, :]
    hi = c.astype(BF16)
    r1 = c - hi.astype(F32)
    mid = r1.astype(BF16)
    lo = (r1 - mid.astype(F32)).astype(BF16)
    lane = lax.broadcasted_iota(jnp.int32, c.shape, 1)
    parts = jnp.where(lane < n_heads, hi, jnp.where(lane < 2 * n_heads, mid, lo))
    qa_ref[...] = (jnp.dot(parts, pq_ref[...], preferred_element_type=F32) + cq_ref[...]).astype(qa_ref.dtype)
    ka_ref[...] = (jnp.dot(parts, pk_ref[...], preferred_element_type=F32) + ck_ref[...]).astype(ka_ref.dtype)


def _fgate_selectors(n_heads):
    W = n_heads * LANES
    pq = np.zeros((LANES, W), np.float32)
    pk = np.zeros((LANES, W), np.float32)
    cq = np.zeros((1, W), np.float32)
    ck = np.zeros((1, W), np.float32)
    for hd in range(n_heads):
        for part in range(3):
            pq[part * n_heads + hd, hd * LANES + part] = 1.0
            pk[part * n_heads + hd, hd * LANES + 3 + part] = -1.0
            cq[0, hd * LANES + 3 + part] = 1.0
            ck[0, hd * LANES + part] = 1.0
    return jnp.asarray(pq, BF16), jnp.asarray(pk, BF16), jnp.asarray(cq), jnp.asarray(ck)


def _fgate(h, w_f, b_f, n_heads, batch, seq):
    T, D = h.shape
    tm = _tile(seq, 512)
    nt = seq // tm
    W = n_heads * LANES
    pq, pk, cq, ck = _fgate_selectors(n_heads)
    blocks = [((tm, D), BF16), ((D, LANES), BF16), ((1, LANES), F32),
              ((LANES, W), BF16), ((LANES, W), BF16), ((1, W), F32), ((1, W), F32),
              ((tm, W), BF16), ((tm, W), BF16)]
    const = lambda b, i: (0, 0)
    return pl.pallas_call(
        functools.partial(_fgate_kernel, n_heads=n_heads),
        grid=(batch, nt),
        in_specs=[
            pl.BlockSpec((tm, D), lambda b, i: (b * nt + i, 0)),
            pl.BlockSpec((D, LANES), const),
            pl.BlockSpec((1, LANES), const),
            pl.BlockSpec((LANES, W), const),
            pl.BlockSpec((LANES, W), const),
            pl.BlockSpec((1, W), const),
            pl.BlockSpec((1, W), const),
        ],
        out_specs=[pl.BlockSpec((tm, W), lambda b, i: (b * nt + i, 0)),
                   pl.BlockSpec((tm, W), lambda b, i: (b * nt + i, 0))],
        out_shape=[jax.ShapeDtypeStruct((T, W), BF16), jax.ShapeDtypeStruct((T, W), BF16)],
        scratch_shapes=[pltpu.VMEM((1, LANES), F32)],
        compiler_params=_params(("parallel", "arbitrary"),
                                _vmem_limit(blocks, [], [((tm, D), F32)] * 2 + [((tm, tm), BF16)])),
        name="fox_forget_cumsum",
    )(h, w_f, b_f, pq, pk, cq, ck)


def _attn_kernel(q_ref, qa_ref, k_ref, ka_ref, v_ref, g_ref, o_ref, qs_ref, acc_ref, m_ref, l_ref):
    qi = pl.program_id(2)
    tq = tk = q_ref.shape[0]
    n_sub = q_ref.shape[1] // LANES
    for hh in range(n_sub):
        lanes = slice(hh * LANES, (hh + 1) * LANES)
        qs_ref[hh] = jnp.concatenate([q_ref[:, lanes], qa_ref[:, lanes]], axis=1)
    acc_ref[...] = jnp.zeros_like(acc_ref)
    m_ref[...] = jnp.full(m_ref.shape, -jnp.inf, F32)
    l_ref[...] = jnp.zeros_like(l_ref)

    def step(first_key, n_keys, diag_offset=None):
        rows = pl.ds(pl.multiple_of(first_key, tk), n_keys)
        for hh in range(n_sub):
            lanes = slice(hh * LANES, (hh + 1) * LANES)
            k = jnp.concatenate([k_ref[rows, lanes], ka_ref[rows, lanes]], axis=1)
            s = _dot_nt(qs_ref[hh], k)
            if diag_offset is not None:
                t_pos = lax.broadcasted_iota(jnp.int32, (tq, n_keys), 0)
                s_pos = lax.broadcasted_iota(jnp.int32, (tq, n_keys), 1) + diag_offset
                s = jnp.where(s_pos <= t_pos, s, -jnp.inf)
            m_prev = m_ref[hh]
            m_new = jnp.maximum(m_prev, jnp.max(s, axis=1, keepdims=True))
            alpha = jnp.exp2(m_prev - m_new)
            p = jnp.exp2(s - jnp.tile(m_new, (1, n_keys // LANES)))
            m_ref[hh] = m_new
            l_ref[hh] = alpha * l_ref[hh] + jnp.sum(p, axis=1, keepdims=True)
            acc_ref[hh] = alpha * acc_ref[hh] + jnp.dot(p.astype(BF16), v_ref[rows, lanes],
                                                        preferred_element_type=F32)

    def loop(n, body):
        lax.fori_loop(0, n, lambda j, c: (body(j), c)[1], 0)

    loop(qi // 4, lambda j: step(j * 4 * tk, 4 * tk))
    loop((qi % 4) // 2, lambda j: step((qi // 4) * 4 * tk, 2 * tk))

    @pl.when(qi % 2 == 1)
    def _():
        step((qi - 1) * tk, 2 * tk, -tk)

    @pl.when(qi % 2 == 0)
    def _():
        step(qi * tk, tk, 0)
    for hh in range(n_sub):
        lanes = slice(hh * LANES, (hh + 1) * LANES)
        gate = jax.nn.sigmoid(g_ref[:, lanes].astype(F32))
        o_ref[:, lanes] = (acc_ref[hh] / l_ref[hh] * gate).astype(o_ref.dtype)


def _attention(qkvg, qa, ka, n_heads, batch, seq):
    T = qkvg.shape[0]
    tq = _tile(seq, ATTN_BLOCK)
    nq = seq // tq
    width = ATTN_HEADS_PER_STEP * LANES
    ng = n_heads * LANES // width
    blocks = [((tq, width), BF16)] * 2 + [((seq, width), BF16)] * 3 + [((tq, width), BF16)] * 2
    scratch = [((width // LANES, tq, 2 * LANES), BF16)] + [((width // LANES, tq, LANES), F32)] * 3
    return pl.pallas_call(
        _attn_kernel,
        grid=(batch, ng, nq),
        in_specs=[
            pl.BlockSpec((tq, width), lambda b, h, i: (b * nq + i, h)),
            pl.BlockSpec((tq, width), lambda b, h, i: (b * nq + i, h)),
            pl.BlockSpec((seq, width), lambda b, h, i: (b, h + ng)),
            pl.BlockSpec((seq, width), lambda b, h, i: (b, h)),
            pl.BlockSpec((seq, width), lambda b, h, i: (b, h + 2 * ng)),
            pl.BlockSpec((tq, width), lambda b, h, i: (b * nq + i, h + 3 * ng)),
        ],
        out_specs=pl.BlockSpec((tq, width), lambda b, h, i: (b * nq + i, h)),
        out_shape=jax.ShapeDtypeStruct((T, n_heads * LANES), BF16),
        scratch_shapes=[pltpu.VMEM(sh, dt) for sh, dt in scratch],
        compiler_params=_params(("parallel", "parallel", "arbitrary"),
                                _vmem_limit(blocks, scratch, [((tq, 4 * tq), F32)] * 2 * ATTN_HEADS_PER_STEP)),
        name="fox_attention",
    )(qkvg, qa, qkvg, ka, qkvg, qkvg)


def _mixer_ab(x2d, gain, w_in, j, sp_w, sp_b, v_gain, gamma, o_gain, w_out, layer, batch, seq):
    a_width = v_gain.shape[1]
    b_width = gamma.shape[1]
    y_a, h = _gmlp_proj(x2d, gain, w_in, j, v_gain, sp_w, sp_b, seq)
    qfig = _matmul(h, w_in, j, 2 * a_width, 4 * b_width, seq, F32, "hgrn_in_proj")
    y_b = _hgrn(qfig, gamma, o_gain, layer, batch, seq)
    return _matmul_residual([y_a, y_b], w_out, j, x2d, seq, "ab_out_proj", tm_pref=OUT_PROJ_ROWS)


def _mixer_c(x2d, gain, w_in, w_forget, j, b_f, q_gain, k_gain, w_out, batch, seq):
    D = x2d.shape[1]
    n_heads = b_f.shape[1]
    head_dim = q_gain.shape[1]
    assert head_dim == LANES and n_heads * head_dim == D
    scale = head_dim ** -0.5 * LOG2E
    head_gain = jnp.concatenate([jnp.tile(q_gain * scale, (1, n_heads)), jnp.tile(k_gain, (1, n_heads)),
                                 jnp.ones((1, 2 * D), F32)], axis=1)
    qkvg, h = _qkvg_proj(x2d, gain, w_in, j, head_gain, 4 * D, 2 * D, seq)
    assert 3 * n_heads <= LANES
    pad = ((0, 0), (0, LANES - 3 * n_heads))
    w_f = jnp.pad(jnp.tile(w_forget, (1, 3)), pad).astype(BF16)
    b_fp = jnp.pad(jnp.tile(b_f, (1, 3)), pad)
    qa, ka = _fgate(h, w_f, b_fp, n_heads, batch, seq)
    o = _attention(qkvg, qa, ka, n_heads, batch, seq)
    return _matmul_residual([o], w_out, j, x2d, seq, "attn_out_proj", tm_pref=OUT_PROJ_ROWS)


def kernel(x, mix_norm, ab_w_in, ab_sp_w, ab_sp_b, ab_v_norm, hgrn_gamma, hgrn_o_norm, ab_w_out,
           c_w_in, c_b_f, c_q_norm, c_k_norm, c_w_out, ffn_norm, ffn_w_up, ffn_conv_w, ffn_conv_b,
           ffn_w_down):
    batch, seq, D = x.shape
    depth = mix_norm.shape[0]
    x2d = x.reshape(batch * seq, D)
    ab_w_in_b = ab_w_in.astype(BF16)
    c_w_in_t = jnp.swapaxes(c_w_in, 1, 2).astype(BF16)
    ffn_w_down = ffn_w_down.astype(BF16)
    ab_w_out = ab_w_out.astype(BF16)
    c_w_out = c_w_out.astype(BF16)
    for l in range(depth):
        j = l // 2
        gain = mix_norm[l][None, :]
        if l % 2 == 0:
            x2d = _mixer_ab(x2d, gain, ab_w_in_b, j, ab_sp_w[j], ab_sp_b[j], ab_v_norm[j][None, :],
                            hgrn_gamma, hgrn_o_norm[j][None, :], ab_w_out, l, batch, seq)
        else:
            x2d = _mixer_c(x2d, gain, c_w_in_t, c_w_in[j, :, 4 * D:], j, c_b_f[j][None, :],
                           c_q_norm[j][None, :], c_k_norm[j][None, :], c_w_out, batch, seq)
        x2d = _conv_ffn(x2d, ffn_norm[l][None, :], ffn_w_up, l, ffn_conv_w[l],
                        ffn_conv_b[l][None, :], ffn_w_down, seq)
    return x2d.reshape(batch, seq, D)
```

```python
import functools
import math

import jax
import jax.numpy as jnp
import numpy as np
from jax import lax
from jax.experimental import pallas as pl
from jax.experimental.pallas import tpu as pltpu

F32 = jnp.float32
BF16 = jnp.bfloat16
RMS_EPS = 1e-6
LANES = 128
SUBLANES = 8
MXU_COLS = 256
HGRN_CHUNK = 64
HGRN_MAX_FACTORED_RANGE = 60.0
OUT_PROJ_ROWS = 2048
ATTN_BLOCK = 512
ATTN_HEADS_PER_STEP = 4
LOG2E = 1.4426950408889634
V7X_VMEM_BYTES = 64 * 1024 * 1024
VMEM_CAP_BYTES = V7X_VMEM_BYTES - 6 * 1024 * 1024


def _nbytes(shape, dtype):
    return math.prod(shape) * jnp.dtype(dtype).itemsize


def _vmem_limit(blocks, scratch=(), temps=()):
    est = 2 * sum(_nbytes(s, d) for s, d in blocks)
    est += sum(_nbytes(s, d) for s, d in scratch) + sum(_nbytes(s, d) for s, d in temps)
    return int(min(VMEM_CAP_BYTES, est * 5 // 4 + (4 << 20)))


def _tile(n, pref, mult=LANES):
    if n <= pref:
        return n
    t = (pref // mult) * mult
    while n % t:
        t -= mult
    return t


def _params(sem, limit):
    return pltpu.CompilerParams(dimension_semantics=sem, vmem_limit_bytes=limit)


def _rms_rows(xf, gain):
    ms = jnp.mean(xf * xf, axis=-1, keepdims=True)
    return xf * lax.rsqrt(ms + RMS_EPS) * gain


def _gelu(x):
    return 0.5 * x * (1.0 + lax.erf(x * (2.0 ** -0.5)))


def _fill_normed(x_ref, gain_ref, h_ref, rows=256):
    rows = min(rows, x_ref.shape[0])

    def body(r, carry):
        sl = pl.ds(pl.multiple_of(r * rows, rows), rows)
        h_ref[sl, :] = _rms_rows(x_ref[sl, :], gain_ref[...]).astype(h_ref.dtype)
        return carry

    lax.fori_loop(0, x_ref.shape[0] // rows, body, 0)


def _tril_mask(n):
    t = lax.broadcasted_iota(jnp.int32, (n, n), 0)
    s = lax.broadcasted_iota(jnp.int32, (n, n), 1)
    return s <= t


def _cumsum_rows(x, tril_bf16):
    hi = x.astype(BF16)
    r1 = x - hi.astype(F32)
    mid = r1.astype(BF16)
    lo = (r1 - mid.astype(F32)).astype(BF16)
    y = jnp.dot(tril_bf16, jnp.concatenate([hi, mid, lo], axis=1), preferred_element_type=F32)
    d = x.shape[1]
    return y[:, :d] + y[:, d:2 * d] + y[:, 2 * d:]


def _dot_nt(a, b):
    return lax.dot_general(a, b, (((1,), (1,)), ((), ())), preferred_element_type=F32)


def _dot_tn(a, b):
    return lax.dot_general(a, b, (((0,), (0,)), ((), ())), preferred_element_type=F32)


def _gmlp_kernel(x_ref, gain_ref, wu_ref, wv_ref, vg_ref, spw_ref, spb_ref, o_ref, h_ref, *, chunk):
    @pl.when(pl.program_id(1) == 0)
    def _():
        _fill_normed(x_ref, gain_ref, h_ref)

    h = h_ref[...]
    u = _gelu(jnp.dot(h, wu_ref[...].astype(BF16), preferred_element_type=F32))
    v = _gelu(jnp.dot(h, wv_ref[...].astype(BF16), preferred_element_type=F32))
    tm, tn = u.shape
    n_chunks = tm // chunk
    tril = _tril_mask(chunk)
    for hh in range(tn // LANES):
        lanes = slice(hh * LANES, (hh + 1) * LANES)
        vh = _rms_rows(v[:, lanes], vg_ref[:, lanes]).astype(BF16)
        vcat = jnp.concatenate([vh[c * chunk:(c + 1) * chunk, :] for c in range(n_chunks)], axis=1)
        w_causal = jnp.where(tril, spw_ref[hh], 0.0).astype(BF16)
        mixed = jnp.dot(w_causal, vcat, preferred_element_type=F32)
        for c in range(n_chunks):
            rows = slice(c * chunk, (c + 1) * chunk)
            m_c = mixed[:, c * LANES:(c + 1) * LANES] + spb_ref[hh]
            o_ref[rows, lanes] = (u[rows, lanes] * m_c).astype(o_ref.dtype)


def _gmlp_proj(x2d, gain, w_in, layer, v_gain, sp_w, sp_b, seq):
    T, D = x2d.shape
    n_heads, chunk, _ = sp_w.shape
    a_width = n_heads * LANES
    tm = _tile(seq, 1024, chunk)
    tn = _tile(a_width, 1024)
    nj = a_width // tn
    spb = jnp.broadcast_to(sp_b[:, :, None], (n_heads, chunk, LANES))
    blocks = [((tm, D), F32), ((1, D), F32), ((D, tn), w_in.dtype), ((D, tn), w_in.dtype), ((1, tn), F32),
              ((tn // LANES, chunk, chunk), F32), ((tn // LANES, chunk, LANES), F32), ((tm, tn), BF16)]
    return pl.pallas_call(
        functools.partial(_gmlp_kernel, chunk=chunk),
        grid=(T // tm, nj),
        in_specs=[
            pl.BlockSpec((tm, D), lambda i, j: (i, 0)),
            pl.BlockSpec((1, D), lambda i, j: (0, 0)),
            pl.BlockSpec((None, D, tn), lambda i, j: (layer, 0, j)),
            pl.BlockSpec((None, D, tn), lambda i, j: (layer, 0, j + nj)),
            pl.BlockSpec((1, tn), lambda i, j: (0, j)),
            pl.BlockSpec((tn // LANES, chunk, chunk), lambda i, j: (j, 0, 0)),
            pl.BlockSpec((tn // LANES, chunk, LANES), lambda i, j: (j, 0, 0)),
        ],
        out_specs=[pl.BlockSpec((tm, tn), lambda i, j: (i, j)), pl.BlockSpec((tm, D), lambda i, j: (i, 0))],
        out_shape=[jax.ShapeDtypeStruct((T, a_width), BF16), jax.ShapeDtypeStruct((T, D), BF16)],
        compiler_params=_params(("parallel", "arbitrary"),
                                _vmem_limit(blocks + [((tm, D), BF16)], [], [((tm, tn), F32)] * 6)),
        name="gmlp_proj",
    )(x2d, gain, w_in, w_in, v_gain, sp_w, spb)


def _mm_kernel(h_ref, w_ref, o_ref):
    o_ref[...] = jnp.dot(h_ref[...], w_ref[...].astype(BF16), preferred_element_type=F32).astype(o_ref.dtype)


def _matmul(h, w, layer, col0, n_cols, seq, out_dtype, name):
    T, D = h.shape
    tm = _tile(seq, 1024)
    tn = _tile(n_cols, 2048)
    assert col0 % tn == 0
    j0 = col0 // tn
    blocks = [((tm, D), BF16), ((D, tn), w.dtype), ((tm, tn), out_dtype)]
    return pl.pallas_call(
        _mm_kernel,
        grid=(T // tm, n_cols // tn),
        in_specs=[
            pl.BlockSpec((tm, D), lambda i, j: (i, 0)),
            pl.BlockSpec((None, D, tn), lambda i, j: (layer, 0, j + j0)),
        ],
        out_specs=pl.BlockSpec((tm, tn), lambda i, j: (i, j)),
        out_shape=jax.ShapeDtypeStruct((T, n_cols), out_dtype),
        compiler_params=_params(("parallel", "arbitrary"), _vmem_limit(blocks, [], [((tm, tn), F32)])),
        name=name,
    )(h, w)


def _hgrn_kernel(gamma_ref, q_ref, f_ref, i_ref, g_ref, og_ref, o_ref, st_ref, g_scr, k_scr, intra_scr, inter_scr,
                 *, layer, chunk):
    @pl.when(pl.program_id(2) == 0)
    def _():
        st_ref[...] = jnp.zeros_like(st_ref)

    gam = gamma_ref[...]
    ex = jnp.exp(gam - jnp.max(gam, axis=0, keepdims=True))
    lb_all = jnp.sum(ex[:layer + 1], axis=0, keepdims=True) / jnp.sum(ex, axis=0, keepdims=True)

    L, width = q_ref.shape
    n_chunks = L // chunk
    mid = chunk // 2 - 1
    row_chunk = lax.broadcasted_iota(jnp.int32, (L, L), 0) // chunk
    col_chunk = lax.broadcasted_iota(jnp.int32, (L, L), 1) // chunk
    mask = _tril_mask(L) & (row_chunk == col_chunk)
    chunk_of_row = lax.broadcasted_iota(jnp.int32, (L, LANES), 0) // chunk

    fl = f_ref[...]
    e = jnp.exp(-jnp.abs(fl))
    r = 1.0 / (1.0 + e)
    pos = fl >= 0
    sig = jnp.where(pos, r, e * r)
    nsig = jnp.where(pos, e * r, r)
    kk = (1.0 - lb_all) * nsig
    G = _cumsum_rows(jnp.log(lb_all + (1.0 - lb_all) * sig), mask.astype(BF16))
    g_scr[...] = G
    k_scr[...] = kk
    g_mid_rows = [G[c * chunk + mid:c * chunk + mid + 1, :] for c in range(n_chunks)]
    g_end_rows = [G[(c + 1) * chunk - 1:(c + 1) * chunk, :] for c in range(n_chunks)]
    per_row = lambda rows: jnp.concatenate([jnp.broadcast_to(x, (chunk, width)) for x in rows], axis=0)
    g_mid = per_row(g_mid_rows)
    g_end = per_row(g_end_rows)
    qv = q_ref[...]
    q_t = (qv * jnp.exp(G - g_mid)).astype(BF16)
    k_t = (kk * jnp.exp(g_mid - G)).astype(BF16)
    q_g = (qv * jnp.exp(G)).astype(BF16)
    k_end = (kk * jnp.exp(g_end - G)).astype(BF16)
    vv = i_ref[...].astype(BF16)
    decay_range = -jnp.min(jnp.concatenate(g_end_rows, axis=0))
    factorable = decay_range <= HGRN_MAX_FACTORED_RANGE

    def finish(o, lanes):
        gv = g_ref[:, lanes]
        o_ref[:, lanes] = (_rms_rows(o, og_ref[...]) * (gv * jax.nn.sigmoid(gv))).astype(o_ref.dtype)

    for hh in range(width // LANES):
        lanes = slice(hh * LANES, (hh + 1) * LANES)
        scores = jnp.where(mask, _dot_nt(q_t[:, lanes], k_t[:, lanes]), 0.0).astype(BF16)
        intra = jnp.dot(scores, vv[:, lanes], preferred_element_type=F32)
        v_exp = jnp.concatenate([jnp.where(chunk_of_row == c, vv[:, lanes], 0) for c in range(n_chunks)],
                                axis=1)
        u_t = _dot_tn(v_exp, k_end[:, lanes])
        st = st_ref[hh]
        prev = []
        for c in range(n_chunks):
            prev.append(st.astype(BF16))
            st = st * jnp.exp(g_end_rows[c][:, lanes]) + u_t[c * LANES:(c + 1) * LANES, :]
        st_ref[hh] = st
        q_exp = jnp.concatenate([jnp.where(chunk_of_row == c, q_g[:, lanes], 0) for c in range(n_chunks)],
                                axis=1)
        inter = _dot_nt(q_exp, jnp.concatenate(prev, axis=1))
        inter_scr[hh] = inter
        finish(intra + inter, lanes)

    @pl.when(jnp.logical_not(factorable))
    def _():
        for hh in range(width // LANES):
            _hgrn_intra_pairwise(q_ref, i_ref, g_scr, k_scr, intra_scr, hh, chunk)
            finish(intra_scr[hh] + inter_scr[hh], slice(hh * LANES, (hh + 1) * LANES))


def _hgrn_intra_pairwise(q_ref, i_ref, g_scr, k_scr, intra_scr, hh, chunk):
    lanes = slice(hh * LANES, (hh + 1) * LANES)
    t_idx = lax.broadcasted_iota(jnp.int32, (chunk, 1), 0)
    for c in range(q_ref.shape[0] // chunk):
        rows = slice(c * chunk, (c + 1) * chunk)
        g_c = g_scr[rows, lanes]
        q_c = q_ref[rows, lanes]

        def body(grp, acc):
            keys = pl.ds(pl.multiple_of(c * chunk + grp * SUBLANES, SUBLANES), SUBLANES)
            g_s, k_s, v_s = g_scr[keys, lanes], k_scr[keys, lanes], i_ref[keys, lanes]
            for j in range(SUBLANES):
                w = q_c * k_s[j:j + 1] * jnp.exp(jnp.minimum(g_c - g_s[j:j + 1], 0.0))
                score = jnp.where(t_idx >= grp * SUBLANES + j, jnp.sum(w, axis=1, keepdims=True), 0.0)
                acc = acc + score * v_s[j:j + 1]
            return acc

        intra_scr[hh, rows, :] = lax.fori_loop(0, chunk // SUBLANES, body, jnp.zeros((chunk, LANES), F32))


def _hgrn(qfig, gamma, o_gain, layer, batch, seq):
    T = qfig.shape[0]
    b_width = gamma.shape[1]
    n_layers = gamma.shape[0]
    width = _tile(b_width, 1024)
    L = _tile(seq, 256, HGRN_CHUNK)
    nw = b_width // width
    nl = seq // L
    blocks = [((L, width), F32)] * 4 + [((L, width), BF16), ((n_layers, width), F32)]
    n_sub = width // LANES
    scratch = [((n_sub, LANES, LANES), F32), ((L, width), F32), ((L, width), F32), ((n_sub, L, LANES), F32),
               ((n_sub, L, LANES), F32)]
    row = lambda b, h, l: b * nl + l
    return pl.pallas_call(
        functools.partial(_hgrn_kernel, layer=layer, chunk=HGRN_CHUNK),
        grid=(batch, nw, nl),
        in_specs=[
            pl.BlockSpec((n_layers, width), lambda b, h, l: (0, h)),
            pl.BlockSpec((L, width), lambda b, h, l: (row(b, h, l), h)),
            pl.BlockSpec((L, width), lambda b, h, l: (row(b, h, l), h + nw)),
            pl.BlockSpec((L, width), lambda b, h, l: (row(b, h, l), h + 2 * nw)),
            pl.BlockSpec((L, width), lambda b, h, l: (row(b, h, l), h + 3 * nw)),
            pl.BlockSpec((1, LANES), lambda b, h, l: (0, 0)),
        ],
        out_specs=pl.BlockSpec((L, width), lambda b, h, l: (row(b, h, l), h)),
        out_shape=jax.ShapeDtypeStruct((T, b_width), BF16),
        scratch_shapes=[pltpu.VMEM(sh, dt) for sh, dt in scratch],
        compiler_params=_params(("parallel", "parallel", "arbitrary"),
                                _vmem_limit(blocks, scratch, [((L, width), F32)] * 12)),
        name="hgrn2",
    )(gamma, qfig, qfig, qfig, qfig, o_gain)


def _mm_resid_kernel(*refs, n_lhs):
    lhs_refs = refs[:n_lhs]
    w_ref, r_ref, o_ref = refs[n_lhs:]
    acc = r_ref[...]
    k0 = 0
    for a_ref in lhs_refs:
        k = a_ref.shape[1]
        acc = acc + jnp.dot(a_ref[...], w_ref[k0:k0 + k, :].astype(BF16), preferred_element_type=F32)
        k0 += k
    o_ref[...] = acc


def _matmul_residual(lhs_list, w, layer, resid, seq, name, tm_pref=1024, tn_pref=512):
    T, N = resid.shape
    K = w.shape[1]
    tm = _tile(seq, tm_pref)
    tn = _tile(N, tn_pref)
    blocks = [((tm, a.shape[1]), BF16) for a in lhs_list] + [((K, tn), w.dtype), ((tm, tn), F32), ((tm, tn), F32)]
    return pl.pallas_call(
        functools.partial(_mm_resid_kernel, n_lhs=len(lhs_list)),
        grid=(T // tm, N // tn),
        in_specs=[pl.BlockSpec((tm, a.shape[1]), lambda i, j: (i, 0)) for a in lhs_list] + [
            pl.BlockSpec((None, K, tn), lambda i, j: (layer, 0, j)),
            pl.BlockSpec((tm, tn), lambda i, j: (i, j)),
        ],
        out_specs=pl.BlockSpec((tm, tn), lambda i, j: (i, j)),
        out_shape=jax.ShapeDtypeStruct((T, N), F32),
        compiler_params=_params(("parallel", "arbitrary"), _vmem_limit(blocks, [], [((tm, tn), F32)])),
        name=name,
    )(*lhs_list, w, resid)


def _ffn_up_kernel(x_ref, xh_ref, gain_ref, wa_ref, wb_ref, cwa_ref, cwb_ref, cba_ref, cbb_ref,
                   o_ref, h_ref, za_ref, *, n_tiles, n_col_tiles, tiles_per_seq, halo):
    s = pl.program_id(0)
    cur = jnp.minimum(s, n_tiles - 1)
    prev = jnp.maximum(s - 1, 0)
    row_c = cur // n_col_tiles
    tm = x_ref.shape[0]

    @pl.when(s == 0)
    def _():
        za_ref[...] = jnp.zeros_like(za_ref)

    @pl.when((cur % n_col_tiles == 0) & (s < n_tiles))
    def _():
        slot = row_c % 2
        keep = (row_c % tiles_per_seq != 0).astype(F32)
        h_ref[slot, 0:halo, :] = (_rms_rows(xh_ref[...], gain_ref[...]) * keep).astype(h_ref.dtype)
        rows = min(256, tm)

        def body(r, carry):
            src = pl.ds(pl.multiple_of(r * rows, rows), rows)
            dst = pl.ds(pl.multiple_of(r * rows + halo, halo), rows)
            h_ref[slot, dst, :] = _rms_rows(x_ref[src, :], gain_ref[...]).astype(h_ref.dtype)
            return carry

        lax.fori_loop(0, tm // rows, body, 0)

    def conv(z, cw_ref, cb_ref):
        z1 = pltpu.roll(z, 1, 0)
        z2 = pltpu.roll(z, 2, 0)
        y = cw_ref[0:1, :] * z2 + cw_ref[1:2, :] * z1 + cw_ref[2:3, :] * z + cb_ref[...]
        return y[halo:, :]

    h_prev = h_ref[(prev // n_col_tiles) % 2]
    zb = jnp.dot(h_prev, wb_ref[...].astype(BF16), preferred_element_type=F32)
    a = conv(za_ref[...], cwa_ref, cba_ref)
    o_ref[...] = (a * jax.nn.sigmoid(a) * conv(zb, cwb_ref, cbb_ref)).astype(o_ref.dtype)
    za_ref[...] = jnp.dot(h_ref[row_c % 2], wa_ref[...].astype(BF16), preferred_element_type=F32)


def _ffn_up(x2d, gain, w_up, layer, conv_w, conv_b, seq):
    T, D = x2d.shape
    F = w_up.shape[2] // 2
    halo = 16
    tm = _tile(seq, 1024)
    tn = _tile(F, 512, MXU_COLS)
    nj = F // tn
    n_tiles = (T // tm) * nj
    tiles_per_seq = seq // tm
    blocks = [((tm, D), F32), ((halo, D), F32), ((1, D), F32), ((D, tn), F32), ((D, tn), F32),
              ((3, tn), F32), ((3, tn), F32), ((1, tn), F32), ((1, tn), F32), ((tm, tn), BF16)]
    scratch = [((2, tm + halo, D), BF16), ((tm + halo, tn), F32)]
    hpt = tm // halo
    cur = lambda s: jnp.minimum(s, n_tiles - 1)
    prev = lambda s: jnp.maximum(s - 1, 0)
    return pl.pallas_call(
        functools.partial(_ffn_up_kernel, n_tiles=n_tiles, n_col_tiles=nj, tiles_per_seq=tiles_per_seq,
                          halo=halo),
        grid=(n_tiles + 1,),
        in_specs=[
            pl.BlockSpec((tm, D), lambda s: (cur(s) // nj, 0)),
            pl.BlockSpec((halo, D), lambda s: (jnp.maximum(cur(s) // nj * hpt - 1, 0), 0)),
            pl.BlockSpec((1, D), lambda s: (0, 0)),
            pl.BlockSpec((None, D, tn), lambda s: (layer, 0, cur(s) % nj)),
            pl.BlockSpec((None, D, tn), lambda s: (layer, 0, prev(s) % nj + nj)),
            pl.BlockSpec((3, tn), lambda s: (0, prev(s) % nj)),
            pl.BlockSpec((3, tn), lambda s: (0, prev(s) % nj + nj)),
            pl.BlockSpec((1, tn), lambda s: (0, prev(s) % nj)),
            pl.BlockSpec((1, tn), lambda s: (0, prev(s) % nj + nj)),
        ],
        out_specs=pl.BlockSpec((tm, tn), lambda s: (prev(s) // nj, prev(s) % nj)),
        out_shape=jax.ShapeDtypeStruct((T, F), BF16),
        scratch_shapes=[pltpu.VMEM(sh, dt) for sh, dt in scratch],
        compiler_params=_params(("arbitrary",),
                                _vmem_limit(blocks, scratch, [((tm + halo, MXU_COLS), F32)] * 8)),
        name="ffn_up_conv_gate",
    )(x2d, x2d, gain, w_up, w_up, conv_w, conv_w, conv_b, conv_b)


def _conv_ffn(x2d, gain, w_up, layer, conv_w, conv_b, w_down, seq):
    act = _ffn_up(x2d, gain, w_up, layer, conv_w, conv_b, seq)
    return _matmul_residual([act], w_down, layer, x2d, seq, "ffn_down", tn_pref=512)


def _qkvg_kernel(x_ref, gain_ref, w_ref, hg_ref, o_ref, h_ref, *, n_norm_tiles):
    j = pl.program_id(1)
    n_sub = o_ref.shape[1] // MXU_COLS

    @pl.when(j == 0)
    def _():
        _fill_normed(x_ref, gain_ref, h_ref)

    @pl.when(j < n_norm_tiles)
    def _():
        h = h_ref[...]
        for c in range(n_sub):
            y = _dot_nt(h, w_ref[c * MXU_COLS:(c + 1) * MXU_COLS, :])
            for grp in range(MXU_COLS // LANES):
                src = slice(grp * LANES, (grp + 1) * LANES)
                dst = slice(c * MXU_COLS + grp * LANES, c * MXU_COLS + (grp + 1) * LANES)
                o_ref[:, dst] = _rms_rows(y[:, src], hg_ref[:, dst]).astype(o_ref.dtype)

    @pl.when(j >= n_norm_tiles)
    def _():
        h = h_ref[...]
        for c in range(n_sub):
            cols = slice(c * MXU_COLS, (c + 1) * MXU_COLS)
            o_ref[:, cols] = _dot_nt(h, w_ref[cols, :]).astype(o_ref.dtype)


def _qkvg_proj(x2d, gain, w, layer, head_gain, n_cols, n_norm_cols, seq):
    T, D = x2d.shape
    tm = _tile(seq, 1024)
    tn = _tile(n_norm_cols // 2, 2048, MXU_COLS)
    blocks = [((tm, D), F32), ((1, D), F32), ((D, tn), w.dtype), ((1, tn), F32), ((tm, tn), BF16)]
    return pl.pallas_call(
        functools.partial(_qkvg_kernel, n_norm_tiles=n_norm_cols // tn),
        grid=(T // tm, n_cols // tn),
        in_specs=[
            pl.BlockSpec((tm, D), lambda i, j: (i, 0)),
            pl.BlockSpec((1, D), lambda i, j: (0, 0)),
            pl.BlockSpec((None, tn, D), lambda i, j: (layer, j, 0)),
            pl.BlockSpec((1, tn), lambda i, j: (0, j)),
        ],
        out_specs=[pl.BlockSpec((tm, tn), lambda i, j: (i, j)), pl.BlockSpec((tm, D), lambda i, j: (i, 0))],
        out_shape=[jax.ShapeDtypeStruct((T, n_cols), BF16), jax.ShapeDtypeStruct((T, D), BF16)],
        compiler_params=_params(("parallel", "arbitrary"),
                                _vmem_limit(blocks + [((tm, D), BF16)], [], [((tm, MXU_COLS), F32)] * 4)),
        name="qkvg_proj",
    )(x2d, gain, w, head_gain)


def _fgate_kernel(h_ref, wf_ref, bf_ref, pq_ref, pk_ref, cq_ref, ck_ref, qa_ref, ka_ref,
                  carry_ref, *, n_heads):
    @pl.when(pl.program_id(1) == 0)
    def _():
        carry_ref[...] = jnp.zeros_like(carry_ref)

    tm = h_ref.shape[0]
    f = jnp.dot(h_ref[...], wf_ref[...], preferred_element_type=F32) + bf_ref[...]
    log_f = (jnp.minimum(f, 0.0) - jnp.log1p(jnp.exp(-jnp.abs(f)))) * LOG2E
    c = _cumsum_rows(log_f, _tril_mask(tm).astype(BF16)) + carry_ref[...]
    carry_ref[...] = c[tm - 1:tm, :]
    hi = c.astype(BF16)
    r1 = c - hi.astype(F32)
    mid = r1.astype(BF16)
    lo = (r1 - mid.astype(F32)).astype(BF16)
    lane = lax.broadcasted_iota(jnp.int32, c.shape, 1)
    parts = jnp.where(lane < n_heads, hi, jnp.where(lane < 2 * n_heads, mid, lo))
    qa_ref[...] = (jnp.dot(parts, pq_ref[...], preferred_element_type=F32) + cq_ref[...]).astype(qa_ref.dtype)
    ka_ref[...] = (jnp.dot(parts, pk_ref[...], preferred_element_type=F32) + ck_ref[...]).astype(ka_ref.dtype)


def _fgate_selectors(n_heads):
    W = n_heads * LANES
    pq = np.zeros((LANES, W), np.float32)
    pk = np.zeros((LANES, W), np.float32)
    cq = np.zeros((1, W), np.float32)
    ck = np.zeros((1, W), np.float32)
    for hd in range(n_heads):
        for part in range(3):
            pq[part * n_heads + hd, hd * LANES + part] = 1.0
            pk[part * n_heads + hd, hd * LANES + 3 + part] = -1.0
            cq[0, hd * LANES + 3 + part] = 1.0
            ck[0, hd * LANES + part] = 1.0
    return jnp.asarray(pq, BF16), jnp.asarray(pk, BF16), jnp.asarray(cq), jnp.asarray(ck)


def _fgate(h, w_f, b_f, n_heads, batch, seq):
    T, D = h.shape
    tm = _tile(seq, 512)
    nt = seq // tm
    W = n_heads * LANES
    pq, pk, cq, ck = _fgate_selectors(n_heads)
    blocks = [((tm, D), BF16), ((D, LANES), BF16), ((1, LANES), F32),
              ((LANES, W), BF16), ((LANES, W), BF16), ((1, W), F32), ((1, W), F32),
              ((tm, W), BF16), ((tm, W), BF16)]
    const = lambda b, i: (0, 0)
    return pl.pallas_call(
        functools.partial(_fgate_kernel, n_heads=n_heads),
        grid=(batch, nt),
        in_specs=[
            pl.BlockSpec((tm, D), lambda b, i: (b * nt + i, 0)),
            pl.BlockSpec((D, LANES), const),
            pl.BlockSpec((1, LANES), const),
            pl.BlockSpec((LANES, W), const),
            pl.BlockSpec((LANES, W), const),
            pl.BlockSpec((1, W), const),
            pl.BlockSpec((1, W), const),
        ],
        out_specs=[pl.BlockSpec((tm, W), lambda b, i: (b * nt + i, 0)),
                   pl.BlockSpec((tm, W), lambda b, i: (b * nt + i, 0))],
        out_shape=[jax.ShapeDtypeStruct((T, W), BF16), jax.ShapeDtypeStruct((T, W), BF16)],
        scratch_shapes=[pltpu.VMEM((1, LANES), F32)],
        compiler_params=_params(("parallel", "arbitrary"),
                                _vmem_limit(blocks, [], [((tm, D), F32)] * 2 + [((tm, tm), BF16)])),
        name="fox_forget_cumsum",
    )(h, w_f, b_f, pq, pk, cq, ck)


def _attn_kernel(q_ref, qa_ref, k_ref, ka_ref, v_ref, g_ref, o_ref, qs_ref, acc_ref, m_ref, l_ref):
    qi = pl.program_id(2)
    tq = tk = q_ref.shape[0]
    n_sub = q_ref.shape[1] // LANES
    for hh in range(n_sub):
        lanes = slice(hh * LANES, (hh + 1) * LANES)
        qs_ref[hh] = jnp.concatenate([q_ref[:, lanes], qa_ref[:, lanes]], axis=1)
    acc_ref[...] = jnp.zeros_like(acc_ref)
    m_ref[...] = jnp.full(m_ref.shape, -jnp.inf, F32)
    l_ref[...] = jnp.zeros_like(l_ref)

    def step(first_key, n_keys, diag_offset=None):
        rows = pl.ds(pl.multiple_of(first_key, tk), n_keys)
        for hh in range(n_sub):
            lanes = slice(hh * LANES, (hh + 1) * LANES)
            k = jnp.concatenate([k_ref[rows, lanes], ka_ref[rows, lanes]], axis=1)
            s = _dot_nt(qs_ref[hh], k)
            if diag_offset is not None:
                t_pos = lax.broadcasted_iota(jnp.int32, (tq, n_keys), 0)
                s_pos = lax.broadcasted_iota(jnp.int32, (tq, n_keys), 1) + diag_offset
                s = jnp.where(s_pos <= t_pos, s, -jnp.inf)
            m_prev = m_ref[hh]
            m_new = jnp.maximum(m_prev, jnp.max(s, axis=1, keepdims=True))
            alpha = jnp.exp2(m_prev - m_new)
            p = jnp.exp2(s - jnp.tile(m_new, (1, n_keys // LANES)))
            m_ref[hh] = m_new
            l_ref[hh] = alpha * l_ref[hh] + jnp.sum(p, axis=1, keepdims=True)
            acc_ref[hh] = alpha * acc_ref[hh] + jnp.dot(p.astype(BF16), v_ref[rows, lanes],
                                                        preferred_element_type=F32)

    def loop(n, body):
        lax.fori_loop(0, n, lambda j, c: (body(j), c)[1], 0)

    n_blocks = k_ref.shape[0] // tk
    if n_blocks > 4:
        loop(qi // 4, lambda j: step(j * 4 * tk, 4 * tk))
    if n_blocks > 2:
        loop((qi % 4) // 2, lambda j: step((qi // 4) * 4 * tk, 2 * tk))

    @pl.when(qi % 2 == 1)
    def _():
        step((qi - 1) * tk, 2 * tk, -tk)

    @pl.when(qi % 2 == 0)
    def _():
        step(qi * tk, tk, 0)
    for hh in range(n_sub):
        lanes = slice(hh * LANES, (hh + 1) * LANES)
        gate = jax.nn.sigmoid(g_ref[:, lanes].astype(F32))
        o_ref[:, lanes] = (acc_ref[hh] / l_ref[hh] * gate).astype(o_ref.dtype)


def _attention(qkvg, qa, ka, n_heads, batch, seq):
    T = qkvg.shape[0]
    tq = _tile(seq, ATTN_BLOCK)
    nq = seq // tq
    width = ATTN_HEADS_PER_STEP * LANES
    ng = n_heads * LANES // width
    blocks = [((tq, width), BF16)] * 2 + [((seq, width), BF16)] * 3 + [((tq, width), BF16)] * 2
    scratch = [((width // LANES, tq, 2 * LANES), BF16)] + [((width // LANES, tq, LANES), F32)] * 3
    return pl.pallas_call(
        _attn_kernel,
        grid=(batch, ng, nq),
        in_specs=[
            pl.BlockSpec((tq, width), lambda b, h, i: (b * nq + i, h)),
            pl.BlockSpec((tq, width), lambda b, h, i: (b * nq + i, h)),
            pl.BlockSpec((seq, width), lambda b, h, i: (b, h + ng)),
            pl.BlockSpec((seq, width), lambda b, h, i: (b, h)),
            pl.BlockSpec((seq, width), lambda b, h, i: (b, h + 2 * ng)),
            pl.BlockSpec((tq, width), lambda b, h, i: (b * nq + i, h + 3 * ng)),
        ],
        out_specs=pl.BlockSpec((tq, width), lambda b, h, i: (b * nq + i, h)),
        out_shape=jax.ShapeDtypeStruct((T, n_heads * LANES), BF16),
        scratch_shapes=[pltpu.VMEM(sh, dt) for sh, dt in scratch],
        compiler_params=_params(("parallel", "parallel", "arbitrary"),
                                _vmem_limit(blocks, scratch, [((tq, 4 * tq), F32)] * 2 * ATTN_HEADS_PER_STEP)),
        name="fox_attention",
    )(qkvg, qa, qkvg, ka, qkvg, qkvg)


def _mixer_ab(x2d, gain, w_in, j, sp_w, sp_b, v_gain, gamma, o_gain, w_out, layer, batch, seq):
    a_width = v_gain.shape[1]
    b_width = gamma.shape[1]
    y_a, h = _gmlp_proj(x2d, gain, w_in, j, v_gain, sp_w, sp_b, seq)
    qfig = _matmul(h, w_in, j, 2 * a_width, 4 * b_width, seq, F32, "hgrn_in_proj")
    y_b = _hgrn(qfig, gamma, o_gain, layer, batch, seq)
    return _matmul_residual([y_a, y_b], w_out, j, x2d, seq, "ab_out_proj", tm_pref=OUT_PROJ_ROWS)


def _mixer_c(x2d, gain, w_in, w_forget, j, b_f, q_gain, k_gain, w_out, batch, seq):
    D = x2d.shape[1]
    n_heads = b_f.shape[1]
    head_dim = q_gain.shape[1]
    assert head_dim == LANES and n_heads * head_dim == D
    scale = head_dim ** -0.5 * LOG2E
    head_gain = jnp.concatenate([jnp.tile(q_gain * scale, (1, n_heads)), jnp.tile(k_gain, (1, n_heads)),
                                 jnp.ones((1, 2 * D), F32)], axis=1)
    qkvg, h = _qkvg_proj(x2d, gain, w_in, j, head_gain, 4 * D, 2 * D, seq)
    assert 3 * n_heads <= LANES
    pad = ((0, 0), (0, LANES - 3 * n_heads))
    w_f = jnp.pad(jnp.tile(w_forget, (1, 3)), pad).astype(BF16)
    b_fp = jnp.pad(jnp.tile(b_f, (1, 3)), pad)
    qa, ka = _fgate(h, w_f, b_fp, n_heads, batch, seq)
    o = _attention(qkvg, qa, ka, n_heads, batch, seq)
    return _matmul_residual([o], w_out, j, x2d, seq, "attn_out_proj", tm_pref=OUT_PROJ_ROWS)


def kernel(x, mix_norm, ab_w_in, ab_sp_w, ab_sp_b, ab_v_norm, hgrn_gamma, hgrn_o_norm, ab_w_out,
           c_w_in, c_b_f, c_q_norm, c_k_norm, c_w_out, ffn_norm, ffn_w_up, ffn_conv_w, ffn_conv_b,
           ffn_w_down):
    batch, seq, D = x.shape
    depth = mix_norm.shape[0]
    x2d = x.reshape(batch * seq, D)
    ab_w_in_b = ab_w_in.astype(BF16)
    c_w_in_t = jnp.swapaxes(c_w_in, 1, 2).astype(BF16)
    ffn_w_down = ffn_w_down.astype(BF16)
    ab_w_out = ab_w_out.astype(BF16)
    c_w_out = c_w_out.astype(BF16)
    for l in range(depth):
        j = l // 2
        gain = mix_norm[l][None, :]
        if l % 2 == 0:
            x2d = _mixer_ab(x2d, gain, ab_w_in_b, j, ab_sp_w[j], ab_sp_b[j], ab_v_norm[j][None, :],
                            hgrn_gamma, hgrn_o_norm[j][None, :], ab_w_out, l, batch, seq)
        else:
            x2d = _mixer_c(x2d, gain, c_w_in_t, c_w_in[j, :, 4 * D:], j, c_b_f[j][None, :],
                           c_q_norm[j][None, :], c_k_norm[j][None, :], c_w_out, batch, seq)
        x2d = _conv_ffn(x2d, ffn_norm[l][None, :], ffn_w_up, l, ffn_conv_w[l],
                        ffn_conv_b[l][None, :], ffn_w_down, seq)
    return x2d.reshape(batch, seq, D)
```

```python
import functools
import math

import jax
import jax.numpy as jnp
import numpy as np
from jax import lax
from jax.experimental import pallas as pl
from jax.experimental.pallas import tpu as pltpu

F32 = jnp.float32
BF16 = jnp.bfloat16
RMS_EPS = 1e-6
LANES = 128
SUBLANES = 8
MXU_COLS = 256
HGRN_CHUNK = 64
HGRN_MAX_FACTORED_RANGE = 60.0
OUT_PROJ_ROWS = 2048
ATTN_BLOCK = 512
ATTN_HEADS_PER_STEP = 4
LOG2E = 1.4426950408889634
V7X_VMEM_BYTES = 64 * 1024 * 1024
VMEM_CAP_BYTES = V7X_VMEM_BYTES - 6 * 1024 * 1024


def _nbytes(shape, dtype):
    return math.prod(shape) * jnp.dtype(dtype).itemsize


def _vmem_limit(blocks, scratch=(), temps=()):
    est = 2 * sum(_nbytes(s, d) for s, d in blocks)
    est += sum(_nbytes(s, d) for s, d in scratch) + sum(_nbytes(s, d) for s, d in temps)
    return int(min(VMEM_CAP_BYTES, est * 5 // 4 + (4 << 20)))


def _tile(n, pref, mult=LANES):
    if n <= pref:
        return n
    t = (pref // mult) * mult
    while n % t:
        t -= mult
    return t


def _params(sem, limit):
    return pltpu.CompilerParams(dimension_semantics=sem, vmem_limit_bytes=limit)


def _rms_rows(xf, gain):
    ms = jnp.mean(xf * xf, axis=-1, keepdims=True)
    return xf * lax.rsqrt(ms + RMS_EPS) * gain


def _gelu(x):
    return 0.5 * x * (1.0 + lax.erf(x * (2.0 ** -0.5)))


def _fill_normed(x_ref, gain_ref, h_ref, rows=256):
    rows = min(rows, x_ref.shape[0])

    def body(r, carry):
        sl = pl.ds(pl.multiple_of(r * rows, rows), rows)
        h_ref[sl, :] = _rms_rows(x_ref[sl, :], gain_ref[...]).astype(h_ref.dtype)
        return carry

    lax.fori_loop(0, x_ref.shape[0] // rows, body, 0)


def _tril_mask(n):
    t = lax.broadcasted_iota(jnp.int32, (n, n), 0)
    s = lax.broadcasted_iota(jnp.int32, (n, n), 1)
    return s <= t


def _cumsum_rows(x, tril_bf16):
    hi = x.astype(BF16)
    r1 = x - hi.astype(F32)
    mid = r1.astype(BF16)
    lo = (r1 - mid.astype(F32)).astype(BF16)
    y = jnp.dot(tril_bf16, jnp.concatenate([hi, mid, lo], axis=1), preferred_element_type=F32)
    d = x.shape[1]
    return y[:, :d] + y[:, d:2 * d] + y[:, 2 * d:]


def _dot_nt(a, b):
    return lax.dot_general(a, b, (((1,), (1,)), ((), ())), preferred_element_type=F32)


def _dot_tn(a, b):
    return lax.dot_general(a, b, (((0,), (0,)), ((), ())), preferred_element_type=F32)


def _gmlp_kernel(x_ref, gain_ref, wu_ref, wv_ref, vg_ref, spw_ref, spb_ref, o_ref, h_ref, *, chunk):
    @pl.when(pl.program_id(1) == 0)
    def _():
        _fill_normed(x_ref, gain_ref, h_ref)

    h = h_ref[...]
    u = _gelu(jnp.dot(h, wu_ref[...].astype(BF16), preferred_element_type=F32))
    v = _gelu(jnp.dot(h, wv_ref[...].astype(BF16), preferred_element_type=F32))
    tm, tn = u.shape
    n_chunks = tm // chunk
    tril = _tril_mask(chunk)
    for hh in range(tn // LANES):
        lanes = slice(hh * LANES, (hh + 1) * LANES)
        vh = _rms_rows(v[:, lanes], vg_ref[:, lanes]).astype(BF16)
        vcat = jnp.concatenate([vh[c * chunk:(c + 1) * chunk, :] for c in range(n_chunks)], axis=1)
        w_causal = jnp.where(tril, spw_ref[hh], 0.0).astype(BF16)
        mixed = jnp.dot(w_causal, vcat, preferred_element_type=F32)
        for c in range(n_chunks):
            rows = slice(c * chunk, (c + 1) * chunk)
            m_c = mixed[:, c * LANES:(c + 1) * LANES] + spb_ref[hh]
            o_ref[rows, lanes] = (u[rows, lanes] * m_c).astype(o_ref.dtype)


def _gmlp_proj(x2d, gain, w_in, layer, v_gain, sp_w, sp_b, seq):
    T, D = x2d.shape
    n_heads, chunk, _ = sp_w.shape
    a_width = n_heads * LANES
    tm = _tile(seq, 1024, chunk)
    tn = _tile(a_width, 1024)
    nj = a_width // tn
    spb = jnp.broadcast_to(sp_b[:, :, None], (n_heads, chunk, LANES))
    blocks = [((tm, D), F32), ((1, D), F32), ((D, tn), w_in.dtype), ((D, tn), w_in.dtype), ((1, tn), F32),
              ((tn // LANES, chunk, chunk), F32), ((tn // LANES, chunk, LANES), F32), ((tm, tn), BF16)]
    return pl.pallas_call(
        functools.partial(_gmlp_kernel, chunk=chunk),
        grid=(T // tm, nj),
        in_specs=[
            pl.BlockSpec((tm, D), lambda i, j: (i, 0)),
            pl.BlockSpec((1, D), lambda i, j: (0, 0)),
            pl.BlockSpec((None, D, tn), lambda i, j: (layer, 0, j)),
            pl.BlockSpec((None, D, tn), lambda i, j: (layer, 0, j + nj)),
            pl.BlockSpec((1, tn), lambda i, j: (0, j)),
            pl.BlockSpec((tn // LANES, chunk, chunk), lambda i, j: (j, 0, 0)),
            pl.BlockSpec((tn // LANES, chunk, LANES), lambda i, j: (j, 0, 0)),
        ],
        out_specs=[pl.BlockSpec((tm, tn), lambda i, j: (i, j)), pl.BlockSpec((tm, D), lambda i, j: (i, 0))],
        out_shape=[jax.ShapeDtypeStruct((T, a_width), BF16), jax.ShapeDtypeStruct((T, D), BF16)],
        compiler_params=_params(("parallel", "arbitrary"),
                                _vmem_limit(blocks + [((tm, D), BF16)], [], [((tm, tn), F32)] * 6)),
        name="gmlp_proj",
    )(x2d, gain, w_in, w_in, v_gain, sp_w, spb)


def _mm_kernel(h_ref, w_ref, o_ref):
    o_ref[...] = jnp.dot(h_ref[...], w_ref[...].astype(BF16), preferred_element_type=F32).astype(o_ref.dtype)


def _matmul(h, w, layer, col0, n_cols, seq, out_dtype, name):
    T, D = h.shape
    tm = _tile(seq, 1024)
    tn = _tile(n_cols, 2048)
    assert col0 % tn == 0
    j0 = col0 // tn
    blocks = [((tm, D), BF16), ((D, tn), w.dtype), ((tm, tn), out_dtype)]
    return pl.pallas_call(
        _mm_kernel,
        grid=(T // tm, n_cols // tn),
        in_specs=[
            pl.BlockSpec((tm, D), lambda i, j: (i, 0)),
            pl.BlockSpec((None, D, tn), lambda i, j: (layer, 0, j + j0)),
        ],
        out_specs=pl.BlockSpec((tm, tn), lambda i, j: (i, j)),
        out_shape=jax.ShapeDtypeStruct((T, n_cols), out_dtype),
        compiler_params=_params(("parallel", "arbitrary"), _vmem_limit(blocks, [], [((tm, tn), F32)])),
        name=name,
    )(h, w)


def _hgrn_kernel(gamma_ref, q_ref, f_ref, i_ref, g_ref, og_ref, o_ref, st_ref, g_scr, k_scr, intra_scr, inter_scr,
                 *, layer, chunk):
    @pl.when(pl.program_id(2) == 0)
    def _():
        st_ref[...] = jnp.zeros_like(st_ref)

    gam = gamma_ref[...]
    ex = jnp.exp(gam - jnp.max(gam, axis=0, keepdims=True))
    lb_all = jnp.sum(ex[:layer + 1], axis=0, keepdims=True) / jnp.sum(ex, axis=0, keepdims=True)

    L, width = q_ref.shape
    n_chunks = L // chunk
    mid = chunk // 2 - 1
    row_chunk = lax.broadcasted_iota(jnp.int32, (L, L), 0) // chunk
    col_chunk = lax.broadcasted_iota(jnp.int32, (L, L), 1) // chunk
    mask = _tril_mask(L) & (row_chunk == col_chunk)
    chunk_of_row = lax.broadcasted_iota(jnp.int32, (L, LANES), 0) // chunk

    fl = f_ref[...]
    e = jnp.exp(-jnp.abs(fl))
    r = 1.0 / (1.0 + e)
    pos = fl >= 0
    sig = jnp.where(pos, r, e * r)
    nsig = jnp.where(pos, e * r, r)
    kk = (1.0 - lb_all) * nsig
    G = _cumsum_rows(jnp.log(lb_all + (1.0 - lb_all) * sig), mask.astype(BF16))
    g_scr[...] = G
    k_scr[...] = kk
    g_mid_rows = [G[c * chunk + mid:c * chunk + mid + 1, :] for c in range(n_chunks)]
    g_end_rows = [G[(c + 1) * chunk - 1:(c + 1) * chunk, :] for c in range(n_chunks)]
    per_row = lambda rows: jnp.concatenate([jnp.broadcast_to(x, (chunk, width)) for x in rows], axis=0)
    g_mid = per_row(g_mid_rows)
    g_end = per_row(g_end_rows)
    qv = q_ref[...]
    q_t = (qv * jnp.exp(G - g_mid)).astype(BF16)
    k_t = (kk * jnp.exp(g_mid - G)).astype(BF16)
    q_g = (qv * jnp.exp(G)).astype(BF16)
    k_end = (kk * jnp.exp(g_end - G)).astype(BF16)
    vv = i_ref[...].astype(BF16)
    decay_range = -jnp.min(jnp.concatenate(g_end_rows, axis=0))
    factorable = decay_range <= HGRN_MAX_FACTORED_RANGE

    def finish(o, lanes):
        gv = g_ref[:, lanes]
        o_ref[:, lanes] = (_rms_rows(o, og_ref[...]) * (gv * jax.nn.sigmoid(gv))).astype(o_ref.dtype)

    for hh in range(width // LANES):
        lanes = slice(hh * LANES, (hh + 1) * LANES)
        scores = jnp.where(mask, _dot_nt(q_t[:, lanes], k_t[:, lanes]), 0.0).astype(BF16)
        intra = jnp.dot(scores, vv[:, lanes], preferred_element_type=F32)
        v_exp = jnp.concatenate([jnp.where(chunk_of_row == c, vv[:, lanes], 0) for c in range(n_chunks)],
                                axis=1)
        u_t = _dot_tn(v_exp, k_end[:, lanes])
        st = st_ref[hh]
        prev = []
        for c in range(n_chunks):
            prev.append(st.astype(BF16))
            st = st * jnp.exp(g_end_rows[c][:, lanes]) + u_t[c * LANES:(c + 1) * LANES, :]
        st_ref[hh] = st
        q_exp = jnp.concatenate([jnp.where(chunk_of_row == c, q_g[:, lanes], 0) for c in range(n_chunks)],
                                axis=1)
        inter = _dot_nt(q_exp, jnp.concatenate(prev, axis=1))
        inter_scr[hh] = inter
        finish(intra + inter, lanes)

    @pl.when(jnp.logical_not(factorable))
    def _():
        for hh in range(width // LANES):
            _hgrn_intra_pairwise(q_ref, i_ref, g_scr, k_scr, intra_scr, hh, chunk)
            finish(intra_scr[hh] + inter_scr[hh], slice(hh * LANES, (hh + 1) * LANES))


def _hgrn_intra_pairwise(q_ref, i_ref, g_scr, k_scr, intra_scr, hh, chunk):
    lanes = slice(hh * LANES, (hh + 1) * LANES)
    t_idx = lax.broadcasted_iota(jnp.int32, (chunk, 1), 0)
    for c in range(q_ref.shape[0] // chunk):
        rows = slice(c * chunk, (c + 1) * chunk)
        g_c = g_scr[rows, lanes]
        q_c = q_ref[rows, lanes]

        def body(grp, acc):
            keys = pl.ds(pl.multiple_of(c * chunk + grp * SUBLANES, SUBLANES), SUBLANES)
            g_s, k_s, v_s = g_scr[keys, lanes], k_scr[keys, lanes], i_ref[keys, lanes]
            for j in range(SUBLANES):
                w = q_c * k_s[j:j + 1] * jnp.exp(jnp.minimum(g_c - g_s[j:j + 1], 0.0))
                score = jnp.where(t_idx >= grp * SUBLANES + j, jnp.sum(w, axis=1, keepdims=True), 0.0)
                acc = acc + score * v_s[j:j + 1]
            return acc

        intra_scr[hh, rows, :] = lax.fori_loop(0, chunk // SUBLANES, body, jnp.zeros((chunk, LANES), F32))


def _hgrn(qfig, gamma, o_gain, layer, batch, seq):
    T = qfig.shape[0]
    b_width = gamma.shape[1]
    n_layers = gamma.shape[0]
    width = _tile(b_width, 1024)
    L = _tile(seq, 256, HGRN_CHUNK)
    nw = b_width // width
    nl = seq // L
    blocks = [((L, width), F32)] * 4 + [((L, width), BF16), ((n_layers, width), F32)]
    n_sub = width // LANES
    scratch = [((n_sub, LANES, LANES), F32), ((L, width), F32), ((L, width), F32), ((n_sub, L, LANES), F32),
               ((n_sub, L, LANES), F32)]
    row = lambda b, h, l: b * nl + l
    return pl.pallas_call(
        functools.partial(_hgrn_kernel, layer=layer, chunk=HGRN_CHUNK),
        grid=(batch, nw, nl),
        in_specs=[
            pl.BlockSpec((n_layers, width), lambda b, h, l: (0, h)),
            pl.BlockSpec((L, width), lambda b, h, l: (row(b, h, l), h)),
            pl.BlockSpec((L, width), lambda b, h, l: (row(b, h, l), h + nw)),
            pl.BlockSpec((L, width), lambda b, h, l: (row(b, h, l), h + 2 * nw)),
            pl.BlockSpec((L, width), lambda b, h, l: (row(b, h, l), h + 3 * nw)),
            pl.BlockSpec((1, LANES), lambda b, h, l: (0, 0)),
        ],
        out_specs=pl.BlockSpec((L, width), lambda b, h, l: (row(b, h, l), h)),
        out_shape=jax.ShapeDtypeStruct((T, b_width), BF16),
        scratch_shapes=[pltpu.VMEM(sh, dt) for sh, dt in scratch],
        compiler_params=_params(("parallel", "parallel", "arbitrary"),
                                _vmem_limit(blocks, scratch, [((L, width), F32)] * 12)),
        name="hgrn2",
    )(gamma, qfig, qfig, qfig, qfig, o_gain)


def _mm_resid_kernel(*refs, n_lhs):
    lhs_refs = refs[:n_lhs]
    w_ref, r_ref, o_ref = refs[n_lhs:]
    acc = r_ref[...]
    k0 = 0
    for a_ref in lhs_refs:
        k = a_ref.shape[1]
        acc = acc + jnp.dot(a_ref[...], w_ref[k0:k0 + k, :].astype(BF16), preferred_element_type=F32)
        k0 += k
    o_ref[...] = acc


def _matmul_residual(lhs_list, w, layer, resid, seq, name, tm_pref=1024, tn_pref=512):
    T, N = resid.shape
    K = w.shape[1]
    tm = _tile(seq, tm_pref)
    tn = _tile(N, tn_pref)
    blocks = [((tm, a.shape[1]), BF16) for a in lhs_list] + [((K, tn), w.dtype), ((tm, tn), F32), ((tm, tn), F32)]
    return pl.pallas_call(
        functools.partial(_mm_resid_kernel, n_lhs=len(lhs_list)),
        grid=(T // tm, N // tn),
        in_specs=[pl.BlockSpec((tm, a.shape[1]), lambda i, j: (i, 0)) for a in lhs_list] + [
            pl.BlockSpec((None, K, tn), lambda i, j: (layer, 0, j)),
            pl.BlockSpec((tm, tn), lambda i, j: (i, j)),
        ],
        out_specs=pl.BlockSpec((tm, tn), lambda i, j: (i, j)),
        out_shape=jax.ShapeDtypeStruct((T, N), F32),
        compiler_params=_params(("parallel", "arbitrary"), _vmem_limit(blocks, [], [((tm, tn), F32)])),
        name=name,
    )(*lhs_list, w, resid)


def _ffn_up_kernel(x_ref, xh_ref, gain_ref, wa_ref, wb_ref, cwa_ref, cwb_ref, cba_ref, cbb_ref,
                   o_ref, h_ref, za_ref, *, n_tiles, n_col_tiles, tiles_per_seq, halo):
    s = pl.program_id(0)
    cur = jnp.minimum(s, n_tiles - 1)
    prev = jnp.maximum(s - 1, 0)
    row_c = cur // n_col_tiles
    tm = x_ref.shape[0]

    @pl.when(s == 0)
    def _():
        za_ref[...] = jnp.zeros_like(za_ref)

    @pl.when((cur % n_col_tiles == 0) & (s < n_tiles))
    def _():
        slot = row_c % 2
        keep = (row_c % tiles_per_seq != 0).astype(F32)
        h_ref[slot, 0:halo, :] = (_rms_rows(xh_ref[...], gain_ref[...]) * keep).astype(h_ref.dtype)
        rows = min(256, tm)

        def body(r, carry):
            src = pl.ds(pl.multiple_of(r * rows, rows), rows)
            dst = pl.ds(pl.multiple_of(r * rows + halo, halo), rows)
            h_ref[slot, dst, :] = _rms_rows(x_ref[src, :], gain_ref[...]).astype(h_ref.dtype)
            return carry

        lax.fori_loop(0, tm // rows, body, 0)

    def conv(z, cw_ref, cb_ref):
        z1 = pltpu.roll(z, 1, 0)
        z2 = pltpu.roll(z, 2, 0)
        y = cw_ref[0:1, :] * z2 + cw_ref[1:2, :] * z1 + cw_ref[2:3, :] * z + cb_ref[...]
        return y[halo:, :]

    h_prev = h_ref[(prev // n_col_tiles) % 2]
    zb = jnp.dot(h_prev, wb_ref[...].astype(BF16), preferred_element_type=F32)
    a = conv(za_ref[...], cwa_ref, cba_ref)
    o_ref[...] = (a * jax.nn.sigmoid(a) * conv(zb, cwb_ref, cbb_ref)).astype(o_ref.dtype)
    za_ref[...] = jnp.dot(h_ref[row_c % 2], wa_ref[...].astype(BF16), preferred_element_type=F32)


def _ffn_up(x2d, gain, w_up, layer, conv_w, conv_b, seq):
    T, D = x2d.shape
    F = w_up.shape[2] // 2
    halo = 16
    tm = _tile(seq, 1024)
    tn = _tile(F, 512, MXU_COLS)
    nj = F // tn
    n_tiles = (T // tm) * nj
    tiles_per_seq = seq // tm
    blocks = [((tm, D), F32), ((halo, D), F32), ((1, D), F32), ((D, tn), F32), ((D, tn), F32),
              ((3, tn), F32), ((3, tn), F32), ((1, tn), F32), ((1, tn), F32), ((tm, tn), BF16)]
    scratch = [((2, tm + halo, D), BF16), ((tm + halo, tn), F32)]
    hpt = tm // halo
    cur = lambda s: jnp.minimum(s, n_tiles - 1)
    prev = lambda s: jnp.maximum(s - 1, 0)
    return pl.pallas_call(
        functools.partial(_ffn_up_kernel, n_tiles=n_tiles, n_col_tiles=nj, tiles_per_seq=tiles_per_seq,
                          halo=halo),
        grid=(n_tiles + 1,),
        in_specs=[
            pl.BlockSpec((tm, D), lambda s: (cur(s) // nj, 0)),
            pl.BlockSpec((halo, D), lambda s: (jnp.maximum(cur(s) // nj * hpt - 1, 0), 0)),
            pl.BlockSpec((1, D), lambda s: (0, 0)),
            pl.BlockSpec((None, D, tn), lambda s: (layer, 0, cur(s) % nj)),
            pl.BlockSpec((None, D, tn), lambda s: (layer, 0, prev(s) % nj + nj)),
            pl.BlockSpec((3, tn), lambda s: (0, prev(s) % nj)),
            pl.BlockSpec((3, tn), lambda s: (0, prev(s) % nj + nj)),
            pl.BlockSpec((1, tn), lambda s: (0, prev(s) % nj)),
            pl.BlockSpec((1, tn), lambda s: (0, prev(s) % nj + nj)),
        ],
        out_specs=pl.BlockSpec((tm, tn), lambda s: (prev(s) // nj, prev(s) % nj)),
        out_shape=jax.ShapeDtypeStruct((T, F), BF16),
        scratch_shapes=[pltpu.VMEM(sh, dt) for sh, dt in scratch],
        compiler_params=_params(("arbitrary",),
                                _vmem_limit(blocks, scratch, [((tm + halo, MXU_COLS), F32)] * 8)),
        name="ffn_up_conv_gate",
    )(x2d, x2d, gain, w_up, w_up, conv_w, conv_w, conv_b, conv_b)


def _conv_ffn(x2d, gain, w_up, layer, conv_w, conv_b, w_down, seq):
    act = _ffn_up(x2d, gain, w_up, layer, conv_w, conv_b, seq)
    return _matmul_residual([act], w_down, layer, x2d, seq, "ffn_down", tn_pref=512)


def _qkvg_kernel(x_ref, gain_ref, w_ref, hg_ref, o_ref, h_ref, *, n_norm_tiles):
    j = pl.program_id(1)
    n_sub = o_ref.shape[1] // MXU_COLS

    @pl.when(j == 0)
    def _():
        _fill_normed(x_ref, gain_ref, h_ref)

    @pl.when(j < n_norm_tiles)
    def _():
        h = h_ref[...]
        for c in range(n_sub):
            y = _dot_nt(h, w_ref[c * MXU_COLS:(c + 1) * MXU_COLS, :])
            for grp in range(MXU_COLS // LANES):
                src = slice(grp * LANES, (grp + 1) * LANES)
                dst = slice(c * MXU_COLS + grp * LANES, c * MXU_COLS + (grp + 1) * LANES)
                o_ref[:, dst] = _rms_rows(y[:, src], hg_ref[:, dst]).astype(o_ref.dtype)

    @pl.when(j >= n_norm_tiles)
    def _():
        h = h_ref[...]
        for c in range(n_sub):
            cols = slice(c * MXU_COLS, (c + 1) * MXU_COLS)
            o_ref[:, cols] = _dot_nt(h, w_ref[cols, :]).astype(o_ref.dtype)


def _qkvg_proj(x2d, gain, w, layer, head_gain, n_cols, n_norm_cols, seq):
    T, D = x2d.shape
    tm = _tile(seq, 1024)
    tn = _tile(n_norm_cols // 2, 2048, MXU_COLS)
    blocks = [((tm, D), F32), ((1, D), F32), ((D, tn), w.dtype), ((1, tn), F32), ((tm, tn), BF16)]
    return pl.pallas_call(
        functools.partial(_qkvg_kernel, n_norm_tiles=n_norm_cols // tn),
        grid=(T // tm, n_cols // tn),
        in_specs=[
            pl.BlockSpec((tm, D), lambda i, j: (i, 0)),
            pl.BlockSpec((1, D), lambda i, j: (0, 0)),
            pl.BlockSpec((None, tn, D), lambda i, j: (layer, j, 0)),
            pl.BlockSpec((1, tn), lambda i, j: (0, j)),
        ],
        out_specs=[pl.BlockSpec((tm, tn), lambda i, j: (i, j)), pl.BlockSpec((tm, D), lambda i, j: (i, 0))],
        out_shape=[jax.ShapeDtypeStruct((T, n_cols), BF16), jax.ShapeDtypeStruct((T, D), BF16)],
        compiler_params=_params(("parallel", "arbitrary"),
                                _vmem_limit(blocks + [((tm, D), BF16)], [], [((tm, MXU_COLS), F32)] * 4)),
        name="qkvg_proj",
    )(x2d, gain, w, head_gain)


def _fgate_kernel(h_ref, wf_ref, bf_ref, pq_ref, pk_ref, cq_ref, ck_ref, qa_ref, ka_ref,
                  carry_ref, *, n_heads):
    @pl.when(pl.program_id(1) == 0)
    def _():
        carry_ref[...] = jnp.zeros_like(carry_ref)

    tm = h_ref.shape[0]
    f = jnp.dot(h_ref[...], wf_ref[...], preferred_element_type=F32) + bf_ref[...]
    log_f = (jnp.minimum(f, 0.0) - jnp.log1p(jnp.exp(-jnp.abs(f)))) * LOG2E
    c = _cumsum_rows(log_f, _tril_mask(tm).astype(BF16)) + carry_ref[...]
    carry_ref[...] = c[tm - 1:tm, :]
    hi = c.astype(BF16)
    r1 = c - hi.astype(F32)
    mid = r1.astype(BF16)
    lo = (r1 - mid.astype(F32)).astype(BF16)
    lane = lax.broadcasted_iota(jnp.int32, c.shape, 1)
    parts = jnp.where(lane < n_heads, hi, jnp.where(lane < 2 * n_heads, mid, lo))
    qa_ref[...] = (jnp.dot(parts, pq_ref[...], preferred_element_type=F32) + cq_ref[...]).astype(qa_ref.dtype)
    ka_ref[...] = (jnp.dot(parts, pk_ref[...], preferred_element_type=F32) + ck_ref[...]).astype(ka_ref.dtype)


def _fgate_selectors(n_heads):
    W = n_heads * LANES
    pq = np.zeros((LANES, W), np.float32)
    pk = np.zeros((LANES, W), np.float32)
    cq = np.zeros((1, W), np.float32)
    ck = np.zeros((1, W), np.float32)
    for hd in range(n_heads):
        for part in range(3):
            pq[part * n_heads + hd, hd * LANES + part] = 1.0
            pk[part * n_heads + hd, hd * LANES + 3 + part] = -1.0
            cq[0, hd * LANES + 3 + part] = 1.0
            ck[0, hd * LANES + part] = 1.0
    return jnp.asarray(pq, BF16), jnp.asarray(pk, BF16), jnp.asarray(cq), jnp.asarray(ck)


def _fgate(h, w_f, b_f, n_heads, batch, seq):
    T, D = h.shape
    tm = _tile(seq, 512)
    nt = seq // tm
    W = n_heads * LANES
    pq, pk, cq, ck = _fgate_selectors(n_heads)
    blocks = [((tm, D), BF16), ((D, LANES), BF16), ((1, LANES), F32),
              ((LANES, W), BF16), ((LANES, W), BF16), ((1, W), F32), ((1, W), F32),
              ((tm, W), BF16), ((tm, W), BF16)]
    const = lambda b, i: (0, 0)
    return pl.pallas_call(
        functools.partial(_fgate_kernel, n_heads=n_heads),
        grid=(batch, nt),
        in_specs=[
            pl.BlockSpec((tm, D), lambda b, i: (b * nt + i, 0)),
            pl.BlockSpec((D, LANES), const),
            pl.BlockSpec((1, LANES), const),
            pl.BlockSpec((LANES, W), const),
            pl.BlockSpec((LANES, W), const),
            pl.BlockSpec((1, W), const),
            pl.BlockSpec((1, W), const),
        ],
        out_specs=[pl.BlockSpec((tm, W), lambda b, i: (b * nt + i, 0)),
                   pl.BlockSpec((tm, W), lambda b, i: (b * nt + i, 0))],
        out_shape=[jax.ShapeDtypeStruct((T, W), BF16), jax.ShapeDtypeStruct((T, W), BF16)],
        scratch_shapes=[pltpu.VMEM((1, LANES), F32)],
        compiler_params=_params(("parallel", "arbitrary"),
                                _vmem_limit(blocks, [], [((tm, D), F32)] * 2 + [((tm, tm), BF16)])),
        name="fox_forget_cumsum",
    )(h, w_f, b_f, pq, pk, cq, ck)


def _attn_kernel(q_ref, qa_ref, k_ref, ka_ref, v_ref, g_ref, o_ref, qs_ref, acc_ref, m_ref, l_ref):
    qi = pl.program_id(2)
    tq = tk = q_ref.shape[0]
    n_sub = q_ref.shape[1] // LANES
    for hh in range(n_sub):
        lanes = slice(hh * LANES, (hh + 1) * LANES)
        qs_ref[hh] = jnp.concatenate([q_ref[:, lanes], qa_ref[:, lanes]], axis=1)
    def step(first_key, n_keys, diag_offset=None):
        rows = pl.ds(pl.multiple_of(first_key, tk), n_keys)
        for hh in range(n_sub):
            lanes = slice(hh * LANES, (hh + 1) * LANES)
            k = jnp.concatenate([k_ref[rows, lanes], ka_ref[rows, lanes]], axis=1)
            s = _dot_nt(qs_ref[hh], k)
            if diag_offset is not None:
                t_pos = lax.broadcasted_iota(jnp.int32, (tq, n_keys), 0)
                s_pos = lax.broadcasted_iota(jnp.int32, (tq, n_keys), 1) + diag_offset
                s = jnp.where(s_pos <= t_pos, s, -jnp.inf)
                m_new = jnp.broadcast_to(jnp.max(s, axis=1, keepdims=True), (tq, LANES))
            else:
                m_prev = m_ref[hh]
                m_new = jnp.maximum(m_prev, jnp.max(s, axis=1, keepdims=True))
                alpha = jnp.exp2(m_prev - m_new)
            p = jnp.exp2(s - jnp.tile(m_new, (1, n_keys // LANES)))
            l_new = jnp.broadcast_to(jnp.sum(p, axis=1, keepdims=True), (tq, LANES))
            pv = jnp.dot(p.astype(BF16), v_ref[rows, lanes], preferred_element_type=F32)
            m_ref[hh] = m_new
            if diag_offset is not None:
                l_ref[hh], acc_ref[hh] = l_new, pv
            else:
                l_ref[hh] = alpha * l_ref[hh] + l_new
                acc_ref[hh] = alpha * acc_ref[hh] + pv

    def loop(n, body):
        lax.fori_loop(0, n, lambda j, c: (body(j), c)[1], 0)

    @pl.when(qi % 2 == 1)
    def _():
        step((qi - 1) * tk, 2 * tk, -tk)

    @pl.when(qi % 2 == 0)
    def _():
        step(qi * tk, tk, 0)

    n_blocks = k_ref.shape[0] // tk
    if n_blocks > 4:
        loop(qi // 4, lambda j: step(j * 4 * tk, 4 * tk))
    if n_blocks > 2:
        loop((qi % 4) // 2, lambda j: step((qi // 4) * 4 * tk, 2 * tk))
    for hh in range(n_sub):
        lanes = slice(hh * LANES, (hh + 1) * LANES)
        gate = jax.nn.sigmoid(g_ref[:, lanes].astype(F32))
        o_ref[:, lanes] = (acc_ref[hh] / l_ref[hh] * gate).astype(o_ref.dtype)


def _attention(qkvg, qa, ka, n_heads, batch, seq):
    T = qkvg.shape[0]
    tq = _tile(seq, ATTN_BLOCK)
    nq = seq // tq
    width = ATTN_HEADS_PER_STEP * LANES
    ng = n_heads * LANES // width
    blocks = [((tq, width), BF16)] * 2 + [((seq, width), BF16)] * 3 + [((tq, width), BF16)] * 2
    scratch = [((width // LANES, tq, 2 * LANES), BF16)] + [((width // LANES, tq, LANES), F32)] * 3
    return pl.pallas_call(
        _attn_kernel,
        grid=(batch, ng, nq),
        in_specs=[
            pl.BlockSpec((tq, width), lambda b, h, i: (b * nq + i, h)),
            pl.BlockSpec((tq, width), lambda b, h, i: (b * nq + i, h)),
            pl.BlockSpec((seq, width), lambda b, h, i: (b, h + ng)),
            pl.BlockSpec((seq, width), lambda b, h, i: (b, h)),
            pl.BlockSpec((seq, width), lambda b, h, i: (b, h + 2 * ng)),
            pl.BlockSpec((tq, width), lambda b, h, i: (b * nq + i, h + 3 * ng)),
        ],
        out_specs=pl.BlockSpec((tq, width), lambda b, h, i: (b * nq + i, h)),
        out_shape=jax.ShapeDtypeStruct((T, n_heads * LANES), BF16),
        scratch_shapes=[pltpu.VMEM(sh, dt) for sh, dt in scratch],
        compiler_params=_params(("parallel", "parallel", "arbitrary"),
                                _vmem_limit(blocks, scratch, [((tq, 4 * tq), F32)] * 2 * ATTN_HEADS_PER_STEP)),
        name="fox_attention",
    )(qkvg, qa, qkvg, ka, qkvg, qkvg)


def _mixer_ab(x2d, gain, w_in, j, sp_w, sp_b, v_gain, gamma, o_gain, w_out, layer, batch, seq):
    a_width = v_gain.shape[1]
    b_width = gamma.shape[1]
    y_a, h = _gmlp_proj(x2d, gain, w_in, j, v_gain, sp_w, sp_b, seq)
    qfig = _matmul(h, w_in, j, 2 * a_width, 4 * b_width, seq, F32, "hgrn_in_proj")
    y_b = _hgrn(qfig, gamma, o_gain, layer, batch, seq)
    return _matmul_residual([y_a, y_b], w_out, j, x2d, seq, "ab_out_proj", tm_pref=OUT_PROJ_ROWS)


def _mixer_c(x2d, gain, w_in, w_forget, j, b_f, q_gain, k_gain, w_out, batch, seq):
    D = x2d.shape[1]
    n_heads = b_f.shape[1]
    head_dim = q_gain.shape[1]
    assert head_dim == LANES and n_heads * head_dim == D
    scale = head_dim ** -0.5 * LOG2E
    head_gain = jnp.concatenate([jnp.tile(q_gain * scale, (1, n_heads)), jnp.tile(k_gain, (1, n_heads)),
                                 jnp.ones((1, 2 * D), F32)], axis=1)
    qkvg, h = _qkvg_proj(x2d, gain, w_in, j, head_gain, 4 * D, 2 * D, seq)
    assert 3 * n_heads <= LANES
    pad = ((0, 0), (0, LANES - 3 * n_heads))
    w_f = jnp.pad(jnp.tile(w_forget, (1, 3)), pad).astype(BF16)
    b_fp = jnp.pad(jnp.tile(b_f, (1, 3)), pad)
    qa, ka = _fgate(h, w_f, b_fp, n_heads, batch, seq)
    o = _attention(qkvg, qa, ka, n_heads, batch, seq)
    return _matmul_residual([o], w_out, j, x2d, seq, "attn_out_proj", tm_pref=OUT_PROJ_ROWS)


def kernel(x, mix_norm, ab_w_in, ab_sp_w, ab_sp_b, ab_v_norm, hgrn_gamma, hgrn_o_norm, ab_w_out,
           c_w_in, c_b_f, c_q_norm, c_k_norm, c_w_out, ffn_norm, ffn_w_up, ffn_conv_w, ffn_conv_b,
           ffn_w_down):
    batch, seq, D = x.shape
    depth = mix_norm.shape[0]
    x2d = x.reshape(batch * seq, D)
    ab_w_in_b = ab_w_in.astype(BF16)
    c_w_in_t = jnp.swapaxes(c_w_in, 1, 2).astype(BF16)
    ffn_w_down = ffn_w_down.astype(BF16)
    ab_w_out = ab_w_out.astype(BF16)
    c_w_out = c_w_out.astype(BF16)
    for l in range(depth):
        j = l // 2
        gain = mix_norm[l][None, :]
        if l % 2 == 0:
            x2d = _mixer_ab(x2d, gain, ab_w_in_b, j, ab_sp_w[j], ab_sp_b[j], ab_v_norm[j][None, :],
                            hgrn_gamma, hgrn_o_norm[j][None, :], ab_w_out, l, batch, seq)
        else:
            x2d = _mixer_c(x2d, gain, c_w_in_t, c_w_in[j, :, 4 * D:], j, c_b_f[j][None, :],
                           c_q_norm[j][None, :], c_k_norm[j][None, :], c_w_out, batch, seq)
        x2d = _conv_ffn(x2d, ffn_norm[l][None, :], ffn_w_up, l, ffn_conv_w[l],
                        ffn_conv_b[l][None, :], ffn_w_down, seq)
    return x2d.reshape(batch, seq, D)
```

```python
import functools
import math

import jax
import jax.numpy as jnp
import numpy as np
from jax import lax
from jax.experimental import pallas as pl
from jax.experimental.pallas import tpu as pltpu

F32 = jnp.float32
BF16 = jnp.bfloat16
RMS_EPS = 1e-6
LANES = 128
SUBLANES = 8
MXU_COLS = 256
HGRN_CHUNK = 64
HGRN_MAX_FACTORED_RANGE = 60.0
OUT_PROJ_ROWS = 2048
ATTN_BLOCK = 512
ATTN_HEADS_PER_STEP = 4
LOG2E = 1.4426950408889634
V7X_VMEM_BYTES = 64 * 1024 * 1024
VMEM_CAP_BYTES = V7X_VMEM_BYTES - 6 * 1024 * 1024


def _nbytes(shape, dtype):
    return math.prod(shape) * jnp.dtype(dtype).itemsize


def _vmem_limit(blocks, scratch=(), temps=()):
    est = 2 * sum(_nbytes(s, d) for s, d in blocks)
    est += sum(_nbytes(s, d) for s, d in scratch) + sum(_nbytes(s, d) for s, d in temps)
    return int(min(VMEM_CAP_BYTES, est * 5 // 4 + (4 << 20)))


def _tile(n, pref, mult=LANES):
    if n <= pref:
        return n
    t = (pref // mult) * mult
    while n % t:
        t -= mult
    return t


def _params(sem, limit):
    return pltpu.CompilerParams(dimension_semantics=sem, vmem_limit_bytes=limit)


def _rms_rows(xf, gain):
    ms = jnp.mean(xf * xf, axis=-1, keepdims=True)
    return xf * lax.rsqrt(ms + RMS_EPS) * gain


def _gelu(x):
    return 0.5 * x * (1.0 + lax.erf(x * (2.0 ** -0.5)))


def _fill_normed(x_ref, gain_ref, h_ref, rows=256):
    rows = min(rows, x_ref.shape[0])

    def body(r, carry):
        sl = pl.ds(pl.multiple_of(r * rows, rows), rows)
        h_ref[sl, :] = _rms_rows(x_ref[sl, :], gain_ref[...]).astype(h_ref.dtype)
        return carry

    lax.fori_loop(0, x_ref.shape[0] // rows, body, 0)


def _tril_mask(n):
    t = lax.broadcasted_iota(jnp.int32, (n, n), 0)
    s = lax.broadcasted_iota(jnp.int32, (n, n), 1)
    return s <= t


def _cumsum_rows(x, tril_bf16):
    hi = x.astype(BF16)
    r1 = x - hi.astype(F32)
    mid = r1.astype(BF16)
    lo = (r1 - mid.astype(F32)).astype(BF16)
    y = jnp.dot(tril_bf16, jnp.concatenate([hi, mid, lo], axis=1), preferred_element_type=F32)
    d = x.shape[1]
    return y[:, :d] + y[:, d:2 * d] + y[:, 2 * d:]


def _dot_nt(a, b):
    return lax.dot_general(a, b, (((1,), (1,)), ((), ())), preferred_element_type=F32)


def _dot_tn(a, b):
    return lax.dot_general(a, b, (((0,), (0,)), ((), ())), preferred_element_type=F32)


def _gmlp_kernel(x_ref, gain_ref, wu_ref, wv_ref, vg_ref, spw_ref, spb_ref, o_ref, h_ref, *, chunk):
    @pl.when(pl.program_id(1) == 0)
    def _():
        _fill_normed(x_ref, gain_ref, h_ref)

    h = h_ref[...]
    u = _gelu(jnp.dot(h, wu_ref[...].astype(BF16), preferred_element_type=F32))
    v = _gelu(jnp.dot(h, wv_ref[...].astype(BF16), preferred_element_type=F32))
    tm, tn = u.shape
    n_chunks = tm // chunk
    tril = _tril_mask(chunk)
    for hh in range(tn // LANES):
        lanes = slice(hh * LANES, (hh + 1) * LANES)
        vh = _rms_rows(v[:, lanes], vg_ref[:, lanes]).astype(BF16)
        vcat = jnp.concatenate([vh[c * chunk:(c + 1) * chunk, :] for c in range(n_chunks)], axis=1)
        w_causal = jnp.where(tril, spw_ref[hh], 0.0).astype(BF16)
        mixed = jnp.dot(w_causal, vcat, preferred_element_type=F32)
        for c in range(n_chunks):
            rows = slice(c * chunk, (c + 1) * chunk)
            m_c = mixed[:, c * LANES:(c + 1) * LANES] + spb_ref[hh]
            o_ref[rows, lanes] = (u[rows, lanes] * m_c).astype(o_ref.dtype)


def _gmlp_proj(x2d, gain, w_in, layer, v_gain, sp_w, sp_b, seq):
    T, D = x2d.shape
    n_heads, chunk, _ = sp_w.shape
    a_width = n_heads * LANES
    tm = _tile(seq, 1024, chunk)
    tn = _tile(a_width, 1024)
    nj = a_width // tn
    spb = jnp.broadcast_to(sp_b[:, :, None], (n_heads, chunk, LANES))
    blocks = [((tm, D), F32), ((1, D), F32), ((D, tn), w_in.dtype), ((D, tn), w_in.dtype), ((1, tn), F32),
              ((tn // LANES, chunk, chunk), F32), ((tn // LANES, chunk, LANES), F32), ((tm, tn), BF16)]
    return pl.pallas_call(
        functools.partial(_gmlp_kernel, chunk=chunk),
        grid=(T // tm, nj),
        in_specs=[
            pl.BlockSpec((tm, D), lambda i, j: (i, 0)),
            pl.BlockSpec((1, D), lambda i, j: (0, 0)),
            pl.BlockSpec((None, D, tn), lambda i, j: (layer, 0, j)),
            pl.BlockSpec((None, D, tn), lambda i, j: (layer, 0, j + nj)),
            pl.BlockSpec((1, tn), lambda i, j: (0, j)),
            pl.BlockSpec((tn // LANES, chunk, chunk), lambda i, j: (j, 0, 0)),
            pl.BlockSpec((tn // LANES, chunk, LANES), lambda i, j: (j, 0, 0)),
        ],
        out_specs=[pl.BlockSpec((tm, tn), lambda i, j: (i, j)), pl.BlockSpec((tm, D), lambda i, j: (i, 0))],
        out_shape=[jax.ShapeDtypeStruct((T, a_width), BF16), jax.ShapeDtypeStruct((T, D), BF16)],
        compiler_params=_params(("parallel", "arbitrary"),
                                _vmem_limit(blocks + [((tm, D), BF16)], [], [((tm, tn), F32)] * 6)),
        name="gmlp_proj",
    )(x2d, gain, w_in, w_in, v_gain, sp_w, spb)


def _mm_kernel(h_ref, w_ref, o_ref):
    o_ref[...] = jnp.dot(h_ref[...], w_ref[...].astype(BF16), preferred_element_type=F32).astype(o_ref.dtype)


def _matmul(h, w, layer, col0, n_cols, seq, out_dtype, name):
    T, D = h.shape
    tm = _tile(seq, 1024)
    tn = _tile(n_cols, 2048)
    assert col0 % tn == 0
    j0 = col0 // tn
    blocks = [((tm, D), BF16), ((D, tn), w.dtype), ((tm, tn), out_dtype)]
    return pl.pallas_call(
        _mm_kernel,
        grid=(T // tm, n_cols // tn),
        in_specs=[
            pl.BlockSpec((tm, D), lambda i, j: (i, 0)),
            pl.BlockSpec((None, D, tn), lambda i, j: (layer, 0, j + j0)),
        ],
        out_specs=pl.BlockSpec((tm, tn), lambda i, j: (i, j)),
        out_shape=jax.ShapeDtypeStruct((T, n_cols), out_dtype),
        compiler_params=_params(("parallel", "arbitrary"), _vmem_limit(blocks, [], [((tm, tn), F32)])),
        name=name,
    )(h, w)


def _hgrn_kernel(gamma_ref, q_ref, f_ref, i_ref, g_ref, og_ref, o_ref, st_ref, g_scr, k_scr, intra_scr, inter_scr,
                 *, layer, chunk):
    @pl.when(pl.program_id(2) == 0)
    def _():
        st_ref[...] = jnp.zeros_like(st_ref)

    gam = gamma_ref[...]
    ex = jnp.exp(gam - jnp.max(gam, axis=0, keepdims=True))
    lb_all = jnp.sum(ex[:layer + 1], axis=0, keepdims=True) / jnp.sum(ex, axis=0, keepdims=True)

    L, width = q_ref.shape
    n_chunks = L // chunk
    mid = chunk // 2 - 1
    row_chunk = lax.broadcasted_iota(jnp.int32, (L, L), 0) // chunk
    col_chunk = lax.broadcasted_iota(jnp.int32, (L, L), 1) // chunk
    mask = _tril_mask(L) & (row_chunk == col_chunk)
    chunk_of_row = lax.broadcasted_iota(jnp.int32, (L, LANES), 0) // chunk

    fl = f_ref[...]
    e = jnp.exp(-jnp.abs(fl))
    r = 1.0 / (1.0 + e)
    pos = fl >= 0
    sig = jnp.where(pos, r, e * r)
    nsig = jnp.where(pos, e * r, r)
    kk = (1.0 - lb_all) * nsig
    G = _cumsum_rows(jnp.log(lb_all + (1.0 - lb_all) * sig), mask.astype(BF16))
    g_scr[...] = G
    k_scr[...] = kk
    g_mid_rows = [G[c * chunk + mid:c * chunk + mid + 1, :] for c in range(n_chunks)]
    g_end_rows = [G[(c + 1) * chunk - 1:(c + 1) * chunk, :] for c in range(n_chunks)]
    per_row = lambda rows: jnp.concatenate([jnp.broadcast_to(x, (chunk, width)) for x in rows], axis=0)
    g_mid = per_row(g_mid_rows)
    g_end = per_row(g_end_rows)
    qv = q_ref[...]
    q_t = (qv * jnp.exp(G - g_mid)).astype(BF16)
    k_t = (kk * jnp.exp(g_mid - G)).astype(BF16)
    q_g = (qv * jnp.exp(G)).astype(BF16)
    k_end = (kk * jnp.exp(g_end - G)).astype(BF16)
    vv = i_ref[...].astype(BF16)
    decay_range = -jnp.min(jnp.concatenate(g_end_rows, axis=0))
    factorable = decay_range <= HGRN_MAX_FACTORED_RANGE

    def finish(o, lanes):
        gv = g_ref[:, lanes]
        o_ref[:, lanes] = (_rms_rows(o, og_ref[...]) * (gv * jax.nn.sigmoid(gv))).astype(o_ref.dtype)

    for hh in range(width // LANES):
        lanes = slice(hh * LANES, (hh + 1) * LANES)
        scores = jnp.where(mask, _dot_nt(q_t[:, lanes], k_t[:, lanes]), 0.0).astype(BF16)
        intra = jnp.dot(scores, vv[:, lanes], preferred_element_type=F32)
        v_exp = jnp.concatenate([jnp.where(chunk_of_row == c, vv[:, lanes], 0) for c in range(n_chunks)],
                                axis=1)
        u_t = _dot_tn(v_exp, k_end[:, lanes])
        st = st_ref[hh]
        prev = []
        for c in range(n_chunks):
            prev.append(st.astype(BF16))
            st = st * jnp.exp(g_end_rows[c][:, lanes]) + u_t[c * LANES:(c + 1) * LANES, :]
        st_ref[hh] = st
        q_exp = jnp.concatenate([jnp.where(chunk_of_row == c, q_g[:, lanes], 0) for c in range(n_chunks)],
                                axis=1)
        inter = _dot_nt(q_exp, jnp.concatenate(prev, axis=1))
        inter_scr[hh] = inter
        finish(intra + inter, lanes)

    @pl.when(jnp.logical_not(factorable))
    def _():
        for hh in range(width // LANES):
            _hgrn_intra_pairwise(q_ref, i_ref, g_scr, k_scr, intra_scr, hh, chunk)
            finish(intra_scr[hh] + inter_scr[hh], slice(hh * LANES, (hh + 1) * LANES))


def _hgrn_intra_pairwise(q_ref, i_ref, g_scr, k_scr, intra_scr, hh, chunk):
    lanes = slice(hh * LANES, (hh + 1) * LANES)
    t_idx = lax.broadcasted_iota(jnp.int32, (chunk, 1), 0)
    for c in range(q_ref.shape[0] // chunk):
        rows = slice(c * chunk, (c + 1) * chunk)
        g_c = g_scr[rows, lanes]
        q_c = q_ref[rows, lanes]

        def body(grp, acc):
            keys = pl.ds(pl.multiple_of(c * chunk + grp * SUBLANES, SUBLANES), SUBLANES)
            g_s, k_s, v_s = g_scr[keys, lanes], k_scr[keys, lanes], i_ref[keys, lanes]
            for j in range(SUBLANES):
                w = q_c * k_s[j:j + 1] * jnp.exp(jnp.minimum(g_c - g_s[j:j + 1], 0.0))
                score = jnp.where(t_idx >= grp * SUBLANES + j, jnp.sum(w, axis=1, keepdims=True), 0.0)
                acc = acc + score * v_s[j:j + 1]
            return acc

        intra_scr[hh, rows, :] = lax.fori_loop(0, chunk // SUBLANES, body, jnp.zeros((chunk, LANES), F32))


def _hgrn(qfig, gamma, o_gain, layer, batch, seq):
    T = qfig.shape[0]
    b_width = gamma.shape[1]
    n_layers = gamma.shape[0]
    width = _tile(b_width, 1024)
    L = _tile(seq, 256, HGRN_CHUNK)
    nw = b_width // width
    nl = seq // L
    blocks = [((L, width), F32)] * 4 + [((L, width), BF16), ((n_layers, width), F32)]
    n_sub = width // LANES
    scratch = [((n_sub, LANES, LANES), F32), ((L, width), F32), ((L, width), F32), ((n_sub, L, LANES), F32),
               ((n_sub, L, LANES), F32)]
    row = lambda b, h, l: b * nl + l
    return pl.pallas_call(
        functools.partial(_hgrn_kernel, layer=layer, chunk=HGRN_CHUNK),
        grid=(batch, nw, nl),
        in_specs=[
            pl.BlockSpec((n_layers, width), lambda b, h, l: (0, h)),
            pl.BlockSpec((L, width), lambda b, h, l: (row(b, h, l), h)),
            pl.BlockSpec((L, width), lambda b, h, l: (row(b, h, l), h + nw)),
            pl.BlockSpec((L, width), lambda b, h, l: (row(b, h, l), h + 2 * nw)),
            pl.BlockSpec((L, width), lambda b, h, l: (row(b, h, l), h + 3 * nw)),
            pl.BlockSpec((1, LANES), lambda b, h, l: (0, 0)),
        ],
        out_specs=pl.BlockSpec((L, width), lambda b, h, l: (row(b, h, l), h)),
        out_shape=jax.ShapeDtypeStruct((T, b_width), BF16),
        scratch_shapes=[pltpu.VMEM(sh, dt) for sh, dt in scratch],
        compiler_params=_params(("parallel", "parallel", "arbitrary"),
                                _vmem_limit(blocks, scratch, [((L, width), F32)] * 12)),
        name="hgrn2",
    )(gamma, qfig, qfig, qfig, qfig, o_gain)


def _mm_resid_kernel(*refs, n_lhs):
    lhs_refs = refs[:n_lhs]
    w_ref, r_ref, o_ref = refs[n_lhs:]
    acc = r_ref[...]
    k0 = 0
    for a_ref in lhs_refs:
        k = a_ref.shape[1]
        acc = acc + jnp.dot(a_ref[...], w_ref[k0:k0 + k, :].astype(BF16), preferred_element_type=F32)
        k0 += k
    o_ref[...] = acc


def _matmul_residual(lhs_list, w, layer, resid, seq, name, tm_pref=1024, tn_pref=512, rows_inner=False):
    T, N = resid.shape
    K = w.shape[1]
    tm = _tile(seq, tm_pref)
    tn = _tile(N, tn_pref)
    blocks = [((tm, a.shape[1]), BF16) for a in lhs_list] + [((K, tn), w.dtype), ((tm, tn), F32), ((tm, tn), F32)]
    grid = (N // tn, T // tm) if rows_inner else (T // tm, N // tn)
    ij = (lambda g0, g1: (g1, g0)) if rows_inner else (lambda g0, g1: (g0, g1))
    return pl.pallas_call(
        functools.partial(_mm_resid_kernel, n_lhs=len(lhs_list)),
        grid=grid,
        in_specs=[pl.BlockSpec((tm, a.shape[1]), lambda g0, g1: (ij(g0, g1)[0], 0)) for a in lhs_list] + [
            pl.BlockSpec((None, K, tn), lambda g0, g1: (layer, 0, ij(g0, g1)[1])),
            pl.BlockSpec((tm, tn), lambda g0, g1: ij(g0, g1)),
        ],
        out_specs=pl.BlockSpec((tm, tn), lambda g0, g1: ij(g0, g1)),
        out_shape=jax.ShapeDtypeStruct((T, N), F32),
        compiler_params=_params(("parallel", "arbitrary"), _vmem_limit(blocks, [], [((tm, tn), F32)])),
        name=name,
    )(*lhs_list, w, resid)


def _ffn_up_kernel(x_ref, xh_ref, gain_ref, wa_ref, wb_ref, cwa_ref, cwb_ref, cba_ref, cbb_ref,
                   o_ref, h_ref, za_ref, *, n_tiles, n_col_tiles, tiles_per_seq, halo):
    s = pl.program_id(0)
    cur = jnp.minimum(s, n_tiles - 1)
    prev = jnp.maximum(s - 1, 0)
    row_c = cur // n_col_tiles
    tm = x_ref.shape[0]

    @pl.when(s == 0)
    def _():
        za_ref[...] = jnp.zeros_like(za_ref)

    @pl.when((cur % n_col_tiles == 0) & (s < n_tiles))
    def _():
        slot = row_c % 2
        keep = (row_c % tiles_per_seq != 0).astype(F32)
        h_ref[slot, 0:halo, :] = (_rms_rows(xh_ref[...], gain_ref[...]) * keep).astype(h_ref.dtype)
        rows = min(256, tm)

        def body(r, carry):
            src = pl.ds(pl.multiple_of(r * rows, rows), rows)
            dst = pl.ds(pl.multiple_of(r * rows + halo, halo), rows)
            h_ref[slot, dst, :] = _rms_rows(x_ref[src, :], gain_ref[...]).astype(h_ref.dtype)
            return carry

        lax.fori_loop(0, tm // rows, body, 0)

    def conv(z, cw_ref, cb_ref):
        z1 = pltpu.roll(z, 1, 0)
        z2 = pltpu.roll(z, 2, 0)
        y = cw_ref[0:1, :] * z2 + cw_ref[1:2, :] * z1 + cw_ref[2:3, :] * z + cb_ref[...]
        return y[halo:, :]

    h_prev = h_ref[(prev // n_col_tiles) % 2]
    zb = jnp.dot(h_prev, wb_ref[...].astype(BF16), preferred_element_type=F32)
    a = conv(za_ref[...], cwa_ref, cba_ref)
    o_ref[...] = (a * jax.nn.sigmoid(a) * conv(zb, cwb_ref, cbb_ref)).astype(o_ref.dtype)
    za_ref[...] = jnp.dot(h_ref[row_c % 2], wa_ref[...].astype(BF16), preferred_element_type=F32)


def _ffn_up(x2d, gain, w_up, layer, conv_w, conv_b, seq):
    T, D = x2d.shape
    F = w_up.shape[2] // 2
    halo = 16
    tm = _tile(seq, 1024)
    tn = _tile(F, 512, MXU_COLS)
    nj = F // tn
    n_tiles = (T // tm) * nj
    tiles_per_seq = seq // tm
    blocks = [((tm, D), F32), ((halo, D), F32), ((1, D), F32), ((D, tn), F32), ((D, tn), F32),
              ((3, tn), F32), ((3, tn), F32), ((1, tn), F32), ((1, tn), F32), ((tm, tn), BF16)]
    scratch = [((2, tm + halo, D), BF16), ((tm + halo, tn), F32)]
    hpt = tm // halo
    cur = lambda s: jnp.minimum(s, n_tiles - 1)
    prev = lambda s: jnp.maximum(s - 1, 0)
    return pl.pallas_call(
        functools.partial(_ffn_up_kernel, n_tiles=n_tiles, n_col_tiles=nj, tiles_per_seq=tiles_per_seq,
                          halo=halo),
        grid=(n_tiles + 1,),
        in_specs=[
            pl.BlockSpec((tm, D), lambda s: (cur(s) // nj, 0)),
            pl.BlockSpec((halo, D), lambda s: (jnp.maximum(cur(s) // nj * hpt - 1, 0), 0)),
            pl.BlockSpec((1, D), lambda s: (0, 0)),
            pl.BlockSpec((None, D, tn), lambda s: (layer, 0, cur(s) % nj)),
            pl.BlockSpec((None, D, tn), lambda s: (layer, 0, prev(s) % nj + nj)),
            pl.BlockSpec((3, tn), lambda s: (0, prev(s) % nj)),
            pl.BlockSpec((3, tn), lambda s: (0, prev(s) % nj + nj)),
            pl.BlockSpec((1, tn), lambda s: (0, prev(s) % nj)),
            pl.BlockSpec((1, tn), lambda s: (0, prev(s) % nj + nj)),
        ],
        out_specs=pl.BlockSpec((tm, tn), lambda s: (prev(s) // nj, prev(s) % nj)),
        out_shape=jax.ShapeDtypeStruct((T, F), BF16),
        scratch_shapes=[pltpu.VMEM(sh, dt) for sh, dt in scratch],
        compiler_params=_params(("arbitrary",),
                                _vmem_limit(blocks, scratch, [((tm + halo, MXU_COLS), F32)] * 8)),
        name="ffn_up_conv_gate",
    )(x2d, x2d, gain, w_up, w_up, conv_w, conv_w, conv_b, conv_b)


def _conv_ffn(x2d, gain, w_up, layer, conv_w, conv_b, w_down, seq):
    act = _ffn_up(x2d, gain, w_up, layer, conv_w, conv_b, seq)
    return _matmul_residual([act], w_down, layer, x2d, seq, "ffn_down", tn_pref=512, rows_inner=True)


def _qkvg_kernel(x_ref, gain_ref, w_ref, hg_ref, o_ref, h_ref, *, n_norm_tiles):
    j = pl.program_id(1)
    n_sub = o_ref.shape[1] // MXU_COLS

    @pl.when(j == 0)
    def _():
        _fill_normed(x_ref, gain_ref, h_ref)

    @pl.when(j < n_norm_tiles)
    def _():
        h = h_ref[...]
        for c in range(n_sub):
            y = _dot_nt(h, w_ref[c * MXU_COLS:(c + 1) * MXU_COLS, :])
            for grp in range(MXU_COLS // LANES):
                src = slice(grp * LANES, (grp + 1) * LANES)
                dst = slice(c * MXU_COLS + grp * LANES, c * MXU_COLS + (grp + 1) * LANES)
                o_ref[:, dst] = _rms_rows(y[:, src], hg_ref[:, dst]).astype(o_ref.dtype)

    @pl.when(j >= n_norm_tiles)
    def _():
        h = h_ref[...]
        for c in range(n_sub):
            cols = slice(c * MXU_COLS, (c + 1) * MXU_COLS)
            o_ref[:, cols] = _dot_nt(h, w_ref[cols, :]).astype(o_ref.dtype)


def _qkvg_proj(x2d, gain, w, layer, head_gain, n_cols, n_norm_cols, seq):
    T, D = x2d.shape
    tm = _tile(seq, 1024)
    tn = _tile(n_norm_cols // 2, 2048, MXU_COLS)
    blocks = [((tm, D), F32), ((1, D), F32), ((D, tn), w.dtype), ((1, tn), F32), ((tm, tn), BF16)]
    return pl.pallas_call(
        functools.partial(_qkvg_kernel, n_norm_tiles=n_norm_cols // tn),
        grid=(T // tm, n_cols // tn),
        in_specs=[
            pl.BlockSpec((tm, D), lambda i, j: (i, 0)),
            pl.BlockSpec((1, D), lambda i, j: (0, 0)),
            pl.BlockSpec((None, tn, D), lambda i, j: (layer, j, 0)),
            pl.BlockSpec((1, tn), lambda i, j: (0, j)),
        ],
        out_specs=[pl.BlockSpec((tm, tn), lambda i, j: (i, j)), pl.BlockSpec((tm, D), lambda i, j: (i, 0))],
        out_shape=[jax.ShapeDtypeStruct((T, n_cols), BF16), jax.ShapeDtypeStruct((T, D), BF16)],
        compiler_params=_params(("parallel", "arbitrary"),
                                _vmem_limit(blocks + [((tm, D), BF16)], [], [((tm, MXU_COLS), F32)] * 4)),
        name="qkvg_proj",
    )(x2d, gain, w, head_gain)


def _fgate_kernel(h_ref, wf_ref, bf_ref, pq_ref, pk_ref, cq_ref, ck_ref, qa_ref, ka_ref,
                  carry_ref, *, n_heads):
    @pl.when(pl.program_id(1) == 0)
    def _():
        carry_ref[...] = jnp.zeros_like(carry_ref)

    tm = h_ref.shape[0]
    f = jnp.dot(h_ref[...], wf_ref[...], preferred_element_type=F32) + bf_ref[...]
    log_f = (jnp.minimum(f, 0.0) - jnp.log1p(jnp.exp(-jnp.abs(f)))) * LOG2E
    c = _cumsum_rows(log_f, _tril_mask(tm).astype(BF16)) + carry_ref[...]
    carry_ref[...] = c[tm - 1:tm, :]
    hi = c.astype(BF16)
    r1 = c - hi.astype(F32)
    mid = r1.astype(BF16)
    lo = (r1 - mid.astype(F32)).astype(BF16)
    lane = lax.broadcasted_iota(jnp.int32, c.shape, 1)
    parts = jnp.where(lane < n_heads, hi, jnp.where(lane < 2 * n_heads, mid, lo))
    qa_ref[...] = (jnp.dot(parts, pq_ref[...], preferred_element_type=F32) + cq_ref[...]).astype(qa_ref.dtype)
    ka_ref[...] = (jnp.dot(parts, pk_ref[...], preferred_element_type=F32) + ck_ref[...]).astype(ka_ref.dtype)


def _fgate_selectors(n_heads):
    W = n_heads * LANES
    pq = np.zeros((LANES, W), np.float32)
    pk = np.zeros((LANES, W), np.float32)
    cq = np.zeros((1, W), np.float32)
    ck = np.zeros((1, W), np.float32)
    for hd in range(n_heads):
        for part in range(3):
            pq[part * n_heads + hd, hd * LANES + part] = 1.0
            pk[part * n_heads + hd, hd * LANES + 3 + part] = -1.0
            cq[0, hd * LANES + 3 + part] = 1.0
            ck[0, hd * LANES + part] = 1.0
    return jnp.asarray(pq, BF16), jnp.asarray(pk, BF16), jnp.asarray(cq), jnp.asarray(ck)


def _fgate(h, w_f, b_f, n_heads, batch, seq):
    T, D = h.shape
    tm = _tile(seq, 512)
    nt = seq // tm
    W = n_heads * LANES
    pq, pk, cq, ck = _fgate_selectors(n_heads)
    blocks = [((tm, D), BF16), ((D, LANES), BF16), ((1, LANES), F32),
              ((LANES, W), BF16), ((LANES, W), BF16), ((1, W), F32), ((1, W), F32),
              ((tm, W), BF16), ((tm, W), BF16)]
    const = lambda b, i: (0, 0)
    return pl.pallas_call(
        functools.partial(_fgate_kernel, n_heads=n_heads),
        grid=(batch, nt),
        in_specs=[
            pl.BlockSpec((tm, D), lambda b, i: (b * nt + i, 0)),
            pl.BlockSpec((D, LANES), const),
            pl.BlockSpec((1, LANES), const),
            pl.BlockSpec((LANES, W), const),
            pl.BlockSpec((LANES, W), const),
            pl.BlockSpec((1, W), const),
            pl.BlockSpec((1, W), const),
        ],
        out_specs=[pl.BlockSpec((tm, W), lambda b, i: (b * nt + i, 0)),
                   pl.BlockSpec((tm, W), lambda b, i: (b * nt + i, 0))],
        out_shape=[jax.ShapeDtypeStruct((T, W), BF16), jax.ShapeDtypeStruct((T, W), BF16)],
        scratch_shapes=[pltpu.VMEM((1, LANES), F32)],
        compiler_params=_params(("parallel", "arbitrary"),
                                _vmem_limit(blocks, [], [((tm, D), F32)] * 2 + [((tm, tm), BF16)])),
        name="fox_forget_cumsum",
    )(h, w_f, b_f, pq, pk, cq, ck)


def _attn_kernel(q_ref, qa_ref, k_ref, ka_ref, v_ref, g_ref, o_ref, qs_ref, acc_ref, m_ref, l_ref):
    qi = pl.program_id(2)
    tq = tk = q_ref.shape[0]
    n_sub = q_ref.shape[1] // LANES
    for hh in range(n_sub):
        lanes = slice(hh * LANES, (hh + 1) * LANES)
        qs_ref[hh] = jnp.concatenate([q_ref[:, lanes], qa_ref[:, lanes]], axis=1)
    def step(first_key, n_keys, diag_offset=None):
        rows = pl.ds(pl.multiple_of(first_key, tk), n_keys)
        for hh in range(n_sub):
            lanes = slice(hh * LANES, (hh + 1) * LANES)
            k = jnp.concatenate([k_ref[rows, lanes], ka_ref[rows, lanes]], axis=1)
            s = _dot_nt(qs_ref[hh], k)
            if diag_offset is not None:
                t_pos = lax.broadcasted_iota(jnp.int32, (tq, n_keys), 0)
                s_pos = lax.broadcasted_iota(jnp.int32, (tq, n_keys), 1) + diag_offset
                s = jnp.where(s_pos <= t_pos, s, -jnp.inf)
                m_new = jnp.broadcast_to(jnp.max(s, axis=1, keepdims=True), (tq, LANES))
            else:
                m_prev = m_ref[hh]
                m_new = jnp.maximum(m_prev, jnp.max(s, axis=1, keepdims=True))
                alpha = jnp.exp2(m_prev - m_new)
            p = jnp.exp2(s - jnp.tile(m_new, (1, n_keys // LANES)))
            l_new = jnp.broadcast_to(jnp.sum(p, axis=1, keepdims=True), (tq, LANES))
            pv = jnp.dot(p.astype(BF16), v_ref[rows, lanes], preferred_element_type=F32)
            m_ref[hh] = m_new
            if diag_offset is not None:
                l_ref[hh], acc_ref[hh] = l_new, pv
            else:
                l_ref[hh] = alpha * l_ref[hh] + l_new
                acc_ref[hh] = alpha * acc_ref[hh] + pv

    def loop(n, body):
        lax.fori_loop(0, n, lambda j, c: (body(j), c)[1], 0)

    @pl.when(qi % 2 == 1)
    def _():
        step((qi - 1) * tk, 2 * tk, -tk)

    @pl.when(qi % 2 == 0)
    def _():
        step(qi * tk, tk, 0)

    n_blocks = k_ref.shape[0] // tk
    if n_blocks > 4:
        loop(qi // 4, lambda j: step(j * 4 * tk, 4 * tk))
    if n_blocks > 2:
        loop((qi % 4) // 2, lambda j: step((qi // 4) * 4 * tk, 2 * tk))
    for hh in range(n_sub):
        lanes = slice(hh * LANES, (hh + 1) * LANES)
        gate = jax.nn.sigmoid(g_ref[:, lanes].astype(F32))
        o_ref[:, lanes] = (acc_ref[hh] / l_ref[hh] * gate).astype(o_ref.dtype)


def _attention(qkvg, qa, ka, n_heads, batch, seq):
    T = qkvg.shape[0]
    tq = _tile(seq, ATTN_BLOCK)
    nq = seq // tq
    width = ATTN_HEADS_PER_STEP * LANES
    ng = n_heads * LANES // width
    blocks = [((tq, width), BF16)] * 2 + [((seq, width), BF16)] * 3 + [((tq, width), BF16)] * 2
    scratch = [((width // LANES, tq, 2 * LANES), BF16)] + [((width // LANES, tq, LANES), F32)] * 3
    return pl.pallas_call(
        _attn_kernel,
        grid=(batch, ng, nq),
        in_specs=[
            pl.BlockSpec((tq, width), lambda b, h, i: (b * nq + i, h)),
            pl.BlockSpec((tq, width), lambda b, h, i: (b * nq + i, h)),
            pl.BlockSpec((seq, width), lambda b, h, i: (b, h + ng)),
            pl.BlockSpec((seq, width), lambda b, h, i: (b, h)),
            pl.BlockSpec((seq, width), lambda b, h, i: (b, h + 2 * ng)),
            pl.BlockSpec((tq, width), lambda b, h, i: (b * nq + i, h + 3 * ng)),
        ],
        out_specs=pl.BlockSpec((tq, width), lambda b, h, i: (b * nq + i, h)),
        out_shape=jax.ShapeDtypeStruct((T, n_heads * LANES), BF16),
        scratch_shapes=[pltpu.VMEM(sh, dt) for sh, dt in scratch],
        compiler_params=_params(("parallel", "parallel", "arbitrary"),
                                _vmem_limit(blocks, scratch, [((tq, 4 * tq), F32)] * 2 * ATTN_HEADS_PER_STEP)),
        name="fox_attention",
    )(qkvg, qa, qkvg, ka, qkvg, qkvg)


def _mixer_ab(x2d, gain, w_in, j, sp_w, sp_b, v_gain, gamma, o_gain, w_out, layer, batch, seq):
    a_width = v_gain.shape[1]
    b_width = gamma.shape[1]
    y_a, h = _gmlp_proj(x2d, gain, w_in, j, v_gain, sp_w, sp_b, seq)
    qfig = _matmul(h, w_in, j, 2 * a_width, 4 * b_width, seq, F32, "hgrn_in_proj")
    y_b = _hgrn(qfig, gamma, o_gain, layer, batch, seq)
    return _matmul_residual([y_a, y_b], w_out, j, x2d, seq, "ab_out_proj", tm_pref=OUT_PROJ_ROWS)


def _mixer_c(x2d, gain, w_in, w_forget, j, b_f, q_gain, k_gain, w_out, batch, seq):
    D = x2d.shape[1]
    n_heads = b_f.shape[1]
    head_dim = q_gain.shape[1]
    assert head_dim == LANES and n_heads * head_dim == D
    scale = head_dim ** -0.5 * LOG2E
    head_gain = jnp.concatenate([jnp.tile(q_gain * scale, (1, n_heads)), jnp.tile(k_gain, (1, n_heads)),
                                 jnp.ones((1, 2 * D), F32)], axis=1)
    qkvg, h = _qkvg_proj(x2d, gain, w_in, j, head_gain, 4 * D, 2 * D, seq)
    assert 3 * n_heads <= LANES
    pad = ((0, 0), (0, LANES - 3 * n_heads))
    w_f = jnp.pad(jnp.tile(w_forget, (1, 3)), pad).astype(BF16)
    b_fp = jnp.pad(jnp.tile(b_f, (1, 3)), pad)
    qa, ka = _fgate(h, w_f, b_fp, n_heads, batch, seq)
    o = _attention(qkvg, qa, ka, n_heads, batch, seq)
    return _matmul_residual([o], w_out, j, x2d, seq, "attn_out_proj", tm_pref=OUT_PROJ_ROWS)


def kernel(x, mix_norm, ab_w_in, ab_sp_w, ab_sp_b, ab_v_norm, hgrn_gamma, hgrn_o_norm, ab_w_out,
           c_w_in, c_b_f, c_q_norm, c_k_norm, c_w_out, ffn_norm, ffn_w_up, ffn_conv_w, ffn_conv_b,
           ffn_w_down):
    batch, seq, D = x.shape
    depth = mix_norm.shape[0]
    x2d = x.reshape(batch * seq, D)
    ab_w_in_b = ab_w_in.astype(BF16)
    c_w_in_t = jnp.swapaxes(c_w_in, 1, 2).astype(BF16)
    ffn_w_down = ffn_w_down.astype(BF16)
    ab_w_out = ab_w_out.astype(BF16)
    c_w_out = c_w_out.astype(BF16)
    for l in range(depth):
        j = l // 2
        gain = mix_norm[l][None, :]
        if l % 2 == 0:
            x2d = _mixer_ab(x2d, gain, ab_w_in_b, j, ab_sp_w[j], ab_sp_b[j], ab_v_norm[j][None, :],
                            hgrn_gamma, hgrn_o_norm[j][None, :], ab_w_out, l, batch, seq)
        else:
            x2d = _mixer_c(x2d, gain, c_w_in_t, c_w_in[j, :, 4 * D:], j, c_b_f[j][None, :],
                           c_q_norm[j][None, :], c_k_norm[j][None, :], c_w_out, batch, seq)
        x2d = _conv_ffn(x2d, ffn_norm[l][None, :], ffn_w_up, l, ffn_conv_w[l],
                        ffn_conv_b[l][None, :], ffn_w_down, seq)
    return x2d.reshape(batch, seq, D)
```

```python
import functools
import math

import jax
import jax.numpy as jnp
import numpy as np
from jax import lax
from jax.experimental import pallas as pl
from jax.experimental.pallas import tpu as pltpu

F32 = jnp.float32
BF16 = jnp.bfloat16
RMS_EPS = 1e-6
LANES = 128
SUBLANES = 8
MXU_COLS = 256
HGRN_CHUNK = 64
HGRN_MAX_FACTORED_RANGE = 60.0
OUT_PROJ_ROWS = 2048
ATTN_BLOCK = 512
ATTN_HEADS_PER_STEP = 4
LOG2E = 1.4426950408889634
V7X_VMEM_BYTES = 64 * 1024 * 1024
VMEM_CAP_BYTES = V7X_VMEM_BYTES - 6 * 1024 * 1024


def _nbytes(shape, dtype):
    return math.prod(shape) * jnp.dtype(dtype).itemsize


def _vmem_limit(blocks, scratch=(), temps=()):
    est = 2 * sum(_nbytes(s, d) for s, d in blocks)
    est += sum(_nbytes(s, d) for s, d in scratch) + sum(_nbytes(s, d) for s, d in temps)
    return int(min(VMEM_CAP_BYTES, est * 5 // 4 + (4 << 20)))


def _tile(n, pref, mult=LANES):
    if n <= pref:
        return n
    t = (pref // mult) * mult
    while n % t:
        t -= mult
    return t


def _params(sem, limit):
    return pltpu.CompilerParams(dimension_semantics=sem, vmem_limit_bytes=limit)


def _rms_rows(xf, gain):
    ms = jnp.mean(xf * xf, axis=-1, keepdims=True)
    return xf * lax.rsqrt(ms + RMS_EPS) * gain


def _gelu(x):
    return 0.5 * x * (1.0 + lax.erf(x * (2.0 ** -0.5)))


def _fill_normed(x_ref, gain_ref, h_ref, rows=256):
    rows = min(rows, x_ref.shape[0])

    def body(r, carry):
        sl = pl.ds(pl.multiple_of(r * rows, rows), rows)
        h_ref[sl, :] = _rms_rows(x_ref[sl, :], gain_ref[...]).astype(h_ref.dtype)
        return carry

    lax.fori_loop(0, x_ref.shape[0] // rows, body, 0)


def _tril_mask(n):
    t = lax.broadcasted_iota(jnp.int32, (n, n), 0)
    s = lax.broadcasted_iota(jnp.int32, (n, n), 1)
    return s <= t


def _cumsum_rows(x, tril_bf16):
    hi = x.astype(BF16)
    r1 = x - hi.astype(F32)
    mid = r1.astype(BF16)
    lo = (r1 - mid.astype(F32)).astype(BF16)
    y = jnp.dot(tril_bf16, jnp.concatenate([hi, mid, lo], axis=1), preferred_element_type=F32)
    d = x.shape[1]
    return y[:, :d] + y[:, d:2 * d] + y[:, 2 * d:]


def _dot_nt(a, b):
    return lax.dot_general(a, b, (((1,), (1,)), ((), ())), preferred_element_type=F32)


def _dot_tn(a, b):
    return lax.dot_general(a, b, (((0,), (0,)), ((), ())), preferred_element_type=F32)


def _gmlp_kernel(x_ref, gain_ref, wu_ref, wv_ref, vg_ref, spw_ref, spb_ref, o_ref, h_ref, *, chunk):
    @pl.when(pl.program_id(1) == 0)
    def _():
        _fill_normed(x_ref, gain_ref, h_ref)

    h = h_ref[...]
    u = _gelu(jnp.dot(h, wu_ref[...].astype(BF16), preferred_element_type=F32))
    v = _gelu(jnp.dot(h, wv_ref[...].astype(BF16), preferred_element_type=F32))
    tm, tn = u.shape
    n_chunks = tm // chunk
    tril = _tril_mask(chunk)
    for hh in range(tn // LANES):
        lanes = slice(hh * LANES, (hh + 1) * LANES)
        vh = _rms_rows(v[:, lanes], vg_ref[:, lanes]).astype(BF16)
        vcat = jnp.concatenate([vh[c * chunk:(c + 1) * chunk, :] for c in range(n_chunks)], axis=1)
        w_causal = jnp.where(tril, spw_ref[hh], 0.0).astype(BF16)
        mixed = jnp.dot(w_causal, vcat, preferred_element_type=F32)
        for c in range(n_chunks):
            rows = slice(c * chunk, (c + 1) * chunk)
            m_c = mixed[:, c * LANES:(c + 1) * LANES] + spb_ref[hh]
            o_ref[rows, lanes] = (u[rows, lanes] * m_c).astype(o_ref.dtype)


def _gmlp_proj(x2d, gain, w_in, layer, v_gain, sp_w, sp_b, seq):
    T, D = x2d.shape
    n_heads, chunk, _ = sp_w.shape
    a_width = n_heads * LANES
    tm = _tile(seq, 1024, chunk)
    tn = _tile(a_width, 1024)
    nj = a_width // tn
    spb = jnp.broadcast_to(sp_b[:, :, None], (n_heads, chunk, LANES))
    blocks = [((tm, D), F32), ((1, D), F32), ((D, tn), w_in.dtype), ((D, tn), w_in.dtype), ((1, tn), F32),
              ((tn // LANES, chunk, chunk), F32), ((tn // LANES, chunk, LANES), F32), ((tm, tn), BF16)]
    return pl.pallas_call(
        functools.partial(_gmlp_kernel, chunk=chunk),
        grid=(T // tm, nj),
        in_specs=[
            pl.BlockSpec((tm, D), lambda i, j: (i, 0)),
            pl.BlockSpec((1, D), lambda i, j: (0, 0)),
            pl.BlockSpec((None, D, tn), lambda i, j: (layer, 0, j)),
            pl.BlockSpec((None, D, tn), lambda i, j: (layer, 0, j + nj)),
            pl.BlockSpec((1, tn), lambda i, j: (0, j)),
            pl.BlockSpec((tn // LANES, chunk, chunk), lambda i, j: (j, 0, 0)),
            pl.BlockSpec((tn // LANES, chunk, LANES), lambda i, j: (j, 0, 0)),
        ],
        out_specs=[pl.BlockSpec((tm, tn), lambda i, j: (i, j)), pl.BlockSpec((tm, D), lambda i, j: (i, 0))],
        out_shape=[jax.ShapeDtypeStruct((T, a_width), BF16), jax.ShapeDtypeStruct((T, D), BF16)],
        compiler_params=_params(("parallel", "arbitrary"),
                                _vmem_limit(blocks + [((tm, D), BF16)], [], [((tm, tn), F32)] * 6)),
        name="gmlp_proj",
    )(x2d, gain, w_in, w_in, v_gain, sp_w, spb)


def _mm_kernel(h_ref, w_ref, o_ref):
    o_ref[...] = jnp.dot(h_ref[...], w_ref[...].astype(BF16), preferred_element_type=F32).astype(o_ref.dtype)


def _matmul(h, w, layer, col0, n_cols, seq, out_dtype, name):
    T, D = h.shape
    tm = _tile(seq, 1024)
    tn = _tile(n_cols, 2048)
    assert col0 % tn == 0
    j0 = col0 // tn
    blocks = [((tm, D), BF16), ((D, tn), w.dtype), ((tm, tn), out_dtype)]
    return pl.pallas_call(
        _mm_kernel,
        grid=(n_cols // tn, T // tm),
        in_specs=[
            pl.BlockSpec((tm, D), lambda j, i: (i, 0)),
            pl.BlockSpec((None, D, tn), lambda j, i: (layer, 0, j + j0)),
        ],
        out_specs=pl.BlockSpec((tm, tn), lambda j, i: (i, j)),
        out_shape=jax.ShapeDtypeStruct((T, n_cols), out_dtype),
        compiler_params=_params(("parallel", "arbitrary"), _vmem_limit(blocks, [], [((tm, tn), F32)])),
        name=name,
    )(h, w)


def _hgrn_kernel(gamma_ref, q_ref, f_ref, i_ref, g_ref, og_ref, o_ref, st_ref, g_scr, k_scr, intra_scr, inter_scr,
                 *, layer, chunk):
    @pl.when(pl.program_id(2) == 0)
    def _():
        st_ref[...] = jnp.zeros_like(st_ref)

    gam = gamma_ref[...]
    ex = jnp.exp(gam - jnp.max(gam, axis=0, keepdims=True))
    lb_all = jnp.sum(ex[:layer + 1], axis=0, keepdims=True) / jnp.sum(ex, axis=0, keepdims=True)

    L, width = q_ref.shape
    n_chunks = L // chunk
    mid = chunk // 2 - 1
    row_chunk = lax.broadcasted_iota(jnp.int32, (L, L), 0) // chunk
    col_chunk = lax.broadcasted_iota(jnp.int32, (L, L), 1) // chunk
    mask = _tril_mask(L) & (row_chunk == col_chunk)
    chunk_of_row = lax.broadcasted_iota(jnp.int32, (L, LANES), 0) // chunk

    fl = f_ref[...]
    e = jnp.exp(-jnp.abs(fl))
    r = 1.0 / (1.0 + e)
    pos = fl >= 0
    sig = jnp.where(pos, r, e * r)
    nsig = jnp.where(pos, e * r, r)
    kk = (1.0 - lb_all) * nsig
    G = _cumsum_rows(jnp.log(lb_all + (1.0 - lb_all) * sig), mask.astype(BF16))
    g_scr[...] = G
    k_scr[...] = kk
    g_mid_rows = [G[c * chunk + mid:c * chunk + mid + 1, :] for c in range(n_chunks)]
    g_end_rows = [G[(c + 1) * chunk - 1:(c + 1) * chunk, :] for c in range(n_chunks)]
    per_row = lambda rows: jnp.concatenate([jnp.broadcast_to(x, (chunk, width)) for x in rows], axis=0)
    g_mid = per_row(g_mid_rows)
    g_end = per_row(g_end_rows)
    qv = q_ref[...]
    q_t = (qv * jnp.exp(G - g_mid)).astype(BF16)
    k_t = (kk * jnp.exp(g_mid - G)).astype(BF16)
    q_g = (qv * jnp.exp(G)).astype(BF16)
    k_end = (kk * jnp.exp(g_end - G)).astype(BF16)
    vv = i_ref[...].astype(BF16)
    decay_range = -jnp.min(jnp.concatenate(g_end_rows, axis=0))
    factorable = decay_range <= HGRN_MAX_FACTORED_RANGE

    def finish(o, lanes):
        gv = g_ref[:, lanes]
        o_ref[:, lanes] = (_rms_rows(o, og_ref[...]) * (gv * jax.nn.sigmoid(gv))).astype(o_ref.dtype)

    for hh in range(width // LANES):
        lanes = slice(hh * LANES, (hh + 1) * LANES)
        scores = jnp.where(mask, _dot_nt(q_t[:, lanes], k_t[:, lanes]), 0.0).astype(BF16)
        intra = jnp.dot(scores, vv[:, lanes], preferred_element_type=F32)
        v_exp = jnp.concatenate([jnp.where(chunk_of_row == c, vv[:, lanes], 0) for c in range(n_chunks)],
                                axis=1)
        u_t = _dot_tn(v_exp, k_end[:, lanes])
        st = st_ref[hh]
        prev = []
        for c in range(n_chunks):
            prev.append(st.astype(BF16))
            st = st * jnp.exp(g_end_rows[c][:, lanes]) + u_t[c * LANES:(c + 1) * LANES, :]
        st_ref[hh] = st
        q_exp = jnp.concatenate([jnp.where(chunk_of_row == c, q_g[:, lanes], 0) for c in range(n_chunks)],
                                axis=1)
        inter = _dot_nt(q_exp, jnp.concatenate(prev, axis=1))
        inter_scr[hh] = inter
        finish(intra + inter, lanes)

    @pl.when(jnp.logical_not(factorable))
    def _():
        for hh in range(width // LANES):
            _hgrn_intra_pairwise(q_ref, i_ref, g_scr, k_scr, intra_scr, hh, chunk)
            finish(intra_scr[hh] + inter_scr[hh], slice(hh * LANES, (hh + 1) * LANES))


def _hgrn_intra_pairwise(q_ref, i_ref, g_scr, k_scr, intra_scr, hh, chunk):
    lanes = slice(hh * LANES, (hh + 1) * LANES)
    t_idx = lax.broadcasted_iota(jnp.int32, (chunk, 1), 0)
    for c in range(q_ref.shape[0] // chunk):
        rows = slice(c * chunk, (c + 1) * chunk)
        g_c = g_scr[rows, lanes]
        q_c = q_ref[rows, lanes]

        def body(grp, acc):
            keys = pl.ds(pl.multiple_of(c * chunk + grp * SUBLANES, SUBLANES), SUBLANES)
            g_s, k_s, v_s = g_scr[keys, lanes], k_scr[keys, lanes], i_ref[keys, lanes]
            for j in range(SUBLANES):
                w = q_c * k_s[j:j + 1] * jnp.exp(jnp.minimum(g_c - g_s[j:j + 1], 0.0))
                score = jnp.where(t_idx >= grp * SUBLANES + j, jnp.sum(w, axis=1, keepdims=True), 0.0)
                acc = acc + score * v_s[j:j + 1]
            return acc

        intra_scr[hh, rows, :] = lax.fori_loop(0, chunk // SUBLANES, body, jnp.zeros((chunk, LANES), F32))


def _hgrn(qfig, gamma, o_gain, layer, batch, seq):
    T = qfig.shape[0]
    b_width = gamma.shape[1]
    n_layers = gamma.shape[0]
    width = _tile(b_width, 1024)
    L = _tile(seq, 256, HGRN_CHUNK)
    nw = b_width // width
    nl = seq // L
    blocks = [((L, width), F32)] * 4 + [((L, width), BF16), ((n_layers, width), F32)]
    n_sub = width // LANES
    scratch = [((n_sub, LANES, LANES), F32), ((L, width), F32), ((L, width), F32), ((n_sub, L, LANES), F32),
               ((n_sub, L, LANES), F32)]
    row = lambda b, h, l: b * nl + l
    return pl.pallas_call(
        functools.partial(_hgrn_kernel, layer=layer, chunk=HGRN_CHUNK),
        grid=(batch, nw, nl),
        in_specs=[
            pl.BlockSpec((n_layers, width), lambda b, h, l: (0, h)),
            pl.BlockSpec((L, width), lambda b, h, l: (row(b, h, l), h)),
            pl.BlockSpec((L, width), lambda b, h, l: (row(b, h, l), h + nw)),
            pl.BlockSpec((L, width), lambda b, h, l: (row(b, h, l), h + 2 * nw)),
            pl.BlockSpec((L, width), lambda b, h, l: (row(b, h, l), h + 3 * nw)),
            pl.BlockSpec((1, LANES), lambda b, h, l: (0, 0)),
        ],
        out_specs=pl.BlockSpec((L, width), lambda b, h, l: (row(b, h, l), h)),
        out_shape=jax.ShapeDtypeStruct((T, b_width), BF16),
        scratch_shapes=[pltpu.VMEM(sh, dt) for sh, dt in scratch],
        compiler_params=_params(("parallel", "parallel", "arbitrary"),
                                _vmem_limit(blocks, scratch, [((L, width), F32)] * 12)),
        name="hgrn2",
    )(gamma, qfig, qfig, qfig, qfig, o_gain)


def _mm_resid_kernel(*refs, n_lhs):
    lhs_refs = refs[:n_lhs]
    w_ref, r_ref, o_ref = refs[n_lhs:]
    acc = r_ref[...]
    k0 = 0
    for a_ref in lhs_refs:
        k = a_ref.shape[1]
        acc = acc + jnp.dot(a_ref[...], w_ref[k0:k0 + k, :].astype(BF16), preferred_element_type=F32)
        k0 += k
    o_ref[...] = acc


def _matmul_residual(lhs_list, w, layer, resid, seq, name, tm_pref=1024, tn_pref=512, rows_inner=False):
    T, N = resid.shape
    K = w.shape[1]
    tm = _tile(seq, tm_pref)
    tn = _tile(N, tn_pref)
    blocks = [((tm, a.shape[1]), BF16) for a in lhs_list] + [((K, tn), w.dtype), ((tm, tn), F32), ((tm, tn), F32)]
    grid = (N // tn, T // tm) if rows_inner else (T // tm, N // tn)
    ij = (lambda g0, g1: (g1, g0)) if rows_inner else (lambda g0, g1: (g0, g1))
    return pl.pallas_call(
        functools.partial(_mm_resid_kernel, n_lhs=len(lhs_list)),
        grid=grid,
        in_specs=[pl.BlockSpec((tm, a.shape[1]), lambda g0, g1: (ij(g0, g1)[0], 0)) for a in lhs_list] + [
            pl.BlockSpec((None, K, tn), lambda g0, g1: (layer, 0, ij(g0, g1)[1])),
            pl.BlockSpec((tm, tn), lambda g0, g1: ij(g0, g1)),
        ],
        out_specs=pl.BlockSpec((tm, tn), lambda g0, g1: ij(g0, g1)),
        out_shape=jax.ShapeDtypeStruct((T, N), F32),
        compiler_params=_params(("parallel", "arbitrary"), _vmem_limit(blocks, [], [((tm, tn), F32)])),
        name=name,
    )(*lhs_list, w, resid)


def _ffn_up_kernel(x_ref, xh_ref, gain_ref, wa_ref, wb_ref, cwa_ref, cwb_ref, cba_ref, cbb_ref,
                   o_ref, h_ref, za_ref, *, n_tiles, n_col_tiles, tiles_per_seq, halo):
    s = pl.program_id(0)
    cur = jnp.minimum(s, n_tiles - 1)
    prev = jnp.maximum(s - 1, 0)
    row_c = cur // n_col_tiles
    tm = x_ref.shape[0]

    @pl.when(s == 0)
    def _():
        za_ref[...] = jnp.zeros_like(za_ref)

    @pl.when((cur % n_col_tiles == 0) & (s < n_tiles))
    def _():
        slot = row_c % 2
        keep = (row_c % tiles_per_seq != 0).astype(F32)
        h_ref[slot, 0:halo, :] = (_rms_rows(xh_ref[...], gain_ref[...]) * keep).astype(h_ref.dtype)
        rows = min(256, tm)

        def body(r, carry):
            src = pl.ds(pl.multiple_of(r * rows, rows), rows)
            dst = pl.ds(pl.multiple_of(r * rows + halo, halo), rows)
            h_ref[slot, dst, :] = _rms_rows(x_ref[src, :], gain_ref[...]).astype(h_ref.dtype)
            return carry

        lax.fori_loop(0, tm // rows, body, 0)

    def conv(z, cw_ref, cb_ref):
        z1 = pltpu.roll(z, 1, 0)
        z2 = pltpu.roll(z, 2, 0)
        y = cw_ref[0:1, :] * z2 + cw_ref[1:2, :] * z1 + cw_ref[2:3, :] * z + cb_ref[...]
        return y[halo:, :]

    h_prev = h_ref[(prev // n_col_tiles) % 2]
    zb = jnp.dot(h_prev, wb_ref[...].astype(BF16), preferred_element_type=F32)
    a = conv(za_ref[...], cwa_ref, cba_ref)
    o_ref[...] = (a * jax.nn.sigmoid(a) * conv(zb, cwb_ref, cbb_ref)).astype(o_ref.dtype)
    za_ref[...] = jnp.dot(h_ref[row_c % 2], wa_ref[...].astype(BF16), preferred_element_type=F32)


def _ffn_up(x2d, gain, w_up, layer, conv_w, conv_b, seq):
    T, D = x2d.shape
    F = w_up.shape[2] // 2
    halo = 16
    tm = _tile(seq, 1024)
    tn = _tile(F, 512, MXU_COLS)
    nj = F // tn
    n_tiles = (T // tm) * nj
    tiles_per_seq = seq // tm
    blocks = [((tm, D), F32), ((halo, D), F32), ((1, D), F32), ((D, tn), F32), ((D, tn), F32),
              ((3, tn), F32), ((3, tn), F32), ((1, tn), F32), ((1, tn), F32), ((tm, tn), BF16)]
    scratch = [((2, tm + halo, D), BF16), ((tm + halo, tn), F32)]
    hpt = tm // halo
    cur = lambda s: jnp.minimum(s, n_tiles - 1)
    prev = lambda s: jnp.maximum(s - 1, 0)
    return pl.pallas_call(
        functools.partial(_ffn_up_kernel, n_tiles=n_tiles, n_col_tiles=nj, tiles_per_seq=tiles_per_seq,
                          halo=halo),
        grid=(n_tiles + 1,),
        in_specs=[
            pl.BlockSpec((tm, D), lambda s: (cur(s) // nj, 0)),
            pl.BlockSpec((halo, D), lambda s: (jnp.maximum(cur(s) // nj * hpt - 1, 0), 0)),
            pl.BlockSpec((1, D), lambda s: (0, 0)),
            pl.BlockSpec((None, D, tn), lambda s: (layer, 0, cur(s) % nj)),
            pl.BlockSpec((None, D, tn), lambda s: (layer, 0, prev(s) % nj + nj)),
            pl.BlockSpec((3, tn), lambda s: (0, prev(s) % nj)),
            pl.BlockSpec((3, tn), lambda s: (0, prev(s) % nj + nj)),
            pl.BlockSpec((1, tn), lambda s: (0, prev(s) % nj)),
            pl.BlockSpec((1, tn), lambda s: (0, prev(s) % nj + nj)),
        ],
        out_specs=pl.BlockSpec((tm, tn), lambda s: (prev(s) // nj, prev(s) % nj)),
        out_shape=jax.ShapeDtypeStruct((T, F), BF16),
        scratch_shapes=[pltpu.VMEM(sh, dt) for sh, dt in scratch],
        compiler_params=_params(("arbitrary",),
                                _vmem_limit(blocks, scratch, [((tm + halo, MXU_COLS), F32)] * 8)),
        name="ffn_up_conv_gate",
    )(x2d, x2d, gain, w_up, w_up, conv_w, conv_w, conv_b, conv_b)


def _conv_ffn(x2d, gain, w_up, layer, conv_w, conv_b, w_down, seq):
    act = _ffn_up(x2d, gain, w_up, layer, conv_w, conv_b, seq)
    return _matmul_residual([act], w_down, layer, x2d, seq, "ffn_down", tn_pref=512, rows_inner=True)


def _qkvg_kernel(x_ref, gain_ref, w_ref, hg_ref, o_ref, h_ref, *, n_norm_tiles):
    j = pl.program_id(1)
    n_sub = o_ref.shape[1] // MXU_COLS

    @pl.when(j == 0)
    def _():
        _fill_normed(x_ref, gain_ref, h_ref)

    @pl.when(j < n_norm_tiles)
    def _():
        h = h_ref[...]
        for c in range(n_sub):
            y = _dot_nt(h, w_ref[c * MXU_COLS:(c + 1) * MXU_COLS, :])
            for grp in range(MXU_COLS // LANES):
                src = slice(grp * LANES, (grp + 1) * LANES)
                dst = slice(c * MXU_COLS + grp * LANES, c * MXU_COLS + (grp + 1) * LANES)
                o_ref[:, dst] = _rms_rows(y[:, src], hg_ref[:, dst]).astype(o_ref.dtype)

    @pl.when(j >= n_norm_tiles)
    def _():
        h = h_ref[...]
        for c in range(n_sub):
            cols = slice(c * MXU_COLS, (c + 1) * MXU_COLS)
            o_ref[:, cols] = _dot_nt(h, w_ref[cols, :]).astype(o_ref.dtype)


def _qkvg_proj(x2d, gain, w, layer, head_gain, n_cols, n_norm_cols, seq):
    T, D = x2d.shape
    tm = _tile(seq, 1024)
    tn = _tile(n_norm_cols // 2, 2048, MXU_COLS)
    blocks = [((tm, D), F32), ((1, D), F32), ((D, tn), w.dtype), ((1, tn), F32), ((tm, tn), BF16)]
    return pl.pallas_call(
        functools.partial(_qkvg_kernel, n_norm_tiles=n_norm_cols // tn),
        grid=(T // tm, n_cols // tn),
        in_specs=[
            pl.BlockSpec((tm, D), lambda i, j: (i, 0)),
            pl.BlockSpec((1, D), lambda i, j: (0, 0)),
            pl.BlockSpec((None, tn, D), lambda i, j: (layer, j, 0)),
            pl.BlockSpec((1, tn), lambda i, j: (0, j)),
        ],
        out_specs=[pl.BlockSpec((tm, tn), lambda i, j: (i, j)), pl.BlockSpec((tm, D), lambda i, j: (i, 0))],
        out_shape=[jax.ShapeDtypeStruct((T, n_cols), BF16), jax.ShapeDtypeStruct((T, D), BF16)],
        compiler_params=_params(("parallel", "arbitrary"),
                                _vmem_limit(blocks + [((tm, D), BF16)], [], [((tm, MXU_COLS), F32)] * 4)),
        name="qkvg_proj",
    )(x2d, gain, w, head_gain)


def _fgate_kernel(h_ref, wf_ref, bf_ref, pq_ref, pk_ref, cq_ref, ck_ref, qa_ref, ka_ref,
                  carry_ref, *, n_heads):
    @pl.when(pl.program_id(1) == 0)
    def _():
        carry_ref[...] = jnp.zeros_like(carry_ref)

    tm = h_ref.shape[0]
    f = jnp.dot(h_ref[...], wf_ref[...], preferred_element_type=F32) + bf_ref[...]
    log_f = (jnp.minimum(f, 0.0) - jnp.log1p(jnp.exp(-jnp.abs(f)))) * LOG2E
    c = _cumsum_rows(log_f, _tril_mask(tm).astype(BF16)) + carry_ref[...]
    carry_ref[...] = c[tm - 1:tm, :]
    hi = c.astype(BF16)
    r1 = c - hi.astype(F32)
    mid = r1.astype(BF16)
    lo = (r1 - mid.astype(F32)).astype(BF16)
    lane = lax.broadcasted_iota(jnp.int32, c.shape, 1)
    parts = jnp.where(lane < n_heads, hi, jnp.where(lane < 2 * n_heads, mid, lo))
    qa_ref[...] = (jnp.dot(parts, pq_ref[...], preferred_element_type=F32) + cq_ref[...]).astype(qa_ref.dtype)
    ka_ref[...] = (jnp.dot(parts, pk_ref[...], preferred_element_type=F32) + ck_ref[...]).astype(ka_ref.dtype)


def _fgate_selectors(n_heads):
    W = n_heads * LANES
    pq = np.zeros((LANES, W), np.float32)
    pk = np.zeros((LANES, W), np.float32)
    cq = np.zeros((1, W), np.float32)
    ck = np.zeros((1, W), np.float32)
    for hd in range(n_heads):
        for part in range(3):
            pq[part * n_heads + hd, hd * LANES + part] = 1.0
            pk[part * n_heads + hd, hd * LANES + 3 + part] = -1.0
            cq[0, hd * LANES + 3 + part] = 1.0
            ck[0, hd * LANES + part] = 1.0
    return jnp.asarray(pq, BF16), jnp.asarray(pk, BF16), jnp.asarray(cq), jnp.asarray(ck)


def _fgate(h, w_f, b_f, n_heads, batch, seq):
    T, D = h.shape
    tm = _tile(seq, 512)
    nt = seq // tm
    W = n_heads * LANES
    pq, pk, cq, ck = _fgate_selectors(n_heads)
    blocks = [((tm, D), BF16), ((D, LANES), BF16), ((1, LANES), F32),
              ((LANES, W), BF16), ((LANES, W), BF16), ((1, W), F32), ((1, W), F32),
              ((tm, W), BF16), ((tm, W), BF16)]
    const = lambda b, i: (0, 0)
    return pl.pallas_call(
        functools.partial(_fgate_kernel, n_heads=n_heads),
        grid=(batch, nt),
        in_specs=[
            pl.BlockSpec((tm, D), lambda b, i: (b * nt + i, 0)),
            pl.BlockSpec((D, LANES), const),
            pl.BlockSpec((1, LANES), const),
            pl.BlockSpec((LANES, W), const),
            pl.BlockSpec((LANES, W), const),
            pl.BlockSpec((1, W), const),
            pl.BlockSpec((1, W), const),
        ],
        out_specs=[pl.BlockSpec((tm, W), lambda b, i: (b * nt + i, 0)),
                   pl.BlockSpec((tm, W), lambda b, i: (b * nt + i, 0))],
        out_shape=[jax.ShapeDtypeStruct((T, W), BF16), jax.ShapeDtypeStruct((T, W), BF16)],
        scratch_shapes=[pltpu.VMEM((1, LANES), F32)],
        compiler_params=_params(("parallel", "arbitrary"),
                                _vmem_limit(blocks, [], [((tm, D), F32)] * 2 + [((tm, tm), BF16)])),
        name="fox_forget_cumsum",
    )(h, w_f, b_f, pq, pk, cq, ck)


def _attn_kernel(q_ref, qa_ref, k_ref, ka_ref, v_ref, g_ref, o_ref, qs_ref, acc_ref, m_ref, l_ref):
    qi = pl.program_id(2)
    tq = tk = q_ref.shape[0]
    n_sub = q_ref.shape[1] // LANES
    for hh in range(n_sub):
        lanes = slice(hh * LANES, (hh + 1) * LANES)
        qs_ref[hh] = jnp.concatenate([q_ref[:, lanes], qa_ref[:, lanes]], axis=1)
    def step(first_key, n_keys, diag_offset=None):
        rows = pl.ds(pl.multiple_of(first_key, tk), n_keys)
        for hh in range(n_sub):
            lanes = slice(hh * LANES, (hh + 1) * LANES)
            k = jnp.concatenate([k_ref[rows, lanes], ka_ref[rows, lanes]], axis=1)
            s = _dot_nt(qs_ref[hh], k)
            if diag_offset is not None:
                t_pos = lax.broadcasted_iota(jnp.int32, (tq, n_keys), 0)
                s_pos = lax.broadcasted_iota(jnp.int32, (tq, n_keys), 1) + diag_offset
                s = jnp.where(s_pos <= t_pos, s, -jnp.inf)
                m_new = jnp.broadcast_to(jnp.max(s, axis=1, keepdims=True), (tq, LANES))
            else:
                m_prev = m_ref[hh]
                m_new = jnp.maximum(m_prev, jnp.max(s, axis=1, keepdims=True))
                alpha = jnp.exp2(m_prev - m_new)
            p = jnp.exp2(s - jnp.tile(m_new, (1, n_keys // LANES)))
            l_new = jnp.broadcast_to(jnp.sum(p, axis=1, keepdims=True), (tq, LANES))
            pv = jnp.dot(p.astype(BF16), v_ref[rows, lanes], preferred_element_type=F32)
            m_ref[hh] = m_new
            if diag_offset is not None:
                l_ref[hh], acc_ref[hh] = l_new, pv
            else:
                l_ref[hh] = alpha * l_ref[hh] + l_new
                acc_ref[hh] = alpha * acc_ref[hh] + pv

    def loop(n, body):
        lax.fori_loop(0, n, lambda j, c: (body(j), c)[1], 0)

    @pl.when(qi % 2 == 1)
    def _():
        step((qi - 1) * tk, 2 * tk, -tk)

    @pl.when(qi % 2 == 0)
    def _():
        step(qi * tk, tk, 0)

    n_blocks = k_ref.shape[0] // tk
    if n_blocks > 4:
        loop(qi // 4, lambda j: step(j * 4 * tk, 4 * tk))
    if n_blocks > 2:
        loop((qi % 4) // 2, lambda j: step((qi // 4) * 4 * tk, 2 * tk))
    for hh in range(n_sub):
        lanes = slice(hh * LANES, (hh + 1) * LANES)
        gate = jax.nn.sigmoid(g_ref[:, lanes].astype(F32))
        o_ref[:, lanes] = (acc_ref[hh] / l_ref[hh] * gate).astype(o_ref.dtype)


def _attention(qkvg, qa, ka, n_heads, batch, seq):
    T = qkvg.shape[0]
    tq = _tile(seq, ATTN_BLOCK)
    nq = seq // tq
    width = ATTN_HEADS_PER_STEP * LANES
    ng = n_heads * LANES // width
    blocks = [((tq, width), BF16)] * 2 + [((seq, width), BF16)] * 3 + [((tq, width), BF16)] * 2
    scratch = [((width // LANES, tq, 2 * LANES), BF16)] + [((width // LANES, tq, LANES), F32)] * 3
    return pl.pallas_call(
        _attn_kernel,
        grid=(batch, ng, nq),
        in_specs=[
            pl.BlockSpec((tq, width), lambda b, h, i: (b * nq + i, h)),
            pl.BlockSpec((tq, width), lambda b, h, i: (b * nq + i, h)),
            pl.BlockSpec((seq, width), lambda b, h, i: (b, h + ng)),
            pl.BlockSpec((seq, width), lambda b, h, i: (b, h)),
            pl.BlockSpec((seq, width), lambda b, h, i: (b, h + 2 * ng)),
            pl.BlockSpec((tq, width), lambda b, h, i: (b * nq + i, h + 3 * ng)),
        ],
        out_specs=pl.BlockSpec((tq, width), lambda b, h, i: (b * nq + i, h)),
        out_shape=jax.ShapeDtypeStruct((T, n_heads * LANES), BF16),
        scratch_shapes=[pltpu.VMEM(sh, dt) for sh, dt in scratch],
        compiler_params=_params(("parallel", "parallel", "arbitrary"),
                                _vmem_limit(blocks, scratch, [((tq, 4 * tq), F32)] * 2 * ATTN_HEADS_PER_STEP)),
        name="fox_attention",
    )(qkvg, qa, qkvg, ka, qkvg, qkvg)


def _mixer_ab(x2d, gain, w_in, j, sp_w, sp_b, v_gain, gamma, o_gain, w_out, layer, batch, seq):
    a_width = v_gain.shape[1]
    b_width = gamma.shape[1]
    y_a, h = _gmlp_proj(x2d, gain, w_in, j, v_gain, sp_w, sp_b, seq)
    qfig = _matmul(h, w_in, j, 2 * a_width, 4 * b_width, seq, F32, "hgrn_in_proj")
    y_b = _hgrn(qfig, gamma, o_gain, layer, batch, seq)
    return _matmul_residual([y_a, y_b], w_out, j, x2d, seq, "ab_out_proj", tm_pref=OUT_PROJ_ROWS)


def _mixer_c(x2d, gain, w_in, w_forget, j, b_f, q_gain, k_gain, w_out, batch, seq):
    D = x2d.shape[1]
    n_heads = b_f.shape[1]
    head_dim = q_gain.shape[1]
    assert head_dim == LANES and n_heads * head_dim == D
    scale = head_dim ** -0.5 * LOG2E
    head_gain = jnp.concatenate([jnp.tile(q_gain * scale, (1, n_heads)), jnp.tile(k_gain, (1, n_heads)),
                                 jnp.ones((1, 2 * D), F32)], axis=1)
    qkvg, h = _qkvg_proj(x2d, gain, w_in, j, head_gain, 4 * D, 2 * D, seq)
    assert 3 * n_heads <= LANES
    pad = ((0, 0), (0, LANES - 3 * n_heads))
    w_f = jnp.pad(jnp.tile(w_forget, (1, 3)), pad).astype(BF16)
    b_fp = jnp.pad(jnp.tile(b_f, (1, 3)), pad)
    qa, ka = _fgate(h, w_f, b_fp, n_heads, batch, seq)
    o = _attention(qkvg, qa, ka, n_heads, batch, seq)
    return _matmul_residual([o], w_out, j, x2d, seq, "attn_out_proj", tm_pref=OUT_PROJ_ROWS)


def kernel(x, mix_norm, ab_w_in, ab_sp_w, ab_sp_b, ab_v_norm, hgrn_gamma, hgrn_o_norm, ab_w_out,
           c_w_in, c_b_f, c_q_norm, c_k_norm, c_w_out, ffn_norm, ffn_w_up, ffn_conv_w, ffn_conv_b,
           ffn_w_down):
    batch, seq, D = x.shape
    depth = mix_norm.shape[0]
    x2d = x.reshape(batch * seq, D)
    ab_w_in_b = ab_w_in.astype(BF16)
    c_w_in_t = jnp.swapaxes(c_w_in, 1, 2).astype(BF16)
    ffn_w_down = ffn_w_down.astype(BF16)
    ab_w_out = ab_w_out.astype(BF16)
    c_w_out = c_w_out.astype(BF16)
    for l in range(depth):
        j = l // 2
        gain = mix_norm[l][None, :]
        if l % 2 == 0:
            x2d = _mixer_ab(x2d, gain, ab_w_in_b, j, ab_sp_w[j], ab_sp_b[j], ab_v_norm[j][None, :],
                            hgrn_gamma, hgrn_o_norm[j][None, :], ab_w_out, l, batch, seq)
        else:
            x2d = _mixer_c(x2d, gain, c_w_in_t, c_w_in[j, :, 4 * D:], j, c_b_f[j][None, :],
                           c_q_norm[j][None, :], c_k_norm[j][None, :], c_w_out, batch, seq)
        x2d = _conv_ffn(x2d, ffn_norm[l][None, :], ffn_w_up, l, ffn_conv_w[l],
                        ffn_conv_b[l][None, :], ffn_w_down, seq)
    return x2d.reshape(batch, seq, D)
```

```python
import functools
import math

import jax
import jax.numpy as jnp
import numpy as np
from jax import lax
from jax.experimental import pallas as pl
from jax.experimental.pallas import tpu as pltpu

F32 = jnp.float32
BF16 = jnp.bfloat16
RMS_EPS = 1e-6
LANES = 128
SUBLANES = 8
MXU_COLS = 256
HGRN_CHUNK = 64
HGRN_MAX_FACTORED_RANGE = 60.0
OUT_PROJ_ROWS = 2048
ATTN_BLOCK = 512
ATTN_HEADS_PER_STEP = 4
LOG2E = 1.4426950408889634
V7X_VMEM_BYTES = 64 * 1024 * 1024
VMEM_CAP_BYTES = V7X_VMEM_BYTES - 6 * 1024 * 1024


def _nbytes(shape, dtype):
    return math.prod(shape) * jnp.dtype(dtype).itemsize


def _vmem_limit(blocks, scratch=(), temps=()):
    est = 2 * sum(_nbytes(s, d) for s, d in blocks)
    est += sum(_nbytes(s, d) for s, d in scratch) + sum(_nbytes(s, d) for s, d in temps)
    return int(min(VMEM_CAP_BYTES, est * 5 // 4 + (4 << 20)))


def _tile(n, pref, mult=LANES):
    if n <= pref:
        return n
    t = (pref // mult) * mult
    while n % t:
        t -= mult
    return t


def _params(sem, limit):
    return pltpu.CompilerParams(dimension_semantics=sem, vmem_limit_bytes=limit)


def _rms_rows(xf, gain):
    ms = jnp.mean(xf * xf, axis=-1, keepdims=True)
    return xf * lax.rsqrt(ms + RMS_EPS) * gain


def _gelu(x):
    return 0.5 * x * (1.0 + lax.erf(x * (2.0 ** -0.5)))


def _fill_normed(x_ref, gain_ref, h_ref, rows=256):
    rows = min(rows, x_ref.shape[0])

    def body(r, carry):
        sl = pl.ds(pl.multiple_of(r * rows, rows), rows)
        h_ref[sl, :] = _rms_rows(x_ref[sl, :], gain_ref[...]).astype(h_ref.dtype)
        return carry

    lax.fori_loop(0, x_ref.shape[0] // rows, body, 0)


def _tril_mask(n):
    t = lax.broadcasted_iota(jnp.int32, (n, n), 0)
    s = lax.broadcasted_iota(jnp.int32, (n, n), 1)
    return s <= t


def _cumsum_rows(x, tril_bf16):
    hi = x.astype(BF16)
    r1 = x - hi.astype(F32)
    mid = r1.astype(BF16)
    lo = (r1 - mid.astype(F32)).astype(BF16)
    y = jnp.dot(tril_bf16, jnp.concatenate([hi, mid, lo], axis=1), preferred_element_type=F32)
    d = x.shape[1]
    return y[:, :d] + y[:, d:2 * d] + y[:, 2 * d:]


def _dot_nt(a, b):
    return lax.dot_general(a, b, (((1,), (1,)), ((), ())), preferred_element_type=F32)


def _dot_tn(a, b):
    return lax.dot_general(a, b, (((0,), (0,)), ((), ())), preferred_element_type=F32)


def _gmlp_kernel(x_ref, gain_ref, wu_ref, wv_ref, vg_ref, spw_ref, spb_ref, o_ref, h_ref, *, chunk):
    @pl.when(pl.program_id(1) == 0)
    def _():
        _fill_normed(x_ref, gain_ref, h_ref)

    h = h_ref[...]
    u = _gelu(jnp.dot(h, wu_ref[...].astype(BF16), preferred_element_type=F32))
    v = _gelu(jnp.dot(h, wv_ref[...].astype(BF16), preferred_element_type=F32))
    tm, tn = u.shape
    n_chunks = tm // chunk
    tril = _tril_mask(chunk)
    for hh in range(tn // LANES):
        lanes = slice(hh * LANES, (hh + 1) * LANES)
        vh = _rms_rows(v[:, lanes], vg_ref[:, lanes]).astype(BF16)
        vcat = jnp.concatenate([vh[c * chunk:(c + 1) * chunk, :] for c in range(n_chunks)], axis=1)
        w_causal = jnp.where(tril, spw_ref[hh], 0.0).astype(BF16)
        mixed = jnp.dot(w_causal, vcat, preferred_element_type=F32)
        for c in range(n_chunks):
            rows = slice(c * chunk, (c + 1) * chunk)
            m_c = mixed[:, c * LANES:(c + 1) * LANES] + spb_ref[hh]
            o_ref[rows, lanes] = (u[rows, lanes] * m_c).astype(o_ref.dtype)


def _gmlp_proj(x2d, gain, w_in, layer, v_gain, sp_w, sp_b, seq):
    T, D = x2d.shape
    n_heads, chunk, _ = sp_w.shape
    a_width = n_heads * LANES
    tm = _tile(seq, 1024, chunk)
    tn = _tile(a_width, 1024)
    nj = a_width // tn
    spb = jnp.broadcast_to(sp_b[:, :, None], (n_heads, chunk, LANES))
    blocks = [((tm, D), F32), ((1, D), F32), ((D, tn), w_in.dtype), ((D, tn), w_in.dtype), ((1, tn), F32),
              ((tn // LANES, chunk, chunk), F32), ((tn // LANES, chunk, LANES), F32), ((tm, tn), BF16)]
    return pl.pallas_call(
        functools.partial(_gmlp_kernel, chunk=chunk),
        grid=(T // tm, nj),
        in_specs=[
            pl.BlockSpec((tm, D), lambda i, j: (i, 0)),
            pl.BlockSpec((1, D), lambda i, j: (0, 0)),
            pl.BlockSpec((None, D, tn), lambda i, j: (layer, 0, j)),
            pl.BlockSpec((None, D, tn), lambda i, j: (layer, 0, j + nj)),
            pl.BlockSpec((1, tn), lambda i, j: (0, j)),
            pl.BlockSpec((tn // LANES, chunk, chunk), lambda i, j: (j, 0, 0)),
            pl.BlockSpec((tn // LANES, chunk, LANES), lambda i, j: (j, 0, 0)),
        ],
        out_specs=[pl.BlockSpec((tm, tn), lambda i, j: (i, j)), pl.BlockSpec((tm, D), lambda i, j: (i, 0))],
        out_shape=[jax.ShapeDtypeStruct((T, a_width), BF16), jax.ShapeDtypeStruct((T, D), BF16)],
        compiler_params=_params(("parallel", "arbitrary"),
                                _vmem_limit(blocks + [((tm, D), BF16)], [], [((tm, tn), F32)] * 6)),
        name="gmlp_proj",
    )(x2d, gain, w_in, w_in, v_gain, sp_w, spb)


def _mm_kernel(h_ref, w_ref, o_ref):
    o_ref[...] = jnp.dot(h_ref[...], w_ref[...].astype(BF16), preferred_element_type=F32).astype(o_ref.dtype)


def _matmul(h, w, layer, col0, n_cols, seq, out_dtype, name):
    T, D = h.shape
    tm = _tile(seq, 1024)
    tn = _tile(n_cols, 2048)
    assert col0 % tn == 0
    j0 = col0 // tn
    blocks = [((tm, D), BF16), ((D, tn), w.dtype), ((tm, tn), out_dtype)]
    return pl.pallas_call(
        _mm_kernel,
        grid=(T // tm, n_cols // tn),
        in_specs=[
            pl.BlockSpec((tm, D), lambda i, j: (i, 0)),
            pl.BlockSpec((None, D, tn), lambda i, j: (layer, 0, j + j0)),
        ],
        out_specs=pl.BlockSpec((tm, tn), lambda i, j: (i, j)),
        out_shape=jax.ShapeDtypeStruct((T, n_cols), out_dtype),
        compiler_params=_params(("parallel", "arbitrary"), _vmem_limit(blocks, [], [((tm, tn), F32)])),
        name=name,
    )(h, w)


def _hgrn_kernel(gamma_ref, q_ref, f_ref, i_ref, g_ref, og_ref, o_ref, st_ref, g_scr, k_scr, intra_scr, inter_scr,
                 *, layer, chunk):
    @pl.when(pl.program_id(2) == 0)
    def _():
        st_ref[...] = jnp.zeros_like(st_ref)

    gam = gamma_ref[...]
    ex = jnp.exp(gam - jnp.max(gam, axis=0, keepdims=True))
    lb_all = jnp.sum(ex[:layer + 1], axis=0, keepdims=True) / jnp.sum(ex, axis=0, keepdims=True)

    L, width = q_ref.shape
    n_chunks = L // chunk
    mid = chunk // 2 - 1
    row_chunk = lax.broadcasted_iota(jnp.int32, (L, L), 0) // chunk
    col_chunk = lax.broadcasted_iota(jnp.int32, (L, L), 1) // chunk
    mask = _tril_mask(L) & (row_chunk == col_chunk)
    chunk_of_row = lax.broadcasted_iota(jnp.int32, (L, LANES), 0) // chunk

    fl = f_ref[...]
    e = jnp.exp(-jnp.abs(fl))
    r = 1.0 / (1.0 + e)
    pos = fl >= 0
    sig = jnp.where(pos, r, e * r)
    nsig = jnp.where(pos, e * r, r)
    kk = (1.0 - lb_all) * nsig
    G = _cumsum_rows(jnp.log(lb_all + (1.0 - lb_all) * sig), mask.astype(BF16))
    g_scr[...] = G
    k_scr[...] = kk
    g_mid_rows = [G[c * chunk + mid:c * chunk + mid + 1, :] for c in range(n_chunks)]
    g_end_rows = [G[(c + 1) * chunk - 1:(c + 1) * chunk, :] for c in range(n_chunks)]
    per_row = lambda rows: jnp.concatenate([jnp.broadcast_to(x, (chunk, width)) for x in rows], axis=0)
    g_mid = per_row(g_mid_rows)
    g_end = per_row(g_end_rows)
    qv = q_ref[...]
    q_t = (qv * jnp.exp(G - g_mid)).astype(BF16)
    k_t = (kk * jnp.exp(g_mid - G)).astype(BF16)
    q_g = (qv * jnp.exp(G)).astype(BF16)
    k_end = (kk * jnp.exp(g_end - G)).astype(BF16)
    vv = i_ref[...].astype(BF16)
    decay_range = -jnp.min(jnp.concatenate(g_end_rows, axis=0))
    factorable = decay_range <= HGRN_MAX_FACTORED_RANGE

    def finish(o, lanes):
        gv = g_ref[:, lanes]
        o_ref[:, lanes] = (_rms_rows(o, og_ref[...]) * (gv * jax.nn.sigmoid(gv))).astype(o_ref.dtype)

    for hh in range(width // LANES):
        lanes = slice(hh * LANES, (hh + 1) * LANES)
        scores = jnp.where(mask, _dot_nt(q_t[:, lanes], k_t[:, lanes]), 0.0).astype(BF16)
        intra = jnp.dot(scores, vv[:, lanes], preferred_element_type=F32)
        v_exp = jnp.concatenate([jnp.where(chunk_of_row == c, vv[:, lanes], 0) for c in range(n_chunks)],
                                axis=1)
        u_t = _dot_tn(v_exp, k_end[:, lanes])
        st = st_ref[hh]
        prev = []
        for c in range(n_chunks):
            prev.append(st.astype(BF16))
            st = st * jnp.exp(g_end_rows[c][:, lanes]) + u_t[c * LANES:(c + 1) * LANES, :]
        st_ref[hh] = st
        q_exp = jnp.concatenate([jnp.where(chunk_of_row == c, q_g[:, lanes], 0) for c in range(n_chunks)],
                                axis=1)
        inter = _dot_nt(q_exp, jnp.concatenate(prev, axis=1))
        inter_scr[hh] = inter
        finish(intra + inter, lanes)

    @pl.when(jnp.logical_not(factorable))
    def _():
        for hh in range(width // LANES):
            _hgrn_intra_pairwise(q_ref, i_ref, g_scr, k_scr, intra_scr, hh, chunk)
            finish(intra_scr[hh] + inter_scr[hh], slice(hh * LANES, (hh + 1) * LANES))


def _hgrn_intra_pairwise(q_ref, i_ref, g_scr, k_scr, intra_scr, hh, chunk):
    lanes = slice(hh * LANES, (hh + 1) * LANES)
    t_idx = lax.broadcasted_iota(jnp.int32, (chunk, 1), 0)
    for c in range(q_ref.shape[0] // chunk):
        rows = slice(c * chunk, (c + 1) * chunk)
        g_c = g_scr[rows, lanes]
        q_c = q_ref[rows, lanes]

        def body(grp, acc):
            keys = pl.ds(pl.multiple_of(c * chunk + grp * SUBLANES, SUBLANES), SUBLANES)
            g_s, k_s, v_s = g_scr[keys, lanes], k_scr[keys, lanes], i_ref[keys, lanes]
            for j in range(SUBLANES):
                w = q_c * k_s[j:j + 1] * jnp.exp(jnp.minimum(g_c - g_s[j:j + 1], 0.0))
                score = jnp.where(t_idx >= grp * SUBLANES + j, jnp.sum(w, axis=1, keepdims=True), 0.0)
                acc = acc + score * v_s[j:j + 1]
            return acc

        intra_scr[hh, rows, :] = lax.fori_loop(0, chunk // SUBLANES, body, jnp.zeros((chunk, LANES), F32))


def _hgrn(qfig, gamma, o_gain, layer, batch, seq):
    T = qfig.shape[0]
    b_width = gamma.shape[1]
    n_layers = gamma.shape[0]
    width = _tile(b_width, 1024)
    L = _tile(seq, 256, HGRN_CHUNK)
    nw = b_width // width
    nl = seq // L
    blocks = [((L, width), F32)] * 4 + [((L, width), BF16), ((n_layers, width), F32)]
    n_sub = width // LANES
    scratch = [((n_sub, LANES, LANES), F32), ((L, width), F32), ((L, width), F32), ((n_sub, L, LANES), F32),
               ((n_sub, L, LANES), F32)]
    row = lambda b, h, l: b * nl + l
    return pl.pallas_call(
        functools.partial(_hgrn_kernel, layer=layer, chunk=HGRN_CHUNK),
        grid=(batch, nw, nl),
        in_specs=[
            pl.BlockSpec((n_layers, width), lambda b, h, l: (0, h)),
            pl.BlockSpec((L, width), lambda b, h, l: (row(b, h, l), h)),
            pl.BlockSpec((L, width), lambda b, h, l: (row(b, h, l), h + nw)),
            pl.BlockSpec((L, width), lambda b, h, l: (row(b, h, l), h + 2 * nw)),
            pl.BlockSpec((L, width), lambda b, h, l: (row(b, h, l), h + 3 * nw)),
            pl.BlockSpec((1, LANES), lambda b, h, l: (0, 0)),
        ],
        out_specs=pl.BlockSpec((L, width), lambda b, h, l: (row(b, h, l), h)),
        out_shape=jax.ShapeDtypeStruct((T, b_width), BF16),
        scratch_shapes=[pltpu.VMEM(sh, dt) for sh, dt in scratch],
        compiler_params=_params(("parallel", "parallel", "arbitrary"),
                                _vmem_limit(blocks, scratch, [((L, width), F32)] * 12)),
        name="hgrn2",
    )(gamma, qfig, qfig, qfig, qfig, o_gain)


def _mm_resid_kernel(*refs, n_lhs):
    lhs_refs = refs[:n_lhs]
    w_ref, r_ref, o_ref = refs[n_lhs:]
    acc = r_ref[...]
    k0 = 0
    for a_ref in lhs_refs:
        k = a_ref.shape[1]
        acc = acc + jnp.dot(a_ref[...], w_ref[k0:k0 + k, :].astype(BF16), preferred_element_type=F32)
        k0 += k
    o_ref[...] = acc


def _matmul_residual(lhs_list, w, layer, resid, seq, name, tm_pref=1024, tn_pref=512, rows_inner=False):
    T, N = resid.shape
    K = w.shape[1]
    tm = _tile(seq, tm_pref)
    tn = _tile(N, tn_pref)
    blocks = [((tm, a.shape[1]), BF16) for a in lhs_list] + [((K, tn), w.dtype), ((tm, tn), F32), ((tm, tn), F32)]
    grid = (N // tn, T // tm) if rows_inner else (T // tm, N // tn)
    ij = (lambda g0, g1: (g1, g0)) if rows_inner else (lambda g0, g1: (g0, g1))
    return pl.pallas_call(
        functools.partial(_mm_resid_kernel, n_lhs=len(lhs_list)),
        grid=grid,
        in_specs=[pl.BlockSpec((tm, a.shape[1]), lambda g0, g1: (ij(g0, g1)[0], 0)) for a in lhs_list] + [
            pl.BlockSpec((None, K, tn), lambda g0, g1: (layer, 0, ij(g0, g1)[1])),
            pl.BlockSpec((tm, tn), lambda g0, g1: ij(g0, g1)),
        ],
        out_specs=pl.BlockSpec((tm, tn), lambda g0, g1: ij(g0, g1)),
        out_shape=jax.ShapeDtypeStruct((T, N), F32),
        compiler_params=_params(("parallel", "arbitrary"), _vmem_limit(blocks, [], [((tm, tn), F32)])),
        name=name,
    )(*lhs_list, w, resid)


def _ffn_up_kernel(x_ref, xh_ref, gain_ref, wa_ref, wb_ref, cwa_ref, cwb_ref, cba_ref, cbb_ref,
                   o_ref, h_ref, za_ref, *, n_tiles, n_col_tiles, tiles_per_seq, halo):
    s = pl.program_id(0)
    cur = jnp.minimum(s, n_tiles - 1)
    prev = jnp.maximum(s - 1, 0)
    row_c = cur // n_col_tiles
    tm = x_ref.shape[0]

    @pl.when(s == 0)
    def _():
        za_ref[...] = jnp.zeros_like(za_ref)

    @pl.when((cur % n_col_tiles == 0) & (s < n_tiles))
    def _():
        slot = row_c % 2
        keep = (row_c % tiles_per_seq != 0).astype(F32)
        h_ref[slot, 0:halo, :] = (_rms_rows(xh_ref[...], gain_ref[...]) * keep).astype(h_ref.dtype)
        rows = min(256, tm)

        def body(r, carry):
            src = pl.ds(pl.multiple_of(r * rows, rows), rows)
            dst = pl.ds(pl.multiple_of(r * rows + halo, halo), rows)
            h_ref[slot, dst, :] = _rms_rows(x_ref[src, :], gain_ref[...]).astype(h_ref.dtype)
            return carry

        lax.fori_loop(0, tm // rows, body, 0)

    def conv(z, cw_ref, cb_ref):
        z1 = pltpu.roll(z, 1, 0)
        z2 = pltpu.roll(z, 2, 0)
        y = cw_ref[0:1, :] * z2 + cw_ref[1:2, :] * z1 + cw_ref[2:3, :] * z + cb_ref[...]
        return y[halo:, :]

    h_prev = h_ref[(prev // n_col_tiles) % 2]
    zb = jnp.dot(h_prev, wb_ref[...].astype(BF16), preferred_element_type=F32)
    a = conv(za_ref[...], cwa_ref, cba_ref)
    o_ref[...] = (a * jax.nn.sigmoid(a) * conv(zb, cwb_ref, cbb_ref)).astype(o_ref.dtype)
    za_ref[...] = jnp.dot(h_ref[row_c % 2], wa_ref[...].astype(BF16), preferred_element_type=F32)


def _ffn_up(x2d, gain, w_up, layer, conv_w, conv_b, seq):
    T, D = x2d.shape
    F = w_up.shape[2] // 2
    halo = 16
    tm = _tile(seq, 1024)
    tn = _tile(F, 512, MXU_COLS)
    nj = F // tn
    n_tiles = (T // tm) * nj
    tiles_per_seq = seq // tm
    blocks = [((tm, D), F32), ((halo, D), F32), ((1, D), F32), ((D, tn), F32), ((D, tn), F32),
              ((3, tn), F32), ((3, tn), F32), ((1, tn), F32), ((1, tn), F32), ((tm, tn), BF16)]
    scratch = [((2, tm + halo, D), BF16), ((tm + halo, tn), F32)]
    hpt = tm // halo
    cur = lambda s: jnp.minimum(s, n_tiles - 1)
    prev = lambda s: jnp.maximum(s - 1, 0)
    return pl.pallas_call(
        functools.partial(_ffn_up_kernel, n_tiles=n_tiles, n_col_tiles=nj, tiles_per_seq=tiles_per_seq,
                          halo=halo),
        grid=(n_tiles + 1,),
        in_specs=[
            pl.BlockSpec((tm, D), lambda s: (cur(s) // nj, 0)),
            pl.BlockSpec((halo, D), lambda s: (jnp.maximum(cur(s) // nj * hpt - 1, 0), 0)),
            pl.BlockSpec((1, D), lambda s: (0, 0)),
            pl.BlockSpec((None, D, tn), lambda s: (layer, 0, cur(s) % nj)),
            pl.BlockSpec((None, D, tn), lambda s: (layer, 0, prev(s) % nj + nj)),
            pl.BlockSpec((3, tn), lambda s: (0, prev(s) % nj)),
            pl.BlockSpec((3, tn), lambda s: (0, prev(s) % nj + nj)),
            pl.BlockSpec((1, tn), lambda s: (0, prev(s) % nj)),
            pl.BlockSpec((1, tn), lambda s: (0, prev(s) % nj + nj)),
        ],
        out_specs=pl.BlockSpec((tm, tn), lambda s: (prev(s) // nj, prev(s) % nj)),
        out_shape=jax.ShapeDtypeStruct((T, F), BF16),
        scratch_shapes=[pltpu.VMEM(sh, dt) for sh, dt in scratch],
        compiler_params=_params(("arbitrary",),
                                _vmem_limit(blocks, scratch, [((tm + halo, MXU_COLS), F32)] * 8)),
        name="ffn_up_conv_gate",
    )(x2d, x2d, gain, w_up, w_up, conv_w, conv_w, conv_b, conv_b)


def _conv_ffn(x2d, gain, w_up, layer, conv_w, conv_b, w_down, seq):
    act = _ffn_up(x2d, gain, w_up, layer, conv_w, conv_b, seq)
    return _matmul_residual([act], w_down, layer, x2d, seq, "ffn_down", tn_pref=512, rows_inner=True)


def _qkvg_kernel(x_ref, gain_ref, w_ref, hg_ref, o_ref, h_ref, *, n_norm_tiles):
    j = pl.program_id(1)
    n_sub = o_ref.shape[1] // MXU_COLS

    @pl.when(j == 0)
    def _():
        _fill_normed(x_ref, gain_ref, h_ref)

    @pl.when(j < n_norm_tiles)
    def _():
        h = h_ref[...]
        for c in range(n_sub):
            y = _dot_nt(h, w_ref[c * MXU_COLS:(c + 1) * MXU_COLS, :])
            for grp in range(MXU_COLS // LANES):
                src = slice(grp * LANES, (grp + 1) * LANES)
                dst = slice(c * MXU_COLS + grp * LANES, c * MXU_COLS + (grp + 1) * LANES)
                o_ref[:, dst] = _rms_rows(y[:, src], hg_ref[:, dst]).astype(o_ref.dtype)

    @pl.when(j >= n_norm_tiles)
    def _():
        h = h_ref[...]
        for c in range(n_sub):
            cols = slice(c * MXU_COLS, (c + 1) * MXU_COLS)
            o_ref[:, cols] = _dot_nt(h, w_ref[cols, :]).astype(o_ref.dtype)


def _qkvg_proj(x2d, gain, w, layer, head_gain, n_cols, n_norm_cols, seq):
    T, D = x2d.shape
    tm = _tile(seq, 1024)
    tn = _tile(n_norm_cols // 2, 2048, MXU_COLS)
    blocks = [((tm, D), F32), ((1, D), F32), ((D, tn), w.dtype), ((1, tn), F32), ((tm, tn), BF16)]
    return pl.pallas_call(
        functools.partial(_qkvg_kernel, n_norm_tiles=n_norm_cols // tn),
        grid=(T // tm, n_cols // tn),
        in_specs=[
            pl.BlockSpec((tm, D), lambda i, j: (i, 0)),
            pl.BlockSpec((1, D), lambda i, j: (0, 0)),
            pl.BlockSpec((None, tn, D), lambda i, j: (layer, j, 0)),
            pl.BlockSpec((1, tn), lambda i, j: (0, j)),
        ],
        out_specs=[pl.BlockSpec((tm, tn), lambda i, j: (i, j)), pl.BlockSpec((tm, D), lambda i, j: (i, 0))],
        out_shape=[jax.ShapeDtypeStruct((T, n_cols), BF16), jax.ShapeDtypeStruct((T, D), BF16)],
        compiler_params=_params(("parallel", "arbitrary"),
                                _vmem_limit(blocks + [((tm, D), BF16)], [], [((tm, MXU_COLS), F32)] * 4)),
        name="qkvg_proj",
    )(x2d, gain, w, head_gain)


def _fgate_kernel(h_ref, wf_ref, bf_ref, pq_ref, pk_ref, cq_ref, ck_ref, qa_ref, ka_ref,
                  carry_ref, *, n_heads):
    @pl.when(pl.program_id(1) == 0)
    def _():
        carry_ref[...] = jnp.zeros_like(carry_ref)

    tm = h_ref.shape[0]
    f = jnp.dot(h_ref[...], wf_ref[...], preferred_element_type=F32) + bf_ref[...]
    log_f = (jnp.minimum(f, 0.0) - jnp.log1p(jnp.exp(-jnp.abs(f)))) * LOG2E
    c = _cumsum_rows(log_f, _tril_mask(tm).astype(BF16)) + carry_ref[...]
    carry_ref[...] = c[tm - 1:tm, :]
    hi = c.astype(BF16)
    r1 = c - hi.astype(F32)
    mid = r1.astype(BF16)
    lo = (r1 - mid.astype(F32)).astype(BF16)
    lane = lax.broadcasted_iota(jnp.int32, c.shape, 1)
    parts = jnp.where(lane < n_heads, hi, jnp.where(lane < 2 * n_heads, mid, lo))
    qa_ref[...] = (jnp.dot(parts, pq_ref[...], preferred_element_type=F32) + cq_ref[...]).astype(qa_ref.dtype)
    ka_ref[...] = (jnp.dot(parts, pk_ref[...], preferred_element_type=F32) + ck_ref[...]).astype(ka_ref.dtype)


def _fgate_selectors(n_heads):
    W = n_heads * LANES
    pq = np.zeros((LANES, W), np.float32)
    pk = np.zeros((LANES, W), np.float32)
    cq = np.zeros((1, W), np.float32)
    ck = np.zeros((1, W), np.float32)
    for hd in range(n_heads):
        for part in range(3):
            pq[part * n_heads + hd, hd * LANES + part] = 1.0
            pk[part * n_heads + hd, hd * LANES + 3 + part] = -1.0
            cq[0, hd * LANES + 3 + part] = 1.0
            ck[0, hd * LANES + part] = 1.0
    return jnp.asarray(pq, BF16), jnp.asarray(pk, BF16), jnp.asarray(cq), jnp.asarray(ck)


def _fgate(h, w_f, b_f, n_heads, batch, seq):
    T, D = h.shape
    tm = _tile(seq, 512)
    nt = seq // tm
    W = n_heads * LANES
    pq, pk, cq, ck = _fgate_selectors(n_heads)
    blocks = [((tm, D), BF16), ((D, LANES), BF16), ((1, LANES), F32),
              ((LANES, W), BF16), ((LANES, W), BF16), ((1, W), F32), ((1, W), F32),
              ((tm, W), BF16), ((tm, W), BF16)]
    const = lambda b, i: (0, 0)
    return pl.pallas_call(
        functools.partial(_fgate_kernel, n_heads=n_heads),
        grid=(batch, nt),
        in_specs=[
            pl.BlockSpec((tm, D), lambda b, i: (b * nt + i, 0)),
            pl.BlockSpec((D, LANES), const),
            pl.BlockSpec((1, LANES), const),
            pl.BlockSpec((LANES, W), const),
            pl.BlockSpec((LANES, W), const),
            pl.BlockSpec((1, W), const),
            pl.BlockSpec((1, W), const),
        ],
        out_specs=[pl.BlockSpec((tm, W), lambda b, i: (b * nt + i, 0)),
                   pl.BlockSpec((tm, W), lambda b, i: (b * nt + i, 0))],
        out_shape=[jax.ShapeDtypeStruct((T, W), BF16), jax.ShapeDtypeStruct((T, W), BF16)],
        scratch_shapes=[pltpu.VMEM((1, LANES), F32)],
        compiler_params=_params(("parallel", "arbitrary"),
                                _vmem_limit(blocks, [], [((tm, D), F32)] * 2 + [((tm, tm), BF16)])),
        name="fox_forget_cumsum",
    )(h, w_f, b_f, pq, pk, cq, ck)


def _attn_kernel(q_ref, qa_ref, k_ref, ka_ref, v_ref, g_ref, o_ref, qs_ref, acc_ref, m_ref, l_ref):
    qi = pl.program_id(2)
    tq = tk = q_ref.shape[0]
    n_sub = q_ref.shape[1] // LANES
    for hh in range(n_sub):
        lanes = slice(hh * LANES, (hh + 1) * LANES)
        qs_ref[hh] = jnp.concatenate([q_ref[:, lanes], qa_ref[:, lanes]], axis=1)
    def step(first_key, n_keys, diag_offset=None):
        rows = pl.ds(pl.multiple_of(first_key, tk), n_keys)
        for hh in range(n_sub):
            lanes = slice(hh * LANES, (hh + 1) * LANES)
            k = jnp.concatenate([k_ref[rows, lanes], ka_ref[rows, lanes]], axis=1)
            s = _dot_nt(qs_ref[hh], k)
            if diag_offset is not None:
                t_pos = lax.broadcasted_iota(jnp.int32, (tq, n_keys), 0)
                s_pos = lax.broadcasted_iota(jnp.int32, (tq, n_keys), 1) + diag_offset
                s = jnp.where(s_pos <= t_pos, s, -jnp.inf)
                m_new = jnp.broadcast_to(jnp.max(s, axis=1, keepdims=True), (tq, LANES))
            else:
                m_prev = m_ref[hh]
                m_new = jnp.maximum(m_prev, jnp.max(s, axis=1, keepdims=True))
                alpha = jnp.exp2(m_prev - m_new)
            l_new = jnp.zeros((tq, LANES), F32)
            pv = jnp.zeros((tq, LANES), F32)
            pk = min(n_keys, 2 * tk)
            for c in range(n_keys // pk):
                p = jnp.exp2(s[:, c * pk:(c + 1) * pk] - jnp.tile(m_new, (1, pk // LANES)))
                l_new = l_new + jnp.sum(p, axis=1, keepdims=True)
                piece = pl.ds(pl.multiple_of(first_key + c * pk, tk), pk)
                pv = pv + jnp.dot(p.astype(BF16), v_ref[piece, lanes], preferred_element_type=F32)
            m_ref[hh] = m_new
            if diag_offset is not None:
                l_ref[hh], acc_ref[hh] = l_new, pv
            else:
                l_ref[hh] = alpha * l_ref[hh] + l_new
                acc_ref[hh] = alpha * acc_ref[hh] + pv

    def loop(n, body):
        lax.fori_loop(0, n, lambda j, c: (body(j), c)[1], 0)

    @pl.when(qi % 2 == 1)
    def _():
        step((qi - 1) * tk, 2 * tk, -tk)

    @pl.when(qi % 2 == 0)
    def _():
        step(qi * tk, tk, 0)

    n_blocks = k_ref.shape[0] // tk
    if n_blocks > 4:
        loop(qi // 4, lambda j: step(j * 4 * tk, 4 * tk))
    if n_blocks > 2:
        loop((qi % 4) // 2, lambda j: step((qi // 4) * 4 * tk, 2 * tk))
    for hh in range(n_sub):
        lanes = slice(hh * LANES, (hh + 1) * LANES)
        gate = jax.nn.sigmoid(g_ref[:, lanes].astype(F32))
        o_ref[:, lanes] = (acc_ref[hh] / l_ref[hh] * gate).astype(o_ref.dtype)


def _attention(qkvg, qa, ka, n_heads, batch, seq):
    T = qkvg.shape[0]
    tq = _tile(seq, ATTN_BLOCK)
    nq = seq // tq
    width = ATTN_HEADS_PER_STEP * LANES
    ng = n_heads * LANES // width
    blocks = [((tq, width), BF16)] * 2 + [((seq, width), BF16)] * 3 + [((tq, width), BF16)] * 2
    scratch = [((width // LANES, tq, 2 * LANES), BF16)] + [((width // LANES, tq, LANES), F32)] * 3
    return pl.pallas_call(
        _attn_kernel,
        grid=(batch, ng, nq),
        in_specs=[
            pl.BlockSpec((tq, width), lambda b, h, i: (b * nq + i, h)),
            pl.BlockSpec((tq, width), lambda b, h, i: (b * nq + i, h)),
            pl.BlockSpec((seq, width), lambda b, h, i: (b, h + ng)),
            pl.BlockSpec((seq, width), lambda b, h, i: (b, h)),
            pl.BlockSpec((seq, width), lambda b, h, i: (b, h + 2 * ng)),
            pl.BlockSpec((tq, width), lambda b, h, i: (b * nq + i, h + 3 * ng)),
        ],
        out_specs=pl.BlockSpec((tq, width), lambda b, h, i: (b * nq + i, h)),
        out_shape=jax.ShapeDtypeStruct((T, n_heads * LANES), BF16),
        scratch_shapes=[pltpu.VMEM(sh, dt) for sh, dt in scratch],
        compiler_params=_params(("parallel", "parallel", "arbitrary"),
                                _vmem_limit(blocks, scratch, [((tq, 4 * tq), F32)] * 2 * ATTN_HEADS_PER_STEP)),
        name="fox_attention",
    )(qkvg, qa, qkvg, ka, qkvg, qkvg)


def _mixer_ab(x2d, gain, w_in, j, sp_w, sp_b, v_gain, gamma, o_gain, w_out, layer, batch, seq):
    a_width = v_gain.shape[1]
    b_width = gamma.shape[1]
    y_a, h = _gmlp_proj(x2d, gain, w_in, j, v_gain, sp_w, sp_b, seq)
    qfig = _matmul(h, w_in, j, 2 * a_width, 4 * b_width, seq, F32, "hgrn_in_proj")
    y_b = _hgrn(qfig, gamma, o_gain, layer, batch, seq)
    return _matmul_residual([y_a, y_b], w_out, j, x2d, seq, "ab_out_proj", tm_pref=OUT_PROJ_ROWS)


def _mixer_c(x2d, gain, w_in, w_forget, j, b_f, q_gain, k_gain, w_out, batch, seq):
    D = x2d.shape[1]
    n_heads = b_f.shape[1]
    head_dim = q_gain.shape[1]
    assert head_dim == LANES and n_heads * head_dim == D
    scale = head_dim ** -0.5 * LOG2E
    head_gain = jnp.concatenate([jnp.tile(q_gain * scale, (1, n_heads)), jnp.tile(k_gain, (1, n_heads)),
                                 jnp.ones((1, 2 * D), F32)], axis=1)
    qkvg, h = _qkvg_proj(x2d, gain, w_in, j, head_gain, 4 * D, 2 * D, seq)
    assert 3 * n_heads <= LANES
    pad = ((0, 0), (0, LANES - 3 * n_heads))
    w_f = jnp.pad(jnp.tile(w_forget, (1, 3)), pad).astype(BF16)
    b_fp = jnp.pad(jnp.tile(b_f, (1, 3)), pad)
    qa, ka = _fgate(h, w_f, b_fp, n_heads, batch, seq)
    o = _attention(qkvg, qa, ka, n_heads, batch, seq)
    return _matmul_residual([o], w_out, j, x2d, seq, "attn_out_proj", tm_pref=OUT_PROJ_ROWS)


def kernel(x, mix_norm, ab_w_in, ab_sp_w, ab_sp_b, ab_v_norm, hgrn_gamma, hgrn_o_norm, ab_w_out,
           c_w_in, c_b_f, c_q_norm, c_k_norm, c_w_out, ffn_norm, ffn_w_up, ffn_conv_w, ffn_conv_b,
           ffn_w_down):
    batch, seq, D = x.shape
    depth = mix_norm.shape[0]
    x2d = x.reshape(batch * seq, D)
    ab_w_in_b = ab_w_in.astype(BF16)
    c_w_in_t = jnp.swapaxes(c_w_in, 1, 2).astype(BF16)
    ffn_w_down = ffn_w_down.astype(BF16)
    ab_w_out = ab_w_out.astype(BF16)
    c_w_out = c_w_out.astype(BF16)
    for l in range(depth):
        j = l // 2
        gain = mix_norm[l][None, :]
        if l % 2 == 0:
            x2d = _mixer_ab(x2d, gain, ab_w_in_b, j, ab_sp_w[j], ab_sp_b[j], ab_v_norm[j][None, :],
                            hgrn_gamma, hgrn_o_norm[j][None, :], ab_w_out, l, batch, seq)
        else:
            x2d = _mixer_c(x2d, gain, c_w_in_t, c_w_in[j, :, 4 * D:], j, c_b_f[j][None, :],
                           c_q_norm[j][None, :], c_k_norm[j][None, :], c_w_out, batch, seq)
        x2d = _conv_ffn(x2d, ffn_norm[l][None, :], ffn_w_up, l, ffn_conv_w[l],
                        ffn_conv_b[l][None, :], ffn_w_down, seq)
    return x2d.reshape(batch, seq, D)
```
